```python
import jax, jax.numpy as jnp
from jax import lax
import numpy as np

D_MODEL = 1024
BATCH = 4
SEQ = 4096
DEPTH = 2

GRID_W = 64
CTX_LEN = 256

N_Q_HEADS = 8
N_KV_HEADS = 2
HEAD_DIM = 64
Q_GROUP = N_Q_HEADS // N_KV_HEADS
ATTN_W = N_Q_HEADS * HEAD_DIM
KV_W = N_KV_HEADS * HEAD_DIM
ROPE_THETA = 10000.0
Q_BLOCK = 128
LRU_W = D_MODEL // 4
LRU_BLOCKS = 4
LRU_BW = LRU_W // LRU_BLOCKS
CONV_W = 4
LRU_C = 8.0
POOL_W = D_MODEL // 4
POOL_GROUPS = 4
POOL_GW = POOL_W // POOL_GROUPS
POOL_WINDOWS = (2, 4, 8, 16)
MIX_W = ATTN_W + LRU_W + POOL_W
IN_W = ATTN_W + 2 * KV_W + 2 * LRU_W + POOL_W
SPLITS = (ATTN_W, ATTN_W + KV_W, ATTN_W + 2 * KV_W, ATTN_W + 2 * KV_W + LRU_W,
          ATTN_W + 2 * KV_W + 2 * LRU_W)
FFN_HIDDEN = -(-8 * D_MODEL // 768) * 256
RMS_EPS = 1e-6

kernel_name = "hybrid_headgroup_diffusion_block"


def rmsnorm(x, g):
    xf = x.astype(jnp.float32)
    y = xf * lax.rsqrt(jnp.mean(xf * xf, axis=-1, keepdims=True) + RMS_EPS)
    return (y * g.astype(jnp.float32)).astype(x.dtype)


def modulate(h, shift, scale):
    return h * (1 + scale) + shift


def rope_1d(x, pos):
    half = x.shape[-1] // 2
    freq = ROPE_THETA ** (-jnp.arange(half, dtype=jnp.float32) / half)
    ang = pos.astype(jnp.float32)[:, None] * freq
    cos = jnp.cos(ang)[:, None, :]
    sin = jnp.sin(ang)[:, None, :]
    xf = x.astype(jnp.float32)
    x1, x2 = xf[..., :half], xf[..., half:]
    return jnp.concatenate([x1 * cos - x2 * sin, x2 * cos + x1 * sin], axis=-1).astype(x.dtype)


def axial_rope(x):
    L = x.shape[1]
    rows = L // GRID_W
    row = jnp.repeat(jnp.arange(rows), GRID_W)
    col = jnp.tile(jnp.arange(GRID_W), rows)
    h = HEAD_DIM // 2
    return jnp.concatenate([rope_1d(x[..., :h], row), rope_1d(x[..., h:], col)], axis=-1)


def attend(q, k, v):
    s = jnp.einsum('bqkgd,bskd->bkgqs', q, k).astype(jnp.float32) * (HEAD_DIM ** -0.5)
    p = jax.nn.softmax(s, axis=-1).astype(v.dtype)
    return jnp.einsum('bkgqs,bskd->bqkgd', p, v)


def centred_dwconv(x, w, b):
    L = x.shape[1]
    left = CONV_W // 2
    right = CONV_W - 1 - left
    xp = jnp.pad(x, ((0, 0), (left, right), (0, 0)))
    return b + sum(xp[:, k:k + L] * w[k] for k in range(CONV_W))


def rglru(x, w, b, lam, h0):
    B_, L, W = x.shape
    xb = x.reshape(B_, L, LRU_BLOCKS, LRU_BW)
    g = jnp.einsum('blnc,gncd->gblnd', xb, w).reshape(2, B_, L, W) + b[:, None, None, :]
    g = jax.nn.sigmoid(g.astype(jnp.float32))
    r, i = g[0], g[1]
    log_a = -LRU_C * r * jax.nn.softplus(-lam.astype(jnp.float32))
    a = jnp.exp(log_a)
    u = jnp.sqrt(-jnp.expm1(2.0 * log_a)) * (i * x.astype(jnp.float32))
    u = u.at[:, 0].add(a[:, 0] * h0)

    def comb(e1, e2):
        return (e1[0] * e2[0], e2[0] * e1[1] + e2[1])

    _, h = lax.associative_scan(comb, (a, u), axis=1)
    return h


def bi_rglru(x, w, b, lam, h0_f, h0_b):
    hf = rglru(x, w[0], b[0], lam[0], h0_f)
    hb = jnp.flip(rglru(jnp.flip(x, axis=1), w[1], b[1], lam[1], h0_b), axis=1)
    return hf, hb


def pool_mix(px, w, b, scale):
    B_, L, C = px.shape
    t = jnp.arange(L)
    cs = jnp.pad(jnp.cumsum(px.astype(jnp.float32), axis=1), ((0, 0), (1, 0), (0, 0)))
    outs = []
    for gi, win in enumerate(POOL_WINDOWS):
        lo = jnp.clip(t - win // 2, 0, L)
        hi = jnp.clip(t + win // 2, 0, L)
        csg = cs[..., gi * POOL_GW:(gi + 1) * POOL_GW]
        mean = (csg[:, hi] - csg[:, lo]) / (hi - lo).astype(jnp.float32)[:, None]
        outs.append(mean.astype(px.dtype) - px[..., gi * POOL_GW:(gi + 1) * POOL_GW])
    d = jnp.stack(outs, axis=2)
    y = jnp.einsum('blgc,gcd->blgd', d, w).reshape(B_, L, C) + b
    return y * scale


def swiglu(h, w_in, w_out):
    g, u = jnp.split(h @ w_in, 2, axis=-1)
    return (jax.nn.silu(g) * u) @ w_out


def setup_inputs(seed: int = 0) -> dict:
    key = jax.random.key(seed)
    ks = jax.random.split(key, 22)
    nrm = jax.random.normal
    D = D_MODEL
    a0 = jax.random.uniform(ks[14], (DEPTH, 2, LRU_W), minval=0.9, maxval=0.999)
    s0 = a0 ** (1.0 / LRU_C)
    return {
        "x": nrm(ks[0], (BATCH, SEQ, D)),
        "c": nrm(ks[1], (BATCH, D)),
        "ctx": nrm(ks[2], (BATCH, CTX_LEN, D)),
        "c_ctx": nrm(ks[3], (D,)),
        "w_mod": nrm(ks[4], (DEPTH, D, 6 * D)) * (0.5 * D ** -0.5),
        "b_mod": 0.01 * nrm(ks[5], (DEPTH, 6 * D)),
        "norm_g": 1.0 + 0.1 * nrm(ks[6], (DEPTH, 4, D)),
        "w_in": nrm(ks[7], (DEPTH, D, IN_W)) * D ** -0.5,
        "q_norm_g": 1.0 + 0.1 * nrm(ks[8], (DEPTH, HEAD_DIM)),
        "k_norm_g": 1.0 + 0.1 * nrm(ks[9], (DEPTH, HEAD_DIM)),
        "lru_conv_w": nrm(ks[10], (DEPTH, CONV_W, LRU_W)) * CONV_W ** -0.5,
        "lru_conv_b": 0.01 * nrm(ks[11], (DEPTH, LRU_W)),
        "lru_gate_w": nrm(ks[12], (DEPTH, 2, 2, LRU_BLOCKS, LRU_BW, LRU_BW)) * LRU_BW ** -0.5,
        "lru_gate_b": 0.01 * nrm(ks[13], (DEPTH, 2, 2, LRU_W)),
        "lru_lambda": jnp.log(s0) - jnp.log1p(-s0),
        "pool_w": nrm(ks[15], (DEPTH, POOL_GROUPS, POOL_GW, POOL_GW)) * POOL_GW ** -0.5,
        "pool_b": 0.01 * nrm(ks[16], (DEPTH, POOL_W)),
        "pool_scale": 1.0 + 0.1 * nrm(ks[17], (DEPTH, POOL_W)),
        "w_out": nrm(ks[18], (DEPTH, MIX_W, D)) * MIX_W ** -0.5,
        "w_ffn_in": nrm(ks[19], (DEPTH, D, 2 * FFN_HIDDEN)) * D ** -0.5,
        "w_ffn_out": nrm(ks[20], (DEPTH, FFN_HIDDEN, D)) * FFN_HIDDEN ** -0.5,
    }


def reference(x, c, ctx, c_ctx, w_mod, b_mod, norm_g, w_in, q_norm_g, k_norm_g,
              lru_conv_w, lru_conv_b, lru_gate_w, lru_gate_b, lru_lambda,
              pool_w, pool_b, pool_scale, w_out, w_ffn_in, w_ffn_out):
    B, L, _ = x.shape
    Lc = ctx.shape[1]
    nb = L // Q_BLOCK
    for l in range(DEPTH):
        last = l == DEPTH - 1
        mod_x = jnp.split((jax.nn.silu(c) @ w_mod[l] + b_mod[l])[:, None, :], 6, axis=-1)
        mod_c = jnp.split(jax.nn.silu(c_ctx) @ w_mod[l] + b_mod[l], 6, axis=-1)

        hc = modulate(rmsnorm(ctx, norm_g[l, 0]), mod_c[0], mod_c[1])
        if last:
            kc, vc, lxc = jnp.split(hc @ w_in[l][:, ATTN_W:ATTN_W + 2 * KV_W + LRU_W],
                                    [KV_W, 2 * KV_W], axis=-1)
        else:
            qc, kc, vc, lxc, lgc, pxc = jnp.split(hc @ w_in[l], SPLITS, axis=-1)
        kc_h = rmsnorm(kc.reshape(B, Lc, N_KV_HEADS, HEAD_DIM), k_norm_g[l])
        vc_h = vc.reshape(B, Lc, N_KV_HEADS, HEAD_DIM)
        uc = centred_dwconv(lxc, lru_conv_w[l], lru_conv_b[l])
        zeros = jnp.zeros((B, LRU_W), jnp.float32)
        hcf, hcb = bi_rglru(uc, lru_gate_w[l], lru_gate_b[l], lru_lambda[l], zeros, zeros)
        if not last:
            qc_h = rmsnorm(qc.reshape(B, Lc, N_KV_HEADS, Q_GROUP, HEAD_DIM), q_norm_g[l])
            att_c = attend(qc_h, kc_h, vc_h).reshape(B, Lc, ATTN_W)
            rec_c = jax.nn.gelu(lgc) * (hcf + hcb).astype(ctx.dtype)
            pl_c = pool_mix(pxc, pool_w[l], pool_b[l], pool_scale[l])
            yc = jnp.concatenate([att_c, rec_c, pl_c], axis=-1) @ w_out[l]
            ctx_mid = ctx + mod_c[2] * rmsnorm(yc, norm_g[l, 1])
            fc = swiglu(modulate(rmsnorm(ctx_mid, norm_g[l, 2]), mod_c[3], mod_c[4]),
                        w_ffn_in[l], w_ffn_out[l])
            ctx_next = ctx_mid + mod_c[5] * rmsnorm(fc, norm_g[l, 3])

        hx = modulate(rmsnorm(x, norm_g[l, 0]), mod_x[0], mod_x[1])
        qx, kx, vx, lxx, lgx, pxx = jnp.split(hx @ w_in[l], SPLITS, axis=-1)
        qx_h = axial_rope(rmsnorm(qx.reshape(B, L, N_Q_HEADS, HEAD_DIM), q_norm_g[l]))
        kx_h = axial_rope(rmsnorm(kx.reshape(B, L, N_KV_HEADS, HEAD_DIM), k_norm_g[l]))
        k_all = jnp.concatenate([kc_h, kx_h], axis=1)
        v_all = jnp.concatenate([vc_h, vx.reshape(B, L, N_KV_HEADS, HEAD_DIM)], axis=1)
        qb = qx_h.reshape(B, nb, Q_BLOCK, N_KV_HEADS, Q_GROUP, HEAD_DIM).transpose(1, 0, 2, 3, 4, 5)
        ob = lax.map(lambda blk: attend(blk, k_all, v_all), qb)
        att_x = ob.transpose(1, 0, 2, 3, 4, 5).reshape(B, L, ATTN_W)
        ux = centred_dwconv(lxx, lru_conv_w[l], lru_conv_b[l])
        hxf, hxb = bi_rglru(ux, lru_gate_w[l], lru_gate_b[l], lru_lambda[l], hcf[:, -1], hcb[:, 0])
        rec_x = jax.nn.gelu(lgx) * (hxf + hxb).astype(x.dtype)
        pl_x = pool_mix(pxx, pool_w[l], pool_b[l], pool_scale[l])
        yx = jnp.concatenate([att_x, rec_x, pl_x], axis=-1) @ w_out[l]
        x = x + mod_x[2] * rmsnorm(yx, norm_g[l, 1])
        fx = swiglu(modulate(rmsnorm(x, norm_g[l, 2]), mod_x[3], mod_x[4]), w_ffn_in[l], w_ffn_out[l])
        x = x + mod_x[5] * rmsnorm(fx, norm_g[l, 3])

        if not last:
            ctx = ctx_next
    return x
```

```python
import functools

import numpy as np
import jax
import jax.numpy as jnp
from jax import lax
from jax.experimental import pallas as pl
from jax.experimental.pallas import tpu as pltpu

D_MODEL = 1024
BATCH = 4
SEQ = 4096
DEPTH = 2
GRID_W = 64
CTX_LEN = 256
LALL = CTX_LEN + SEQ

N_Q_HEADS = 8
N_KV_HEADS = 2
HEAD_DIM = 64
Q_GROUP = N_Q_HEADS // N_KV_HEADS
ATTN_W = N_Q_HEADS * HEAD_DIM
KV_W = N_KV_HEADS * HEAD_DIM
ROPE_THETA = 10000.0
LRU_W = D_MODEL // 4
LRU_BLOCKS = 4
LRU_BW = LRU_W // LRU_BLOCKS
CONV_W = 4
LRU_C = 8.0
POOL_W = D_MODEL // 4
POOL_GROUPS = 4
POOL_GW = POOL_W // POOL_GROUPS
POOL_WINDOWS = (2, 4, 8, 16)
MIX_W = ATTN_W + LRU_W + POOL_W
IN_W = ATTN_W + 2 * KV_W + 2 * LRU_W + POOL_W
REST_W = 2 * LRU_W + POOL_W
FFN_HIDDEN = -(-8 * D_MODEL // 768) * 256
RMS_EPS = 1e-6

V7X_LANES = 128
V7X_SUBLANES = 8
V7X_VMEM_BYTES = 64 * 1024 * 1024

TOKEN_TILE = CTX_LEN
N_TILES = LALL // TOKEN_TILE
MOD_ROWS = 8
MOD_TILE_N = 1536
ROPE_HALF = HEAD_DIM // 4
GROUPS_PER_TILE = TOKEN_TILE // V7X_SUBLANES
ONES_ROWS = 16

F32 = jnp.float32
BF16 = jnp.bfloat16


def _vmem_limit(nbytes):
    return int(min(max(nbytes, 16 * 1024 * 1024), V7X_VMEM_BYTES - 6 * 1024 * 1024))


def _rms(x):
    return x * lax.rsqrt(jnp.mean(x * x, axis=-1, keepdims=True) + RMS_EPS)


def _dot(a, b):
    return jnp.dot(a, b, preferred_element_type=F32)


def _mod_kernel(c_ref, w_ref, b_ref, o_ref):
    c = c_ref[...]
    h = (c * jax.nn.sigmoid(c)).astype(BF16)
    o_ref[0] = _dot(h, w_ref[0].astype(BF16)) + b_ref[0]


def _modulation(c_all, w_mod, b_mod):
    n = 6 * D_MODEL
    return pl.pallas_call(
        _mod_kernel,
        grid=(DEPTH, n // MOD_TILE_N),
        in_specs=[
            pl.BlockSpec((MOD_ROWS, D_MODEL), lambda l, j: (0, 0)),
            pl.BlockSpec((1, D_MODEL, MOD_TILE_N), lambda l, j: (l, 0, j)),
            pl.BlockSpec((1, 1, MOD_TILE_N), lambda l, j: (l, 0, j)),
        ],
        out_specs=pl.BlockSpec((1, MOD_ROWS, MOD_TILE_N), lambda l, j: (l, 0, j)),
        out_shape=jax.ShapeDtypeStruct((DEPTH, MOD_ROWS, n), F32),
        compiler_params=pltpu.CompilerParams(
            dimension_semantics=("arbitrary", "arbitrary"),
            vmem_limit_bytes=_vmem_limit(4 * D_MODEL * MOD_TILE_N * 4)),
        name="modulation",
    )(c_all, w_mod, b_mod.reshape(DEPTH, 1, n))


def _mod_row(b, t):
    return jnp.where(t == 0, BATCH, b)


def _head_norm_rope(z, s_ref, g, cos, sin_a, sin_b):
    z2 = z * z
    hi = z2.astype(BF16)
    lo = (z2 - hi.astype(F32)).astype(BF16)
    ms = _dot(hi, s_ref[...]) + _dot(lo, s_ref[...])
    zn = (z * lax.rsqrt(ms + RMS_EPS)) * g
    outs = []
    for c in range(z.shape[1] // V7X_LANES):
        zc = zn[:, c * V7X_LANES:(c + 1) * V7X_LANES]
        up = pltpu.roll(zc, V7X_LANES - ROPE_HALF, 1)
        dn = pltpu.roll(zc, ROPE_HALF, 1)
        outs.append(zc * cos + up * sin_a + dn * sin_b)
    return outs


def _inproj_kernel(x_ref, mod_ref, g_ref, w_ref, sq_ref, sk_ref, gq_ref, gk_ref,
                   cos_ref, sa_ref, sb_ref, q_ref, k_ref, vt_ref, rest_ref):
    x = x_ref[0]
    mod = mod_ref[0]
    h = (_rms(x) * g_ref[...]) * (1.0 + mod[1:2]) + mod[0:1]
    y = _dot(h.astype(BF16), w_ref[...])
    cos, sin_a, sin_b = cos_ref[...], sa_ref[...], sb_ref[...]

    q_cols = _head_norm_rope(y[:, 0:ATTN_W], sq_ref, gq_ref[...], cos, sin_a, sin_b)
    for c, qc in enumerate(q_cols):
        qc = (qc * (HEAD_DIM ** -0.5)).astype(BF16)
        q_ref[0, 2 * c] = qc[:, 0:HEAD_DIM]
        q_ref[0, 2 * c + 1] = qc[:, HEAD_DIM:2 * HEAD_DIM]

    (kc,) = _head_norm_rope(y[:, ATTN_W:ATTN_W + KV_W], sk_ref, gk_ref[...], cos, sin_a, sin_b)
    kc = kc.astype(BF16)
    k_ref[0, 0] = kc[:, 0:HEAD_DIM]
    k_ref[0, 1] = kc[:, HEAD_DIM:2 * HEAD_DIM]

    vt = y[:, ATTN_W + KV_W:ATTN_W + 2 * KV_W].T.astype(BF16)
    vt_ref[0, 0, 0] = vt[0:HEAD_DIM]
    vt_ref[0, 1, 0] = vt[HEAD_DIM:2 * HEAD_DIM]

    rest_ref[0] = y[:, ATTN_W + 2 * KV_W:IN_W]


def _in_projection(xa, mod, g0, w_in, consts, gq, gk):
    sq, sk, cos, sin_a, sin_b = consts
    T = TOKEN_TILE
    full = lambda shape: pl.BlockSpec(shape, lambda b, t: (0,) * len(shape))
    tab = pl.BlockSpec((T, V7X_LANES), lambda b, t: (t, 0))
    return pl.pallas_call(
        _inproj_kernel,
        grid=(BATCH, N_TILES),
        in_specs=[
            pl.BlockSpec((1, T, D_MODEL), lambda b, t: (b, t, 0)),
            pl.BlockSpec((1, 6, D_MODEL), lambda b, t: (_mod_row(b, t), 0, 0)),
            full((1, D_MODEL)),
            full((D_MODEL, IN_W)),
            full((ATTN_W, ATTN_W)),
            full((KV_W, KV_W)),
            full((1, ATTN_W)),
            full((1, KV_W)),
            tab, tab, tab,
        ],
        out_specs=[
            pl.BlockSpec((1, N_Q_HEADS, T, HEAD_DIM), lambda b, t: (b, 0, t, 0)),
            pl.BlockSpec((1, N_KV_HEADS, T, HEAD_DIM), lambda b, t: (b, 0, t, 0)),
            pl.BlockSpec((1, N_KV_HEADS, 1, HEAD_DIM, T), lambda b, t: (b, 0, t, 0, 0)),
            pl.BlockSpec((1, T, REST_W), lambda b, t: (b, t, 0)),
        ],
        out_shape=[
            jax.ShapeDtypeStruct((BATCH, N_Q_HEADS, LALL, HEAD_DIM), BF16),
            jax.ShapeDtypeStruct((BATCH, N_KV_HEADS, LALL, HEAD_DIM), BF16),
            jax.ShapeDtypeStruct((BATCH, N_KV_HEADS, N_TILES, HEAD_DIM, T), BF16),
            jax.ShapeDtypeStruct((BATCH, LALL, REST_W), F32),
        ],
        compiler_params=pltpu.CompilerParams(
            dimension_semantics=("arbitrary", "arbitrary"),
            vmem_limit_bytes=_vmem_limit(40 * 1024 * 1024)),
        name="in_projection",
    )(xa, mod, g0, w_in, sq, sk, gq, gk, cos, sin_a, sin_b)


def _attn_kernel(q_ref, k_ref, vt_ref, o_ref, *, first_tile):
    T = TOKEN_TILE
    n_q = Q_GROUP * T
    t = pl.program_id(2) + first_tile
    n_chunks = jnp.where(t == 0, 1, N_TILES)
    q = q_ref[0].reshape(n_q, HEAD_DIM)
    ones = jnp.ones((ONES_ROWS, T), BF16)

    def body(j, carry):
        m, acc = carry
        s = lax.dot_general(k_ref[0, 0, pl.ds(pl.multiple_of(j * T, T), T), :], q,
                            (((1,), (1,)), ((), ())), preferred_element_type=F32)
        m_new = jnp.maximum(m, jnp.max(s, axis=0, keepdims=True))
        alpha = jnp.exp(m - m_new)
        p = jnp.exp(s - m_new).astype(BF16)
        vt1 = jnp.concatenate([vt_ref[0, 0, j], ones], axis=0)
        return m_new, alpha * acc + _dot(vt1, p)

    m0 = jnp.full((1, n_q), -1e30, F32)
    acc0 = jnp.zeros((HEAD_DIM + ONES_ROWS, n_q), F32)
    _, acc = lax.fori_loop(0, n_chunks, body, (m0, acc0))
    o = acc[0:HEAD_DIM] / acc[HEAD_DIM:HEAD_DIM + 1]
    o = jnp.concatenate([o[:, g * T:(g + 1) * T] for g in range(Q_GROUP)], axis=0)
    o_ref[0] = o.T.astype(BF16)


def _attention(q, k, vt, first_tile):
    T = TOKEN_TILE
    return pl.pallas_call(
        functools.partial(_attn_kernel, first_tile=first_tile),
        grid=(BATCH, N_KV_HEADS, N_TILES - first_tile),
        in_specs=[
            pl.BlockSpec((1, Q_GROUP, T, HEAD_DIM), lambda b, h, t: (b, h, t + first_tile, 0)),
            pl.BlockSpec((1, 1, LALL, HEAD_DIM), lambda b, h, t: (b, h, 0, 0)),
            pl.BlockSpec((1, 1, N_TILES, HEAD_DIM, T), lambda b, h, t: (b, h, 0, 0, 0)),
        ],
        out_specs=pl.BlockSpec((1, T, Q_GROUP * HEAD_DIM), lambda b, h, t: (b, t, h)),
        out_shape=jax.ShapeDtypeStruct((BATCH, (N_TILES - first_tile) * T, ATTN_W), BF16),
        compiler_params=pltpu.CompilerParams(
            dimension_semantics=("arbitrary", "arbitrary", "arbitrary"),
            vmem_limit_bytes=_vmem_limit(32 * 1024 * 1024)),
        name="attention",
    )(q, k, vt)


def _softplus(z):
    return jnp.maximum(z, 0.0) + jnp.log1p(jnp.exp(-jnp.abs(z)))


def _with_halo(ref, col, c):
    T = TOKEN_TILE
    r0 = pl.multiple_of(c * T, T)
    main = ref[0, pl.ds(r0, T), col:col + LRU_W]
    p0 = pl.multiple_of(jnp.maximum(r0 - V7X_SUBLANES, 0), V7X_SUBLANES)
    n0 = pl.multiple_of(jnp.minimum(r0 + T, LALL - V7X_SUBLANES), V7X_SUBLANES)
    prev = ref[0, pl.ds(p0, V7X_SUBLANES), col:col + LRU_W]
    nxt = ref[0, pl.ds(n0, V7X_SUBLANES), col:col + LRU_W]
    prev = jnp.where(c >= 2, prev, 0.0)
    nxt = jnp.where((c >= 1) & (c <= N_TILES - 2), nxt, 0.0)
    return main, jnp.concatenate([prev, main, nxt], axis=0)


def _shift_rows(ext, k):
    n = ext.shape[0]
    if k == 0:
        return ext[V7X_SUBLANES:V7X_SUBLANES + TOKEN_TILE]
    return pltpu.roll(ext, (-k) % n, 0)[V7X_SUBLANES:V7X_SUBLANES + TOKEN_TILE]


def _lru_pool_kernel(rest_ref, cw_ref, cb_ref, gw_ref, gb_ref, lam_ref, pw_ref, pb_ref, ps_ref,
                     rec_ref, pool_ref, hf_ref, hb_ref, a_ref, u_ref):
    T = TOKEN_TILE
    S = V7X_SUBLANES
    row_in_group = lax.broadcasted_iota(jnp.int32, (T, LRU_W), 0) % S

    def local_scan(c, d):
        main, ext = _with_halo(rest_ref, 0, c)
        u = cb_ref[...] + (cw_ref[0:1] * _shift_rows(ext, -2) + cw_ref[1:2] * _shift_rows(ext, -1)
                           + cw_ref[2:3] * main + cw_ref[3:4] * _shift_rows(ext, 1))
        g = _dot(u.astype(BF16), gw_ref[:, d * 2 * LRU_W:(d + 1) * 2 * LRU_W])
        g = jax.nn.sigmoid(g + gb_ref[:, d * 2 * LRU_W:(d + 1) * 2 * LRU_W])
        log_a = (-LRU_C * g[:, 0:LRU_W]) * _softplus(-lam_ref[d:d + 1])
        a = jnp.exp(log_a)
        th = jnp.tanh(log_a)
        neg_expm1 = (-2.0 * th) / (1.0 - th)
        v = jnp.sqrt(neg_expm1) * (g[:, LRU_W:2 * LRU_W] * u)
        for step in (1, 2, 4):
            if d == 0:
                a_s, v_s = pltpu.roll(a, step, 0), pltpu.roll(v, step, 0)
                keep = row_in_group >= step
            else:
                a_s, v_s = pltpu.roll(a, T - step, 0), pltpu.roll(v, T - step, 0)
                keep = row_in_group < S - step
            v = jnp.where(keep, a * v_s + v, v)
            a = jnp.where(keep, a * a_s, a)
        a_ref[d] = a
        u_ref[d] = v

    def tile_step(i, carry):
        h_f, h_b = carry
        c_f = i
        c_b = jnp.where(i == 0, 0, N_TILES - i)
        local_scan(c_f, 0)
        local_scan(c_b, 1)
        r_f = pl.multiple_of(c_f * T, T)
        r_b = pl.multiple_of(c_b * T, T)
        for gi in range(GROUPS_PER_TILE):
            lo = gi * S
            a, v = a_ref[0, lo:lo + S], u_ref[0, lo:lo + S]
            hf_ref[pl.ds(r_f + lo, S), :] = v + a * h_f
            h_f = (jnp.broadcast_to(v[S - 1:S], (S, LRU_W))
                   + jnp.broadcast_to(a[S - 1:S], (S, LRU_W)) * h_f)
            lo = (GROUPS_PER_TILE - 1 - gi) * S
            a, v = a_ref[1, lo:lo + S], u_ref[1, lo:lo + S]
            hb_ref[pl.ds(r_b + lo, S), :] = v + a * h_b
            h_b = (jnp.broadcast_to(v[0:1], (S, LRU_W))
                   + jnp.broadcast_to(a[0:1], (S, LRU_W)) * h_b)
        return h_f, h_b

    zeros = jnp.zeros((S, LRU_W), F32)
    lax.fori_loop(0, N_TILES, tile_step, (zeros, zeros))

    lane = lax.broadcasted_iota(jnp.int32, (T, POOL_W), 1)
    half_win = jnp.where(lane < POOL_GW, POOL_WINDOWS[0] // 2,
                         jnp.where(lane < 2 * POOL_GW, POOL_WINDOWS[1] // 2,
                                   jnp.where(lane < 3 * POOL_GW, POOL_WINDOWS[2] // 2,
                                             POOL_WINDOWS[3] // 2)))
    row = lax.broadcasted_iota(jnp.int32, (T, POOL_W), 0)

    def out_step(c, _):
        r0 = pl.multiple_of(c * T, T)
        gate = rest_ref[0, pl.ds(r0, T), LRU_W:2 * LRU_W]
        h = hf_ref[pl.ds(r0, T), :] + hb_ref[pl.ds(r0, T), :]
        rec_ref[0, pl.ds(r0, T), :] = (jax.nn.gelu(gate) * h).astype(BF16)

        px, ext = _with_halo(rest_ref, 2 * LRU_W, c)
        n = ext.shape[0]
        p2 = ext + pltpu.roll(ext, 1, 0)
        p4 = p2 + pltpu.roll(p2, 2, 0)
        p8 = p4 + pltpu.roll(p4, 4, 0)
        p16 = p8 + pltpu.roll(p8, 8, 0)
        win = jnp.where(lane < POOL_GW, _shift_rows(p2, 0),
                        jnp.where(lane < 2 * POOL_GW, _shift_rows(p4, 1),
                                  jnp.where(lane < 3 * POOL_GW, _shift_rows(p8, 3),
                                            _shift_rows(p16, 7))))
        seg_len = jnp.where(c == 0, CTX_LEN, SEQ)
        pos = row + jnp.where(c == 0, 0, r0 - CTX_LEN)
        cnt = jnp.minimum(pos + half_win, seg_len) - jnp.maximum(pos - half_win, 0)
        d = win / cnt.astype(F32) - px
        y = _dot(d.astype(BF16), pw_ref[...]) + pb_ref[...]
        pool_ref[0, pl.ds(r0, T), :] = (y * ps_ref[...]).astype(BF16)
        return 0

    lax.fori_loop(0, N_TILES, out_step, 0)


def _lru_pool(rest, conv_w, conv_b, gate_w, gate_b, lam, pool_w, pool_b, pool_scale):
    full = lambda shape: pl.BlockSpec(shape, lambda b: (0,) * len(shape))
    seq = lambda w: pl.BlockSpec((1, LALL, w), lambda b: (b, 0, 0))
    return pl.pallas_call(
        _lru_pool_kernel,
        grid=(BATCH,),
        in_specs=[
            seq(REST_W),
            full((CONV_W, LRU_W)), full((1, LRU_W)),
            full((LRU_W, 4 * LRU_W)), full((1, 4 * LRU_W)), full((2, LRU_W)),
            full((POOL_W, POOL_W)), full((1, POOL_W)), full((1, POOL_W)),
        ],
        out_specs=[seq(LRU_W), seq(POOL_W)],
        out_shape=[jax.ShapeDtypeStruct((BATCH, LALL, LRU_W), BF16),
                   jax.ShapeDtypeStruct((BATCH, LALL, POOL_W), BF16)],
        scratch_shapes=[
            pltpu.VMEM((LALL, LRU_W), F32), pltpu.VMEM((LALL, LRU_W), F32),
            pltpu.VMEM((2, TOKEN_TILE, LRU_W), F32), pltpu.VMEM((2, TOKEN_TILE, LRU_W), F32),
        ],
        compiler_params=pltpu.CompilerParams(
            dimension_semantics=("arbitrary",),
            vmem_limit_bytes=_vmem_limit(56 * 1024 * 1024)),
        name="lru_pool",
    )(rest, conv_w, conv_b, gate_w, gate_b, lam, pool_w, pool_b, pool_scale)


def _out_ffn_kernel(att_ref, rec_ref, pool_ref, x_ref, mod_ref, g_ref, wo_ref, wi_ref, wf_ref, o_ref):
    mod = mod_ref[0]
    y = (_dot(att_ref[0], wo_ref[0:ATTN_W])
         + _dot(rec_ref[0], wo_ref[ATTN_W:ATTN_W + LRU_W])
         + _dot(pool_ref[0], wo_ref[ATTN_W + LRU_W:MIX_W]))
    x = x_ref[0] + mod[2:3] * (_rms(y) * g_ref[1:2])
    h = (_rms(x) * g_ref[2:3]) * (1.0 + mod[4:5]) + mod[3:4]
    gu = _dot(h.astype(BF16), wi_ref[...])
    g, u = gu[:, 0:FFN_HIDDEN], gu[:, FFN_HIDDEN:2 * FFN_HIDDEN]
    f = _dot((g * jax.nn.sigmoid(g) * u).astype(BF16), wf_ref[...])
    o_ref[0] = x + mod[5:6] * (_rms(f) * g_ref[3:4])


def _out_ffn(att, rec, pool, xa, mod, norm_g, w_out, w_ffn_in, w_ffn_out, first_tile):
    T = TOKEN_TILE
    n_t = N_TILES - first_tile
    tok = lambda w: pl.BlockSpec((1, T, w), lambda b, t: (b, t + first_tile, 0))
    weight = lambda shape: pl.BlockSpec(shape, lambda b, t: (0,) * len(shape),
                                        pipeline_mode=pl.Buffered(1))
    return pl.pallas_call(
        _out_ffn_kernel,
        grid=(BATCH, n_t),
        in_specs=[
            pl.BlockSpec((1, T, ATTN_W), lambda b, t: (b, t, 0)),
            tok(LRU_W), tok(POOL_W), tok(D_MODEL),
            pl.BlockSpec((1, 6, D_MODEL), lambda b, t: (_mod_row(b, t + first_tile), 0, 0)),
            weight((4, D_MODEL)),
            weight((MIX_W, D_MODEL)),
            weight((D_MODEL, 2 * FFN_HIDDEN)),
            weight((FFN_HIDDEN, D_MODEL)),
        ],
        out_specs=pl.BlockSpec((1, T, D_MODEL), lambda b, t: (b, t, 0)),
        out_shape=jax.ShapeDtypeStruct((BATCH, n_t * T, D_MODEL), F32),
        compiler_params=pltpu.CompilerParams(
            dimension_semantics=("arbitrary", "arbitrary"),
            vmem_limit_bytes=_vmem_limit(48 * 1024 * 1024)),
        name="out_ffn",
    )(att, rec, pool, xa, mod, norm_g, w_out, w_ffn_in, w_ffn_out)


def _position_tables():
    half = ROPE_HALF
    freq = (ROPE_THETA ** (-np.arange(half, dtype=np.float32) / half)).astype(np.float32)
    p = np.arange(SEQ)
    ang_row = (p // GRID_W).astype(np.float32)[:, None] * freq
    ang_col = (p % GRID_W).astype(np.float32)[:, None] * freq
    ang = np.concatenate([ang_row, ang_row, ang_col, ang_col], axis=1)
    first = np.tile(np.arange(HEAD_DIM) % (2 * half) < half, (SEQ, 1))
    cos = np.cos(ang)
    sin_a = np.where(first, -np.sin(ang), 0.0)
    sin_b = np.where(first, 0.0, np.sin(ang))
    pad = lambda a, v: np.concatenate([np.full((CTX_LEN, HEAD_DIM), v), a], axis=0)
    two = lambda a: np.tile(a, (1, V7X_LANES // HEAD_DIM)).astype(np.float32)
    return two(pad(cos, 1.0)), two(pad(sin_a, 0.0)), two(pad(sin_b, 0.0))


def _head_mean_matrix(width):
    blk = np.kron(np.eye(width // HEAD_DIM), np.full((HEAD_DIM, HEAD_DIM), 1.0 / HEAD_DIM))
    return jnp.asarray(blk, dtype=BF16)


def _block_diag(w):
    n, c, d = w.shape[-3:]
    on_diag = jnp.eye(n, dtype=bool)[:, None, :, None]
    out = jnp.where(on_diag, w[..., :, :, None, :], 0.0)
    return out.reshape(w.shape[:-3] + (n * c, n * d))


def kernel(x, c, ctx, c_ctx, w_mod, b_mod, norm_g, w_in, q_norm_g, k_norm_g, lru_conv_w, lru_conv_b,
           lru_gate_w, lru_gate_b, lru_lambda, pool_w, pool_b, pool_scale, w_out, w_ffn_in, w_ffn_out):
    assert x.shape == (BATCH, SEQ, D_MODEL) and ctx.shape == (BATCH, CTX_LEN, D_MODEL)
    cos, sin_a, sin_b = (jnp.asarray(a) for a in _position_tables())
    consts = (_head_mean_matrix(ATTN_W), _head_mean_matrix(KV_W), cos, sin_a, sin_b)

    c_all = jnp.zeros((MOD_ROWS, D_MODEL), F32).at[0:BATCH].set(c).at[BATCH].set(c_ctx)
    mod = _modulation(c_all, w_mod, b_mod).reshape(DEPTH, MOD_ROWS, 6, D_MODEL)

    xa = jnp.concatenate([ctx, x], axis=1)
    for l in range(DEPTH):
        first_tile = 1 if l == DEPTH - 1 else 0
        gate_w = _block_diag(lru_gate_w[l])
        gate_w = gate_w.transpose(2, 0, 1, 3).reshape(LRU_W, 4 * LRU_W).astype(BF16)
        q, k, vt, rest = _in_projection(
            xa, mod[l], norm_g[l, 0:1], w_in[l].astype(BF16), consts,
            jnp.tile(q_norm_g[l], N_Q_HEADS)[None], jnp.tile(k_norm_g[l], N_KV_HEADS)[None])
        att = _attention(q, k, vt, first_tile)
        rec, pool = _lru_pool(
            rest, lru_conv_w[l], lru_conv_b[l][None], gate_w, lru_gate_b[l].reshape(1, 4 * LRU_W),
            lru_lambda[l], _block_diag(pool_w[l]).astype(BF16), pool_b[l][None], pool_scale[l][None])
        xa = _out_ffn(att, rec, pool, xa, mod[l], norm_g[l], w_out[l].astype(BF16),
                      w_ffn_in[l].astype(BF16), w_ffn_out[l].astype(BF16), first_tile)
    return xa
```

```python
import functools

import numpy as np
import jax
import jax.numpy as jnp
from jax import lax
from jax.experimental import pallas as pl
from jax.experimental.pallas import tpu as pltpu

D_MODEL = 1024
BATCH = 4
SEQ = 4096
DEPTH = 2
GRID_W = 64
CTX_LEN = 256
LALL = CTX_LEN + SEQ

N_Q_HEADS = 8
N_KV_HEADS = 2
HEAD_DIM = 64
Q_GROUP = N_Q_HEADS // N_KV_HEADS
ATTN_W = N_Q_HEADS * HEAD_DIM
KV_W = N_KV_HEADS * HEAD_DIM
ROPE_THETA = 10000.0
LRU_W = D_MODEL // 4
LRU_BLOCKS = 4
LRU_BW = LRU_W // LRU_BLOCKS
CONV_W = 4
LRU_C = 8.0
POOL_W = D_MODEL // 4
POOL_GROUPS = 4
POOL_GW = POOL_W // POOL_GROUPS
POOL_WINDOWS = (2, 4, 8, 16)
MIX_W = ATTN_W + LRU_W + POOL_W
IN_W = ATTN_W + 2 * KV_W + 2 * LRU_W + POOL_W
REST_W = 2 * LRU_W + POOL_W
FFN_HIDDEN = -(-8 * D_MODEL // 768) * 256
RMS_EPS = 1e-6

V7X_LANES = 128
V7X_SUBLANES = 8
V7X_VMEM_BYTES = 64 * 1024 * 1024

TOKEN_TILE = CTX_LEN
N_TILES = LALL // TOKEN_TILE
MOD_ROWS = 8
MOD_TILE_N = 1536
ROPE_HALF = HEAD_DIM // 4
GROUPS_PER_TILE = TOKEN_TILE // V7X_SUBLANES
ONES_ROWS = 16

F32 = jnp.float32
BF16 = jnp.bfloat16


def _vmem_limit(nbytes):
    return int(min(max(nbytes, 16 * 1024 * 1024), V7X_VMEM_BYTES - 6 * 1024 * 1024))


def _rms(x):
    return x * lax.rsqrt(jnp.mean(x * x, axis=-1, keepdims=True) + RMS_EPS)


def _dot(a, b):
    return jnp.dot(a, b, preferred_element_type=F32)


def _mod_kernel(c_ref, w_ref, b_ref, o_ref):
    c = c_ref[...]
    h = (c * jax.nn.sigmoid(c)).astype(BF16)
    o_ref[0] = _dot(h, w_ref[0].astype(BF16)) + b_ref[0]


def _modulation(c_all, w_mod, b_mod):
    n = 6 * D_MODEL
    return pl.pallas_call(
        _mod_kernel,
        grid=(DEPTH, n // MOD_TILE_N),
        in_specs=[
            pl.BlockSpec((MOD_ROWS, D_MODEL), lambda l, j: (0, 0)),
            pl.BlockSpec((1, D_MODEL, MOD_TILE_N), lambda l, j: (l, 0, j)),
            pl.BlockSpec((1, 1, MOD_TILE_N), lambda l, j: (l, 0, j)),
        ],
        out_specs=pl.BlockSpec((1, MOD_ROWS, MOD_TILE_N), lambda l, j: (l, 0, j)),
        out_shape=jax.ShapeDtypeStruct((DEPTH, MOD_ROWS, n), F32),
        compiler_params=pltpu.CompilerParams(
            dimension_semantics=("arbitrary", "arbitrary"),
            vmem_limit_bytes=_vmem_limit(4 * D_MODEL * MOD_TILE_N * 4)),
        name="modulation",
    )(c_all, w_mod, b_mod.reshape(DEPTH, 1, n))


def _mod_row(b, t):
    return jnp.where(t == 0, BATCH, b)


def _head_norm_rope(z, s_ref, g, cos, sin_a, sin_b):
    z2 = z * z
    hi = z2.astype(BF16)
    lo = (z2 - hi.astype(F32)).astype(BF16)
    ms = _dot(hi, s_ref[...]) + _dot(lo, s_ref[...])
    zn = (z * lax.rsqrt(ms + RMS_EPS)) * g
    outs = []
    for c in range(z.shape[1] // V7X_LANES):
        zc = zn[:, c * V7X_LANES:(c + 1) * V7X_LANES]
        up = pltpu.roll(zc, V7X_LANES - ROPE_HALF, 1)
        dn = pltpu.roll(zc, ROPE_HALF, 1)
        outs.append(zc * cos + up * sin_a + dn * sin_b)
    return outs


def _inproj_kernel(x_ref, mod_ref, g_ref, w_ref, sq_ref, sk_ref, gq_ref, gk_ref,
                   cos_ref, sa_ref, sb_ref, q_ref, k_ref, vt_ref, rest_ref):
    x = x_ref[0]
    mod = mod_ref[0]
    h = (_rms(x) * g_ref[...]) * (1.0 + mod[1:2]) + mod[0:1]
    y = _dot(h.astype(BF16), w_ref[...])
    cos, sin_a, sin_b = cos_ref[...], sa_ref[...], sb_ref[...]

    q_cols = _head_norm_rope(y[:, 0:ATTN_W], sq_ref, gq_ref[...], cos, sin_a, sin_b)
    for c, qc in enumerate(q_cols):
        qc = (qc * (HEAD_DIM ** -0.5)).astype(BF16)
        q_ref[0, 2 * c] = qc[:, 0:HEAD_DIM]
        q_ref[0, 2 * c + 1] = qc[:, HEAD_DIM:2 * HEAD_DIM]

    (kc,) = _head_norm_rope(y[:, ATTN_W:ATTN_W + KV_W], sk_ref, gk_ref[...], cos, sin_a, sin_b)
    kc = kc.astype(BF16)
    k_ref[0, 0] = kc[:, 0:HEAD_DIM]
    k_ref[0, 1] = kc[:, HEAD_DIM:2 * HEAD_DIM]

    vt = y[:, ATTN_W + KV_W:ATTN_W + 2 * KV_W].T.astype(BF16)
    vt_ref[0, 0, 0] = vt[0:HEAD_DIM]
    vt_ref[0, 1, 0] = vt[HEAD_DIM:2 * HEAD_DIM]

    rest_ref[0] = y[:, ATTN_W + 2 * KV_W:IN_W]


def _in_projection(xa, mod, g0, w_in, consts, gq, gk):
    sq, sk, cos, sin_a, sin_b = consts
    T = TOKEN_TILE
    full = lambda shape: pl.BlockSpec(shape, lambda b, t: (0,) * len(shape))
    tab = pl.BlockSpec((T, V7X_LANES), lambda b, t: (t, 0))
    return pl.pallas_call(
        _inproj_kernel,
        grid=(BATCH, N_TILES),
        in_specs=[
            pl.BlockSpec((1, T, D_MODEL), lambda b, t: (b, t, 0)),
            pl.BlockSpec((1, 6, D_MODEL), lambda b, t: (_mod_row(b, t), 0, 0)),
            full((1, D_MODEL)),
            full((D_MODEL, IN_W)),
            full((ATTN_W, ATTN_W)),
            full((KV_W, KV_W)),
            full((1, ATTN_W)),
            full((1, KV_W)),
            tab, tab, tab,
        ],
        out_specs=[
            pl.BlockSpec((1, N_Q_HEADS, T, HEAD_DIM), lambda b, t: (b, 0, t, 0)),
            pl.BlockSpec((1, N_KV_HEADS, T, HEAD_DIM), lambda b, t: (b, 0, t, 0)),
            pl.BlockSpec((1, N_KV_HEADS, 1, HEAD_DIM, T), lambda b, t: (b, 0, t, 0, 0)),
            pl.BlockSpec((1, T, REST_W), lambda b, t: (b, t, 0)),
        ],
        out_shape=[
            jax.ShapeDtypeStruct((BATCH, N_Q_HEADS, LALL, HEAD_DIM), BF16),
            jax.ShapeDtypeStruct((BATCH, N_KV_HEADS, LALL, HEAD_DIM), BF16),
            jax.ShapeDtypeStruct((BATCH, N_KV_HEADS, N_TILES, HEAD_DIM, T), BF16),
            jax.ShapeDtypeStruct((BATCH, LALL, REST_W), F32),
        ],
        compiler_params=pltpu.CompilerParams(
            dimension_semantics=("arbitrary", "arbitrary"),
            vmem_limit_bytes=_vmem_limit(40 * 1024 * 1024)),
        name="in_projection",
    )(xa, mod, g0, w_in, sq, sk, gq, gk, cos, sin_a, sin_b)


def _attn_kernel(q_ref, k_ref, vt_ref, o_ref, s_ref, m_ref, acc_ref, *, first_tile):
    T = TOKEN_TILE
    n_q = Q_GROUP * T
    t = pl.program_id(2) + first_tile
    q = q_ref[0].reshape(n_q, HEAD_DIM)
    ones = jnp.ones((ONES_ROWS, T), BF16)

    def scores(j, slot):
        kc = k_ref[0, 0, pl.ds(pl.multiple_of(j * T, T), T), :]
        s_ref[slot] = lax.dot_general(kc, q, (((1,), (1,)), ((), ())),
                                      preferred_element_type=F32)

    def update(j, slot):
        vt1 = jnp.concatenate([vt_ref[0, 0, j], ones], axis=0)
        for g in range(Q_GROUP):
            cols = slice(g * T, (g + 1) * T)
            s = s_ref[slot, :, cols]
            m = m_ref[:, cols]
            m_new = jnp.maximum(m, jnp.max(s, axis=0, keepdims=True))
            alpha = jnp.exp(m - m_new)
            p = jnp.exp(s - m_new).astype(BF16)
            m_ref[:, cols] = m_new
            acc_ref[:, cols] = alpha * acc_ref[:, cols] + _dot(vt1, p)

    def pair(i, _):
        j = 2 * i + 1
        scores(j + 1, 1)
        update(j, 0)
        scores(jnp.minimum(j + 2, N_TILES - 1), 0)
        update(j + 1, 1)
        return 0

    m_ref[...] = jnp.full((1, n_q), -1e30, F32)
    acc_ref[...] = jnp.zeros((HEAD_DIM + ONES_ROWS, n_q), F32)
    scores(0, 1)
    scores(1, 0)
    update(0, 1)
    n_pairs = jnp.where(t == 0, 0, (N_TILES - 1) // 2)
    lax.fori_loop(0, n_pairs, pair, 0)
    o = acc_ref[0:HEAD_DIM] / acc_ref[HEAD_DIM:HEAD_DIM + 1]
    o = jnp.concatenate([o[:, g * T:(g + 1) * T] for g in range(Q_GROUP)], axis=0)
    o_ref[0] = o.T.astype(BF16)


def _attention(q, k, vt, first_tile):
    T = TOKEN_TILE
    return pl.pallas_call(
        functools.partial(_attn_kernel, first_tile=first_tile),
        grid=(BATCH, N_KV_HEADS, N_TILES - first_tile),
        in_specs=[
            pl.BlockSpec((1, Q_GROUP, T, HEAD_DIM), lambda b, h, t: (b, h, t + first_tile, 0)),
            pl.BlockSpec((1, 1, LALL, HEAD_DIM), lambda b, h, t: (b, h, 0, 0)),
            pl.BlockSpec((1, 1, N_TILES, HEAD_DIM, T), lambda b, h, t: (b, h, 0, 0, 0)),
        ],
        out_specs=pl.BlockSpec((1, T, Q_GROUP * HEAD_DIM), lambda b, h, t: (b, t, h)),
        out_shape=jax.ShapeDtypeStruct((BATCH, (N_TILES - first_tile) * T, ATTN_W), BF16),
        scratch_shapes=[
            pltpu.VMEM((2, T, Q_GROUP * T), F32),
            pltpu.VMEM((1, Q_GROUP * T), F32),
            pltpu.VMEM((HEAD_DIM + ONES_ROWS, Q_GROUP * T), F32),
        ],
        compiler_params=pltpu.CompilerParams(
            dimension_semantics=("arbitrary", "arbitrary", "arbitrary"),
            vmem_limit_bytes=_vmem_limit(32 * 1024 * 1024)),
        name="attention",
    )(q, k, vt)


def _softplus(z):
    return jnp.maximum(z, 0.0) + jnp.log1p(jnp.exp(-jnp.abs(z)))


def _with_halo(ref, col, c):
    T = TOKEN_TILE
    r0 = pl.multiple_of(c * T, T)
    main = ref[0, pl.ds(r0, T), col:col + LRU_W]
    p0 = pl.multiple_of(jnp.maximum(r0 - V7X_SUBLANES, 0), V7X_SUBLANES)
    n0 = pl.multiple_of(jnp.minimum(r0 + T, LALL - V7X_SUBLANES), V7X_SUBLANES)
    prev = ref[0, pl.ds(p0, V7X_SUBLANES), col:col + LRU_W]
    nxt = ref[0, pl.ds(n0, V7X_SUBLANES), col:col + LRU_W]
    prev = jnp.where(c >= 2, prev, 0.0)
    nxt = jnp.where((c >= 1) & (c <= N_TILES - 2), nxt, 0.0)
    return main, jnp.concatenate([prev, main, nxt], axis=0)


def _shift_rows(ext, k):
    n = ext.shape[0]
    if k == 0:
        return ext[V7X_SUBLANES:V7X_SUBLANES + TOKEN_TILE]
    return pltpu.roll(ext, (-k) % n, 0)[V7X_SUBLANES:V7X_SUBLANES + TOKEN_TILE]


def _lru_pool_kernel(rest_ref, cw_ref, cb_ref, gw_ref, gb_ref, lam_ref, pw_ref, pb_ref, ps_ref,
                     rec_ref, pool_ref, hf_ref, hb_ref, a_ref, u_ref):
    T = TOKEN_TILE
    S = V7X_SUBLANES
    row_in_group = lax.broadcasted_iota(jnp.int32, (T, LRU_W), 0) % S

    def local_scan(c, d):
        main, ext = _with_halo(rest_ref, 0, c)
        u = cb_ref[...] + (cw_ref[0:1] * _shift_rows(ext, -2) + cw_ref[1:2] * _shift_rows(ext, -1)
                           + cw_ref[2:3] * main + cw_ref[3:4] * _shift_rows(ext, 1))
        g = _dot(u.astype(BF16), gw_ref[:, d * 2 * LRU_W:(d + 1) * 2 * LRU_W])
        g = jax.nn.sigmoid(g + gb_ref[:, d * 2 * LRU_W:(d + 1) * 2 * LRU_W])
        log_a = (-LRU_C * g[:, 0:LRU_W]) * _softplus(-lam_ref[d:d + 1])
        a = jnp.exp(log_a)
        th = jnp.tanh(log_a)
        neg_expm1 = (-2.0 * th) / (1.0 - th)
        v = jnp.sqrt(neg_expm1) * (g[:, LRU_W:2 * LRU_W] * u)
        for step in (1, 2, 4):
            if d == 0:
                a_s, v_s = pltpu.roll(a, step, 0), pltpu.roll(v, step, 0)
                keep = row_in_group >= step
            else:
                a_s, v_s = pltpu.roll(a, T - step, 0), pltpu.roll(v, T - step, 0)
                keep = row_in_group < S - step
            v = jnp.where(keep, a * v_s + v, v)
            a = jnp.where(keep, a * a_s, a)
        a_ref[d] = a
        u_ref[d] = v

    def tile_step(i, carry):
        h_f, h_b = carry
        c_f = i
        c_b = jnp.where(i == 0, 0, N_TILES - i)
        local_scan(c_f, 0)
        local_scan(c_b, 1)
        r_f = pl.multiple_of(c_f * T, T)
        r_b = pl.multiple_of(c_b * T, T)
        for gi in range(GROUPS_PER_TILE):
            lo = gi * S
            a, v = a_ref[0, lo:lo + S], u_ref[0, lo:lo + S]
            hf_ref[pl.ds(r_f + lo, S), :] = v + a * h_f
            h_f = (jnp.broadcast_to(v[S - 1:S], (S, LRU_W))
                   + jnp.broadcast_to(a[S - 1:S], (S, LRU_W)) * h_f)
            lo = (GROUPS_PER_TILE - 1 - gi) * S
            a, v = a_ref[1, lo:lo + S], u_ref[1, lo:lo + S]
            hb_ref[pl.ds(r_b + lo, S), :] = v + a * h_b
            h_b = (jnp.broadcast_to(v[0:1], (S, LRU_W))
                   + jnp.broadcast_to(a[0:1], (S, LRU_W)) * h_b)
        return h_f, h_b

    zeros = jnp.zeros((S, LRU_W), F32)
    lax.fori_loop(0, N_TILES, tile_step, (zeros, zeros))

    lane = lax.broadcasted_iota(jnp.int32, (T, POOL_W), 1)
    half_win = jnp.where(lane < POOL_GW, POOL_WINDOWS[0] // 2,
                         jnp.where(lane < 2 * POOL_GW, POOL_WINDOWS[1] // 2,
                                   jnp.where(lane < 3 * POOL_GW, POOL_WINDOWS[2] // 2,
                                             POOL_WINDOWS[3] // 2)))
    row = lax.broadcasted_iota(jnp.int32, (T, POOL_W), 0)

    def out_step(c, _):
        r0 = pl.multiple_of(c * T, T)
        gate = rest_ref[0, pl.ds(r0, T), LRU_W:2 * LRU_W]
        h = hf_ref[pl.ds(r0, T), :] + hb_ref[pl.ds(r0, T), :]
        rec_ref[0, pl.ds(r0, T), :] = (jax.nn.gelu(gate) * h).astype(BF16)

        px, ext = _with_halo(rest_ref, 2 * LRU_W, c)
        n = ext.shape[0]
        p2 = ext + pltpu.roll(ext, 1, 0)
        p4 = p2 + pltpu.roll(p2, 2, 0)
        p8 = p4 + pltpu.roll(p4, 4, 0)
        p16 = p8 + pltpu.roll(p8, 8, 0)
        win = jnp.where(lane < POOL_GW, _shift_rows(p2, 0),
                        jnp.where(lane < 2 * POOL_GW, _shift_rows(p4, 1),
                                  jnp.where(lane < 3 * POOL_GW, _shift_rows(p8, 3),
                                            _shift_rows(p16, 7))))
        seg_len = jnp.where(c == 0, CTX_LEN, SEQ)
        pos = row + jnp.where(c == 0, 0, r0 - CTX_LEN)
        cnt = jnp.minimum(pos + half_win, seg_len) - jnp.maximum(pos - half_win, 0)
        d = win / cnt.astype(F32) - px
        y = _dot(d.astype(BF16), pw_ref[...]) + pb_ref[...]
        pool_ref[0, pl.ds(r0, T), :] = (y * ps_ref[...]).astype(BF16)
        return 0

    lax.fori_loop(0, N_TILES, out_step, 0)


def _lru_pool(rest, conv_w, conv_b, gate_w, gate_b, lam, pool_w, pool_b, pool_scale):
    full = lambda shape: pl.BlockSpec(shape, lambda b: (0,) * len(shape))
    seq = lambda w: pl.BlockSpec((1, LALL, w), lambda b: (b, 0, 0))
    return pl.pallas_call(
        _lru_pool_kernel,
        grid=(BATCH,),
        in_specs=[
            seq(REST_W),
            full((CONV_W, LRU_W)), full((1, LRU_W)),
            full((LRU_W, 4 * LRU_W)), full((1, 4 * LRU_W)), full((2, LRU_W)),
            full((POOL_W, POOL_W)), full((1, POOL_W)), full((1, POOL_W)),
        ],
        out_specs=[seq(LRU_W), seq(POOL_W)],
        out_shape=[jax.ShapeDtypeStruct((BATCH, LALL, LRU_W), BF16),
                   jax.ShapeDtypeStruct((BATCH, LALL, POOL_W), BF16)],
        scratch_shapes=[
            pltpu.VMEM((LALL, LRU_W), F32), pltpu.VMEM((LALL, LRU_W), F32),
            pltpu.VMEM((2, TOKEN_TILE, LRU_W), F32), pltpu.VMEM((2, TOKEN_TILE, LRU_W), F32),
        ],
        compiler_params=pltpu.CompilerParams(
            dimension_semantics=("arbitrary",),
            vmem_limit_bytes=_vmem_limit(56 * 1024 * 1024)),
        name="lru_pool",
    )(rest, conv_w, conv_b, gate_w, gate_b, lam, pool_w, pool_b, pool_scale)


def _out_ffn_kernel(att_ref, rec_ref, pool_ref, x_ref, mod_ref, g_ref, wo_ref, wi_ref, wf_ref, o_ref):
    mod = mod_ref[0]
    y = (_dot(att_ref[0], wo_ref[0:ATTN_W])
         + _dot(rec_ref[0], wo_ref[ATTN_W:ATTN_W + LRU_W])
         + _dot(pool_ref[0], wo_ref[ATTN_W + LRU_W:MIX_W]))
    x = x_ref[0] + mod[2:3] * (_rms(y) * g_ref[1:2])
    h = (_rms(x) * g_ref[2:3]) * (1.0 + mod[4:5]) + mod[3:4]
    gu = _dot(h.astype(BF16), wi_ref[...])
    g, u = gu[:, 0:FFN_HIDDEN], gu[:, FFN_HIDDEN:2 * FFN_HIDDEN]
    f = _dot((g * jax.nn.sigmoid(g) * u).astype(BF16), wf_ref[...])
    o_ref[0] = x + mod[5:6] * (_rms(f) * g_ref[3:4])


def _out_ffn(att, rec, pool, xa, mod, norm_g, w_out, w_ffn_in, w_ffn_out, first_tile):
    T = TOKEN_TILE
    n_t = N_TILES - first_tile
    tok = lambda w: pl.BlockSpec((1, T, w), lambda b, t: (b, t + first_tile, 0))
    weight = lambda shape: pl.BlockSpec(shape, lambda b, t: (0,) * len(shape),
                                        pipeline_mode=pl.Buffered(1))
    return pl.pallas_call(
        _out_ffn_kernel,
        grid=(BATCH, n_t),
        in_specs=[
            pl.BlockSpec((1, T, ATTN_W), lambda b, t: (b, t, 0)),
            tok(LRU_W), tok(POOL_W), tok(D_MODEL),
            pl.BlockSpec((1, 6, D_MODEL), lambda b, t: (_mod_row(b, t + first_tile), 0, 0)),
            weight((4, D_MODEL)),
            weight((MIX_W, D_MODEL)),
            weight((D_MODEL, 2 * FFN_HIDDEN)),
            weight((FFN_HIDDEN, D_MODEL)),
        ],
        out_specs=pl.BlockSpec((1, T, D_MODEL), lambda b, t: (b, t, 0)),
        out_shape=jax.ShapeDtypeStruct((BATCH, n_t * T, D_MODEL), F32),
        compiler_params=pltpu.CompilerParams(
            dimension_semantics=("arbitrary", "arbitrary"),
            vmem_limit_bytes=_vmem_limit(48 * 1024 * 1024)),
        name="out_ffn",
    )(att, rec, pool, xa, mod, norm_g, w_out, w_ffn_in, w_ffn_out)


def _position_tables():
    half = ROPE_HALF
    freq = (ROPE_THETA ** (-np.arange(half, dtype=np.float32) / half)).astype(np.float32)
    p = np.arange(SEQ)
    ang_row = (p // GRID_W).astype(np.float32)[:, None] * freq
    ang_col = (p % GRID_W).astype(np.float32)[:, None] * freq
    ang = np.concatenate([ang_row, ang_row, ang_col, ang_col], axis=1)
    first = np.tile(np.arange(HEAD_DIM) % (2 * half) < half, (SEQ, 1))
    cos = np.cos(ang)
    sin_a = np.where(first, -np.sin(ang), 0.0)
    sin_b = np.where(first, 0.0, np.sin(ang))
    pad = lambda a, v: np.concatenate([np.full((CTX_LEN, HEAD_DIM), v), a], axis=0)
    two = lambda a: np.tile(a, (1, V7X_LANES // HEAD_DIM)).astype(np.float32)
    return two(pad(cos, 1.0)), two(pad(sin_a, 0.0)), two(pad(sin_b, 0.0))


def _head_mean_matrix(width):
    blk = np.kron(np.eye(width // HEAD_DIM), np.full((HEAD_DIM, HEAD_DIM), 1.0 / HEAD_DIM))
    return jnp.asarray(blk, dtype=BF16)


def _block_diag(w):
    n, c, d = w.shape[-3:]
    on_diag = jnp.eye(n, dtype=bool)[:, None, :, None]
    out = jnp.where(on_diag, w[..., :, :, None, :], 0.0)
    return out.reshape(w.shape[:-3] + (n * c, n * d))


def kernel(x, c, ctx, c_ctx, w_mod, b_mod, norm_g, w_in, q_norm_g, k_norm_g, lru_conv_w, lru_conv_b,
           lru_gate_w, lru_gate_b, lru_lambda, pool_w, pool_b, pool_scale, w_out, w_ffn_in, w_ffn_out):
    assert x.shape == (BATCH, SEQ, D_MODEL) and ctx.shape == (BATCH, CTX_LEN, D_MODEL)
    cos, sin_a, sin_b = (jnp.asarray(a) for a in _position_tables())
    consts = (_head_mean_matrix(ATTN_W), _head_mean_matrix(KV_W), cos, sin_a, sin_b)

    c_all = jnp.zeros((MOD_ROWS, D_MODEL), F32).at[0:BATCH].set(c).at[BATCH].set(c_ctx)
    mod = _modulation(c_all, w_mod, b_mod).reshape(DEPTH, MOD_ROWS, 6, D_MODEL)

    xa = jnp.concatenate([ctx, x], axis=1)
    for l in range(DEPTH):
        first_tile = 1 if l == DEPTH - 1 else 0
        gate_w = _block_diag(lru_gate_w[l])
        gate_w = gate_w.transpose(2, 0, 1, 3).reshape(LRU_W, 4 * LRU_W).astype(BF16)
        q, k, vt, rest = _in_projection(
            xa, mod[l], norm_g[l, 0:1], w_in[l].astype(BF16), consts,
            jnp.tile(q_norm_g[l], N_Q_HEADS)[None], jnp.tile(k_norm_g[l], N_KV_HEADS)[None])
        att = _attention(q, k, vt, first_tile)
        rec, pool = _lru_pool(
            rest, lru_conv_w[l], lru_conv_b[l][None], gate_w, lru_gate_b[l].reshape(1, 4 * LRU_W),
            lru_lambda[l], _block_diag(pool_w[l]).astype(BF16), pool_b[l][None], pool_scale[l][None])
        xa = _out_ffn(att, rec, pool, xa, mod[l], norm_g[l], w_out[l].astype(BF16),
                      w_ffn_in[l].astype(BF16), w_ffn_out[l].astype(BF16), first_tile)
    return xa
```

```python
import functools

import numpy as np
import jax
import jax.numpy as jnp
from jax import lax
from jax.experimental import pallas as pl
from jax.experimental.pallas import tpu as pltpu

D_MODEL = 1024
BATCH = 4
SEQ = 4096
DEPTH = 2
GRID_W = 64
CTX_LEN = 256
LALL = CTX_LEN + SEQ

N_Q_HEADS = 8
N_KV_HEADS = 2
HEAD_DIM = 64
Q_GROUP = N_Q_HEADS // N_KV_HEADS
ATTN_W = N_Q_HEADS * HEAD_DIM
KV_W = N_KV_HEADS * HEAD_DIM
ROPE_THETA = 10000.0
LRU_W = D_MODEL // 4
LRU_BLOCKS = 4
LRU_BW = LRU_W // LRU_BLOCKS
CONV_W = 4
LRU_C = 8.0
POOL_W = D_MODEL // 4
POOL_GROUPS = 4
POOL_GW = POOL_W // POOL_GROUPS
POOL_WINDOWS = (2, 4, 8, 16)
MIX_W = ATTN_W + LRU_W + POOL_W
IN_W = ATTN_W + 2 * KV_W + 2 * LRU_W + POOL_W
REST_W = 2 * LRU_W + POOL_W
FFN_HIDDEN = -(-8 * D_MODEL // 768) * 256
RMS_EPS = 1e-6

V7X_LANES = 128
V7X_SUBLANES = 8
V7X_VMEM_BYTES = 64 * 1024 * 1024

TOKEN_TILE = CTX_LEN
N_TILES = LALL // TOKEN_TILE
MOD_ROWS = 8
MOD_TILE_N = 1536
ROPE_HALF = HEAD_DIM // 4
GROUPS_PER_TILE = TOKEN_TILE // V7X_SUBLANES
ONES_ROWS = 16
KEY_CHUNK_TILES = 1
Q_SCALE = HEAD_DIM ** -0.5 * float(np.log2(np.e))

F32 = jnp.float32
BF16 = jnp.bfloat16


def _vmem_limit(nbytes):
    return int(min(max(nbytes, 16 * 1024 * 1024), V7X_VMEM_BYTES - 6 * 1024 * 1024))


def _rms(x):
    return x * lax.rsqrt(jnp.mean(x * x, axis=-1, keepdims=True) + RMS_EPS)


def _dot(a, b):
    return jnp.dot(a, b, preferred_element_type=F32)


def _layer_spec(l, shape, **kw):
    return pl.BlockSpec((None,) + shape, lambda *_: (l,) + (0,) * len(shape), **kw)


def _mod_row(b, t):
    return jnp.where(t == 0, BATCH, b)


def _token_specs(width, first_tile, separate):
    T = TOKEN_TILE
    ctx_spec = pl.BlockSpec((1, T, width), lambda b, t: (b, 0, 0))
    if separate:
        x_spec = pl.BlockSpec((1, T, width), lambda b, t: (b, jnp.maximum(t + first_tile - 1, 0), 0))
    else:
        x_spec = pl.BlockSpec((1, T, width), lambda b, t: (b, t + first_tile, 0))
    return ctx_spec, x_spec


def _mod_kernel(c_ref, w_ref, b_ref, o_ref):
    c = c_ref[...]
    h = (c * jax.nn.sigmoid(c)).astype(BF16)
    o_ref[0] = _dot(h, w_ref[0].astype(BF16)) + b_ref[0]


def _modulation(c_all, w_mod, b_mod):
    n = 6 * D_MODEL
    return pl.pallas_call(
        _mod_kernel,
        grid=(DEPTH, n // MOD_TILE_N),
        in_specs=[
            pl.BlockSpec((MOD_ROWS, D_MODEL), lambda l, j: (0, 0)),
            pl.BlockSpec((1, D_MODEL, MOD_TILE_N), lambda l, j: (l, 0, j)),
            pl.BlockSpec((1, 1, MOD_TILE_N), lambda l, j: (l, 0, j)),
        ],
        out_specs=pl.BlockSpec((1, MOD_ROWS, MOD_TILE_N), lambda l, j: (l, 0, j)),
        out_shape=jax.ShapeDtypeStruct((DEPTH, MOD_ROWS, n), F32),
        compiler_params=pltpu.CompilerParams(
            dimension_semantics=("arbitrary", "arbitrary"),
            vmem_limit_bytes=_vmem_limit(4 * D_MODEL * MOD_TILE_N * 4)),
        name="modulation",
    )(c_all, w_mod, b_mod.reshape(DEPTH, 1, n))


def _head_norm_rope(z, s_ref, g, cos, sin_a, sin_b):
    z2 = z * z
    hi = z2.astype(BF16)
    lo = (z2 - hi.astype(F32)).astype(BF16)
    ms = _dot(hi, s_ref[...]) + _dot(lo, s_ref[...])
    zn = (z * lax.rsqrt(ms + RMS_EPS)) * g
    outs = []
    for c in range(z.shape[1] // V7X_LANES):
        zc = zn[:, c * V7X_LANES:(c + 1) * V7X_LANES]
        up = pltpu.roll(zc, V7X_LANES - ROPE_HALF, 1)
        dn = pltpu.roll(zc, ROPE_HALF, 1)
        outs.append(zc * cos + up * sin_a + dn * sin_b)
    return outs


def _inproj_kernel(ctx_ref, x_ref, mod_ref, g_ref, w_ref, sq_ref, sk_ref, gq_ref, gk_ref,
                   cos_ref, sa_ref, sb_ref, q_ref, k_ref, vt_ref, rest_ref):
    x = jnp.where(pl.program_id(1) == 0, ctx_ref[0], x_ref[0])
    mod = mod_ref[0]
    h = (_rms(x) * g_ref[0:1]) * (1.0 + mod[1:2]) + mod[0:1]
    y = _dot(h.astype(BF16), w_ref[...])
    cos, sin_a, sin_b = cos_ref[...], sa_ref[...], sb_ref[...]

    q_cols = _head_norm_rope(y[:, 0:ATTN_W], sq_ref, gq_ref[...], cos, sin_a, sin_b)
    for c, qc in enumerate(q_cols):
        qc = (qc * Q_SCALE).astype(BF16)
        q_ref[0, 2 * c] = qc[:, 0:HEAD_DIM]
        q_ref[0, 2 * c + 1] = qc[:, HEAD_DIM:2 * HEAD_DIM]

    (kc,) = _head_norm_rope(y[:, ATTN_W:ATTN_W + KV_W], sk_ref, gk_ref[...], cos, sin_a, sin_b)
    kc = kc.astype(BF16)
    k_ref[0, 0] = kc[:, 0:HEAD_DIM]
    k_ref[0, 1] = kc[:, HEAD_DIM:2 * HEAD_DIM]

    vt = y[:, ATTN_W + KV_W:ATTN_W + 2 * KV_W].T.astype(BF16)
    vt_ref[0, 0, 0] = vt[0:HEAD_DIM]
    vt_ref[0, 1, 0] = vt[HEAD_DIM:2 * HEAD_DIM]

    rest_ref[0] = y[:, ATTN_W + 2 * KV_W:IN_W]


def _in_projection(l, ctx_src, x_src, separate, mod, norm_g, w_in, consts, gq, gk):
    sq, sk, cos, sin_a, sin_b = consts
    T = TOKEN_TILE
    full = lambda shape: pl.BlockSpec(shape, lambda b, t: (0,) * len(shape))
    tab = pl.BlockSpec((T, V7X_LANES), lambda b, t: (t, 0))
    return pl.pallas_call(
        _inproj_kernel,
        grid=(BATCH, N_TILES),
        in_specs=[
            *_token_specs(D_MODEL, 0, separate),
            pl.BlockSpec((None, 1, 6, D_MODEL), lambda b, t: (l, _mod_row(b, t), 0, 0)),
            _layer_spec(l, (4, D_MODEL)),
            _layer_spec(l, (D_MODEL, IN_W)),
            full((ATTN_W, ATTN_W)),
            full((KV_W, KV_W)),
            _layer_spec(l, (1, ATTN_W)),
            _layer_spec(l, (1, KV_W)),
            tab, tab, tab,
        ],
        out_specs=[
            pl.BlockSpec((1, N_Q_HEADS, T, HEAD_DIM), lambda b, t: (b, 0, t, 0)),
            pl.BlockSpec((1, N_KV_HEADS, T, HEAD_DIM), lambda b, t: (b, 0, t, 0)),
            pl.BlockSpec((1, N_KV_HEADS, 1, HEAD_DIM, T), lambda b, t: (b, 0, t, 0, 0)),
            pl.BlockSpec((1, T, REST_W), lambda b, t: (b, t, 0)),
        ],
        out_shape=[
            jax.ShapeDtypeStruct((BATCH, N_Q_HEADS, LALL, HEAD_DIM), BF16),
            jax.ShapeDtypeStruct((BATCH, N_KV_HEADS, LALL, HEAD_DIM), BF16),
            jax.ShapeDtypeStruct((BATCH, N_KV_HEADS, N_TILES, HEAD_DIM, T), BF16),
            jax.ShapeDtypeStruct((BATCH, LALL, REST_W), F32),
        ],
        compiler_params=pltpu.CompilerParams(
            dimension_semantics=("arbitrary", "arbitrary"),
            vmem_limit_bytes=_vmem_limit(40 * 1024 * 1024)),
        name="in_projection",
    )(ctx_src, x_src, mod, norm_g, w_in, sq, sk, gq, gk, cos, sin_a, sin_b)


def _attn_kernel(q_ref, k_ref, vt_ref, o_ref, s_ref, m_ref, acc_ref, *, first_tile):
    T = TOKEN_TILE
    n_q = Q_GROUP * T
    t = pl.program_id(2) + first_tile
    q = q_ref[0].reshape(n_q, HEAD_DIM)
    chunks = [(0, 1)] + [(j, KEY_CHUNK_TILES) for j in range(1, N_TILES, KEY_CHUNK_TILES)]

    def scores(c):
        j, n = chunks[c]
        s_ref[c % 2, 0:n * T] = lax.dot_general(
            k_ref[0, 0, j * T:(j + n) * T, :], q, (((1,), (1,)), ((), ())),
            preferred_element_type=F32)

    def update(c):
        j, n = chunks[c]
        vt = jnp.concatenate([vt_ref[0, 0, j + i] for i in range(n)], axis=1)
        vt1 = jnp.concatenate([vt, jnp.ones((ONES_ROWS, n * T), BF16)], axis=0)
        for g in range(Q_GROUP):
            cols = slice(g * T, (g + 1) * T)
            s = s_ref[c % 2, 0:n * T, cols]
            m = m_ref[:, cols]
            m_new = jnp.maximum(m, jnp.max(s, axis=0, keepdims=True))
            alpha = jnp.exp2(m - m_new)
            p = jnp.exp2((s - m_new).astype(BF16))
            m_ref[:, cols] = m_new
            acc_ref[:, cols] = alpha * acc_ref[:, cols] + _dot(vt1, p)

    def latent_chunks():
        for c in range(1, len(chunks)):
            if c + 1 < len(chunks):
                scores(c + 1)
            update(c)

    m_ref[...] = jnp.full((1, n_q), -1e30, F32)
    acc_ref[...] = jnp.zeros((HEAD_DIM + ONES_ROWS, n_q), F32)
    scores(0)
    scores(1)
    update(0)
    if first_tile == 0:
        pl.when(t != 0)(latent_chunks)
    else:
        latent_chunks()
    o = acc_ref[0:HEAD_DIM] / acc_ref[HEAD_DIM:HEAD_DIM + 1]
    o = jnp.concatenate([o[:, g * T:(g + 1) * T] for g in range(Q_GROUP)], axis=0)
    o_ref[0] = o.T.astype(BF16)


def _attention(q, k, vt, first_tile):
    T = TOKEN_TILE
    return pl.pallas_call(
        functools.partial(_attn_kernel, first_tile=first_tile),
        grid=(BATCH, N_KV_HEADS, N_TILES - first_tile),
        in_specs=[
            pl.BlockSpec((1, Q_GROUP, T, HEAD_DIM), lambda b, h, t: (b, h, t + first_tile, 0)),
            pl.BlockSpec((1, 1, LALL, HEAD_DIM), lambda b, h, t: (b, h, 0, 0)),
            pl.BlockSpec((1, 1, N_TILES, HEAD_DIM, T), lambda b, h, t: (b, h, 0, 0, 0)),
        ],
        out_specs=pl.BlockSpec((1, T, Q_GROUP * HEAD_DIM), lambda b, h, t: (b, t, h)),
        out_shape=jax.ShapeDtypeStruct((BATCH, (N_TILES - first_tile) * T, ATTN_W), BF16),
        scratch_shapes=[
            pltpu.VMEM((2, KEY_CHUNK_TILES * T, Q_GROUP * T), F32),
            pltpu.VMEM((1, Q_GROUP * T), F32),
            pltpu.VMEM((HEAD_DIM + ONES_ROWS, Q_GROUP * T), F32),
        ],
        compiler_params=pltpu.CompilerParams(
            dimension_semantics=("arbitrary", "arbitrary", "arbitrary"),
            vmem_limit_bytes=_vmem_limit(32 * 1024 * 1024)),
        name="attention",
    )(q, k, vt)


def _softplus(z):
    return jnp.maximum(z, 0.0) + jnp.log1p(jnp.exp(-jnp.abs(z)))


def _with_halo(ref, col, c):
    T = TOKEN_TILE
    r0 = pl.multiple_of(c * T, T)
    main = ref[0, pl.ds(r0, T), col:col + LRU_W]
    p0 = pl.multiple_of(jnp.maximum(r0 - V7X_SUBLANES, 0), V7X_SUBLANES)
    n0 = pl.multiple_of(jnp.minimum(r0 + T, LALL - V7X_SUBLANES), V7X_SUBLANES)
    prev = ref[0, pl.ds(p0, V7X_SUBLANES), col:col + LRU_W]
    nxt = ref[0, pl.ds(n0, V7X_SUBLANES), col:col + LRU_W]
    prev = jnp.where(c >= 2, prev, 0.0)
    nxt = jnp.where((c >= 1) & (c <= N_TILES - 2), nxt, 0.0)
    return main, jnp.concatenate([prev, main, nxt], axis=0)


def _shift_rows(ext, k):
    n = ext.shape[0]
    if k == 0:
        return ext[V7X_SUBLANES:V7X_SUBLANES + TOKEN_TILE]
    return pltpu.roll(ext, (-k) % n, 0)[V7X_SUBLANES:V7X_SUBLANES + TOKEN_TILE]


def _lru_pool_kernel(rest_ref, cw_ref, cb_ref, gw_ref, gb_ref, lam_ref, pw_ref, pb_ref, ps_ref,
                     rec_ref, pool_ref, hf_ref, hb_ref, a_ref, u_ref):
    T = TOKEN_TILE
    S = V7X_SUBLANES
    grouped = (GROUPS_PER_TILE, S, LRU_W)
    row_in_group = lax.broadcasted_iota(jnp.int32, grouped, 1)

    def local_scan(c, d):
        main, ext = _with_halo(rest_ref, 0, c)
        u = cb_ref[...] + (cw_ref[0:1] * _shift_rows(ext, -2) + cw_ref[1:2] * _shift_rows(ext, -1)
                           + cw_ref[2:3] * main + cw_ref[3:4] * _shift_rows(ext, 1))
        g = _dot(u.astype(BF16), gw_ref[:, d * 2 * LRU_W:(d + 1) * 2 * LRU_W])
        g = jax.nn.sigmoid(g + gb_ref[:, d * 2 * LRU_W:(d + 1) * 2 * LRU_W])
        log_a = (-LRU_C * g[:, 0:LRU_W]) * _softplus(-lam_ref[d:d + 1])
        a = jnp.exp(log_a)
        th = jnp.tanh(log_a)
        neg_expm1 = (-2.0 * th) / (1.0 - th)
        root = jnp.where(neg_expm1 > 0.0, neg_expm1 * lax.rsqrt(neg_expm1), 0.0)
        v = root * (g[:, LRU_W:2 * LRU_W] * u)
        a, v = a.reshape(grouped), v.reshape(grouped)
        for step in (1, 2, 4):
            if d == 0:
                a_s, v_s = pltpu.roll(a, step, 1), pltpu.roll(v, step, 1)
                keep = row_in_group >= step
            else:
                a_s, v_s = pltpu.roll(a, S - step, 1), pltpu.roll(v, S - step, 1)
                keep = row_in_group < S - step
            v = jnp.where(keep, a * v_s + v, v)
            a = jnp.where(keep, a * a_s, a)
        a_ref[d] = a.reshape(T, LRU_W)
        u_ref[d] = v.reshape(T, LRU_W)

    def tile_step(i, carry):
        h_f, h_b = carry
        c_f = i
        c_b = jnp.where(i == 0, 0, N_TILES - i)
        local_scan(c_f, 0)
        local_scan(c_b, 1)
        r_f = pl.multiple_of(c_f * T, T)
        r_b = pl.multiple_of(c_b * T, T)
        for gi in range(GROUPS_PER_TILE):
            lo = gi * S
            a, v = a_ref[0, lo:lo + S], u_ref[0, lo:lo + S]
            hf_ref[pl.ds(r_f + lo, S), :] = v + a * h_f
            h_f = (jnp.broadcast_to(v[S - 1:S], (S, LRU_W))
                   + jnp.broadcast_to(a[S - 1:S], (S, LRU_W)) * h_f)
            lo = (GROUPS_PER_TILE - 1 - gi) * S
            a, v = a_ref[1, lo:lo + S], u_ref[1, lo:lo + S]
            hb_ref[pl.ds(r_b + lo, S), :] = v + a * h_b
            h_b = (jnp.broadcast_to(v[0:1], (S, LRU_W))
                   + jnp.broadcast_to(a[0:1], (S, LRU_W)) * h_b)
        return h_f, h_b

    zeros = jnp.zeros((S, LRU_W), F32)
    lax.fori_loop(0, N_TILES, tile_step, (zeros, zeros))

    lane = lax.broadcasted_iota(jnp.int32, (T, POOL_W), 1)
    half_win = jnp.where(lane < POOL_GW, POOL_WINDOWS[0] // 2,
                         jnp.where(lane < 2 * POOL_GW, POOL_WINDOWS[1] // 2,
                                   jnp.where(lane < 3 * POOL_GW, POOL_WINDOWS[2] // 2,
                                             POOL_WINDOWS[3] // 2)))
    row = lax.broadcasted_iota(jnp.int32, (T, POOL_W), 0)

    def out_step(c, _):
        r0 = pl.multiple_of(c * T, T)
        gate = rest_ref[0, pl.ds(r0, T), LRU_W:2 * LRU_W]
        h = hf_ref[pl.ds(r0, T), :] + hb_ref[pl.ds(r0, T), :]
        rec_ref[0, pl.ds(r0, T), :] = (jax.nn.gelu(gate) * h).astype(BF16)

        px, ext = _with_halo(rest_ref, 2 * LRU_W, c)
        p2 = ext + pltpu.roll(ext, 1, 0)
        p4 = p2 + pltpu.roll(p2, 2, 0)
        p8 = p4 + pltpu.roll(p4, 4, 0)
        p16 = p8 + pltpu.roll(p8, 8, 0)
        win = jnp.where(lane < POOL_GW, _shift_rows(p2, 0),
                        jnp.where(lane < 2 * POOL_GW, _shift_rows(p4, 1),
                                  jnp.where(lane < 3 * POOL_GW, _shift_rows(p8, 3),
                                            _shift_rows(p16, 7))))
        seg_len = jnp.where(c == 0, CTX_LEN, SEQ)
        pos = row + jnp.where(c == 0, 0, r0 - CTX_LEN)
        cnt = jnp.minimum(pos + half_win, seg_len) - jnp.maximum(pos - half_win, 0)
        d = win / cnt.astype(F32) - px
        y = _dot(d.astype(BF16), pw_ref[...]) + pb_ref[...]
        pool_ref[0, pl.ds(r0, T), :] = (y * ps_ref[...]).astype(BF16)
        return 0

    lax.fori_loop(0, N_TILES, out_step, 0)


def _lru_pool(l, rest, conv_w, conv_b, gate_w, gate_b, lam, pool_w, pool_b, pool_scale):
    seq = lambda w: pl.BlockSpec((1, LALL, w), lambda b: (b, 0, 0))
    return pl.pallas_call(
        _lru_pool_kernel,
        grid=(BATCH,),
        in_specs=[
            seq(REST_W),
            _layer_spec(l, (CONV_W, LRU_W)), _layer_spec(l, (1, LRU_W)),
            _layer_spec(l, (LRU_W, 4 * LRU_W)), _layer_spec(l, (1, 4 * LRU_W)), _layer_spec(l, (2, LRU_W)),
            _layer_spec(l, (POOL_W, POOL_W)), _layer_spec(l, (1, POOL_W)), _layer_spec(l, (1, POOL_W)),
        ],
        out_specs=[seq(LRU_W), seq(POOL_W)],
        out_shape=[jax.ShapeDtypeStruct((BATCH, LALL, LRU_W), BF16),
                   jax.ShapeDtypeStruct((BATCH, LALL, POOL_W), BF16)],
        scratch_shapes=[
            pltpu.VMEM((LALL, LRU_W), F32), pltpu.VMEM((LALL, LRU_W), F32),
            pltpu.VMEM((2, TOKEN_TILE, LRU_W), F32), pltpu.VMEM((2, TOKEN_TILE, LRU_W), F32),
        ],
        compiler_params=pltpu.CompilerParams(
            dimension_semantics=("arbitrary",),
            vmem_limit_bytes=_vmem_limit(56 * 1024 * 1024)),
        name="lru_pool",
    )(rest, conv_w, conv_b, gate_w, gate_b, lam, pool_w, pool_b, pool_scale)


def _out_ffn_kernel(att_ref, rec_ref, pool_ref, ctx_ref, x_ref, mod_ref, g_ref, wo_ref, wi_ref, wf_ref,
                    o_ref, *, first_tile):
    mod = mod_ref[0]
    y = (_dot(att_ref[0], wo_ref[0:ATTN_W])
         + _dot(rec_ref[0], wo_ref[ATTN_W:ATTN_W + LRU_W])
         + _dot(pool_ref[0], wo_ref[ATTN_W + LRU_W:MIX_W]))
    x = x_ref[0]
    if first_tile == 0:
        x = jnp.where(pl.program_id(1) == 0, ctx_ref[0], x)
    x = x + mod[2:3] * (_rms(y) * g_ref[1:2])
    h = (_rms(x) * g_ref[2:3]) * (1.0 + mod[4:5]) + mod[3:4]
    gu = _dot(h.astype(BF16), wi_ref[...])
    g, u = gu[:, 0:FFN_HIDDEN], gu[:, FFN_HIDDEN:2 * FFN_HIDDEN]
    f = _dot((g * jax.nn.sigmoid(g) * u).astype(BF16), wf_ref[...])
    o_ref[0] = x + mod[5:6] * (_rms(f) * g_ref[3:4])


def _out_ffn(l, att, rec, pool, ctx_src, x_src, separate, mod, norm_g, w_out, w_ffn_in, w_ffn_out,
             first_tile):
    T = TOKEN_TILE
    n_t = N_TILES - first_tile
    tok = lambda w: pl.BlockSpec((1, T, w), lambda b, t: (b, t + first_tile, 0))
    weight = lambda shape: _layer_spec(l, shape, pipeline_mode=pl.Buffered(1))
    return pl.pallas_call(
        functools.partial(_out_ffn_kernel, first_tile=first_tile),
        grid=(BATCH, n_t),
        in_specs=[
            pl.BlockSpec((1, T, ATTN_W), lambda b, t: (b, t, 0)),
            tok(LRU_W), tok(POOL_W),
            *_token_specs(D_MODEL, first_tile, separate),
            pl.BlockSpec((None, 1, 6, D_MODEL), lambda b, t: (l, _mod_row(b, t + first_tile), 0, 0)),
            weight((4, D_MODEL)),
            weight((MIX_W, D_MODEL)),
            weight((D_MODEL, 2 * FFN_HIDDEN)),
            weight((FFN_HIDDEN, D_MODEL)),
        ],
        out_specs=pl.BlockSpec((1, T, D_MODEL), lambda b, t: (b, t, 0)),
        out_shape=jax.ShapeDtypeStruct((BATCH, n_t * T, D_MODEL), F32),
        compiler_params=pltpu.CompilerParams(
            dimension_semantics=("arbitrary", "arbitrary"),
            vmem_limit_bytes=_vmem_limit(48 * 1024 * 1024)),
        name="out_ffn",
    )(att, rec, pool, ctx_src, x_src, mod, norm_g, w_out, w_ffn_in, w_ffn_out)


def _position_tables():
    half = ROPE_HALF
    freq = (ROPE_THETA ** (-np.arange(half, dtype=np.float32) / half)).astype(np.float32)
    p = np.arange(SEQ)
    ang_row = (p // GRID_W).astype(np.float32)[:, None] * freq
    ang_col = (p % GRID_W).astype(np.float32)[:, None] * freq
    ang = np.concatenate([ang_row, ang_row, ang_col, ang_col], axis=1)
    first = np.tile(np.arange(HEAD_DIM) % (2 * half) < half, (SEQ, 1))
    cos = np.cos(ang)
    sin_a = np.where(first, -np.sin(ang), 0.0)
    sin_b = np.where(first, 0.0, np.sin(ang))
    pad = lambda a, v: np.concatenate([np.full((CTX_LEN, HEAD_DIM), v), a], axis=0)
    two = lambda a: np.tile(a, (1, V7X_LANES // HEAD_DIM)).astype(np.float32)
    return two(pad(cos, 1.0)), two(pad(sin_a, 0.0)), two(pad(sin_b, 0.0))


def _head_mean_matrix(width):
    blk = np.kron(np.eye(width // HEAD_DIM), np.full((HEAD_DIM, HEAD_DIM), 1.0 / HEAD_DIM))
    return jnp.asarray(blk, dtype=BF16)


def _block_diag(w):
    n, c, d = w.shape[-3:]
    on_diag = jnp.eye(n, dtype=bool)[:, None, :, None]
    out = jnp.where(on_diag, w[..., :, :, None, :], 0.0)
    return out.reshape(w.shape[:-3] + (n * c, n * d))


def kernel(x, c, ctx, c_ctx, w_mod, b_mod, norm_g, w_in, q_norm_g, k_norm_g, lru_conv_w, lru_conv_b,
           lru_gate_w, lru_gate_b, lru_lambda, pool_w, pool_b, pool_scale, w_out, w_ffn_in, w_ffn_out):
    assert x.shape == (BATCH, SEQ, D_MODEL) and ctx.shape == (BATCH, CTX_LEN, D_MODEL)
    cos, sin_a, sin_b = (jnp.asarray(a) for a in _position_tables())
    consts = (_head_mean_matrix(ATTN_W), _head_mean_matrix(KV_W), cos, sin_a, sin_b)

    c_all = jnp.zeros((MOD_ROWS, D_MODEL), F32).at[0:BATCH].set(c).at[BATCH].set(c_ctx)
    mod = _modulation(c_all, w_mod, b_mod).reshape(DEPTH, MOD_ROWS, 6, D_MODEL)

    w_in_b, w_out_b = w_in.astype(BF16), w_out.astype(BF16)
    w_ffn_in_b, w_ffn_out_b = w_ffn_in.astype(BF16), w_ffn_out.astype(BF16)
    gq = jnp.tile(q_norm_g, (1, N_Q_HEADS))[:, None, :]
    gk = jnp.tile(k_norm_g, (1, N_KV_HEADS))[:, None, :]
    gate_w = _block_diag(lru_gate_w)
    gate_w = gate_w.transpose(0, 3, 1, 2, 4).reshape(DEPTH, LRU_W, 4 * LRU_W).astype(BF16)
    gate_b = lru_gate_b.reshape(DEPTH, 1, 4 * LRU_W)
    pool_w_b = _block_diag(pool_w).astype(BF16)
    conv_b, pool_b3, pool_s3 = lru_conv_b[:, None, :], pool_b[:, None, :], pool_scale[:, None, :]

    ctx_src, x_src, separate = ctx, x, True
    for l in range(DEPTH):
        first_tile = 1 if l == DEPTH - 1 else 0
        q, k, vt, rest = _in_projection(l, ctx_src, x_src, separate, mod, norm_g, w_in_b, consts, gq, gk)
        att = _attention(q, k, vt, first_tile)
        rec, pool = _lru_pool(l, rest, lru_conv_w, conv_b, gate_w, gate_b, lru_lambda,
                              pool_w_b, pool_b3, pool_s3)
        xa = _out_ffn(l, att, rec, pool, ctx_src, x_src, separate, mod, norm_g,
                      w_out_b, w_ffn_in_b, w_ffn_out_b, first_tile)
        ctx_src, x_src, separate = xa, xa, False
    return xa
```

```python
import functools

import numpy as np
import jax
import jax.numpy as jnp
from jax import lax
from jax.experimental import pallas as pl
from jax.experimental.pallas import tpu as pltpu

D_MODEL = 1024
BATCH = 4
SEQ = 4096
DEPTH = 2
GRID_W = 64
CTX_LEN = 256
LALL = CTX_LEN + SEQ

N_Q_HEADS = 8
N_KV_HEADS = 2
HEAD_DIM = 64
Q_GROUP = N_Q_HEADS // N_KV_HEADS
ATTN_W = N_Q_HEADS * HEAD_DIM
KV_W = N_KV_HEADS * HEAD_DIM
ROPE_THETA = 10000.0
LRU_W = D_MODEL // 4
LRU_BLOCKS = 4
LRU_BW = LRU_W // LRU_BLOCKS
CONV_W = 4
LRU_C = 8.0
POOL_W = D_MODEL // 4
POOL_GROUPS = 4
POOL_GW = POOL_W // POOL_GROUPS
POOL_WINDOWS = (2, 4, 8, 16)
MIX_W = ATTN_W + LRU_W + POOL_W
IN_W = ATTN_W + 2 * KV_W + 2 * LRU_W + POOL_W
REST_W = 2 * LRU_W + POOL_W
FFN_HIDDEN = -(-8 * D_MODEL // 768) * 256
RMS_EPS = 1e-6

V7X_LANES = 128
V7X_SUBLANES = 8
V7X_VMEM_BYTES = 64 * 1024 * 1024

TOKEN_TILE = CTX_LEN
N_TILES = LALL // TOKEN_TILE
MOD_ROWS = 8
MOD_TILE_N = 1536
ROPE_HALF = HEAD_DIM // 4
GROUPS_PER_TILE = TOKEN_TILE // V7X_SUBLANES
ONES_ROWS = 16
KEY_CHUNK_TILES = 1
FFN_TILES_PER_STEP = 2
FFN_CHUNK = 512
FFN_CHUNKS = [(lo, min(FFN_CHUNK, FFN_HIDDEN - lo)) for lo in range(0, FFN_HIDDEN, FFN_CHUNK)]
Q_SCALE = HEAD_DIM ** -0.5 * float(np.log2(np.e))

F32 = jnp.float32
BF16 = jnp.bfloat16


def _vmem_limit(nbytes):
    return int(min(max(nbytes, 16 * 1024 * 1024), V7X_VMEM_BYTES - 6 * 1024 * 1024))


def _rms(x):
    return x * lax.rsqrt(jnp.mean(x * x, axis=-1, keepdims=True) + RMS_EPS)


def _dot(a, b):
    return jnp.dot(a, b, preferred_element_type=F32)


def _layer_spec(l, shape, **kw):
    return pl.BlockSpec((None,) + shape, lambda *_: (l,) + (0,) * len(shape), **kw)


def _mod_row(b, t):
    return jnp.where(t == 0, BATCH, b)


def _token_specs(width):
    T = TOKEN_TILE
    ctx_spec = pl.BlockSpec((1, T, width), lambda b, t: (b, 0, 0))
    x_spec = pl.BlockSpec((1, T, width), lambda b, t: (b, jnp.maximum(t - 1, 0), 0))
    return ctx_spec, x_spec


def _mod_kernel(c_ref, w_ref, b_ref, o_ref):
    c = c_ref[...]
    h = (c * jax.nn.sigmoid(c)).astype(BF16)
    o_ref[0] = _dot(h, w_ref[0].astype(BF16)) + b_ref[0]


def _modulation(c_all, w_mod, b_mod):
    n = 6 * D_MODEL
    return pl.pallas_call(
        _mod_kernel,
        grid=(DEPTH, n // MOD_TILE_N),
        in_specs=[
            pl.BlockSpec((MOD_ROWS, D_MODEL), lambda l, j: (0, 0)),
            pl.BlockSpec((1, D_MODEL, MOD_TILE_N), lambda l, j: (l, 0, j)),
            pl.BlockSpec((1, 1, MOD_TILE_N), lambda l, j: (l, 0, j)),
        ],
        out_specs=pl.BlockSpec((1, MOD_ROWS, MOD_TILE_N), lambda l, j: (l, 0, j)),
        out_shape=jax.ShapeDtypeStruct((DEPTH, MOD_ROWS, n), F32),
        compiler_params=pltpu.CompilerParams(
            dimension_semantics=("arbitrary", "arbitrary"),
            vmem_limit_bytes=_vmem_limit(4 * D_MODEL * MOD_TILE_N * 4)),
        name="modulation",
    )(c_all, w_mod, b_mod.reshape(DEPTH, 1, n))


def _head_norm_rope(z, s_ref, g, cos, sin_a, sin_b):
    z2 = z * z
    hi = z2.astype(BF16)
    lo = (z2 - hi.astype(F32)).astype(BF16)
    ms = _dot(hi, s_ref[...]) + _dot(lo, s_ref[...])
    zn = (z * lax.rsqrt(ms + RMS_EPS)) * g
    outs = []
    for c in range(z.shape[1] // V7X_LANES):
        zc = zn[:, c * V7X_LANES:(c + 1) * V7X_LANES]
        up = pltpu.roll(zc, V7X_LANES - ROPE_HALF, 1)
        dn = pltpu.roll(zc, ROPE_HALF, 1)
        outs.append(zc * cos + up * sin_a + dn * sin_b)
    return outs


def _inproj_kernel(ctx_ref, x_ref, mod_ref, g_ref, w_ref, sq_ref, sk_ref, gq_ref, gk_ref,
                   cos_ref, sa_ref, sb_ref, q_ref, k_ref, vt_ref, rest_ref):
    x = jnp.where(pl.program_id(1) == 0, ctx_ref[0], x_ref[0])
    mod = mod_ref[0]
    h = ((_rms(x) * g_ref[0:1]) * (1.0 + mod[1:2]) + mod[0:1]).astype(BF16)
    cos, sin_a, sin_b = cos_ref[...], sa_ref[...], sb_ref[...]
    qkv_w = ATTN_W + 2 * KV_W
    y = _dot(h, w_ref[:, 0:qkv_w])
    q_cols = _head_norm_rope(y[:, 0:ATTN_W], sq_ref, gq_ref[...], cos, sin_a, sin_b)
    (kc,) = _head_norm_rope(y[:, ATTN_W:ATTN_W + KV_W], sk_ref, gk_ref[...], cos, sin_a, sin_b)
    rest_ref[0] = _dot(h, w_ref[:, qkv_w:IN_W])

    for c, qc in enumerate(q_cols):
        qc = (qc * Q_SCALE).astype(BF16)
        q_ref[0, 2 * c] = qc[:, 0:HEAD_DIM]
        q_ref[0, 2 * c + 1] = qc[:, HEAD_DIM:2 * HEAD_DIM]

    kc = kc.astype(BF16)
    k_ref[0, 0] = kc[:, 0:HEAD_DIM]
    k_ref[0, 1] = kc[:, HEAD_DIM:2 * HEAD_DIM]

    vt = y[:, ATTN_W + KV_W:qkv_w].T.astype(BF16)
    vt_ref[0, 0, 0] = vt[0:HEAD_DIM]
    vt_ref[0, 1, 0] = vt[HEAD_DIM:2 * HEAD_DIM]


def _in_projection(l, ctx_src, x_src, mod, norm_g, w_in, consts, gq, gk):
    sq, sk, cos, sin_a, sin_b = consts
    T = TOKEN_TILE
    full = lambda shape: pl.BlockSpec(shape, lambda b, t: (0,) * len(shape))
    tab = pl.BlockSpec((T, V7X_LANES), lambda b, t: (t, 0))
    return pl.pallas_call(
        _inproj_kernel,
        grid=(BATCH, N_TILES),
        in_specs=[
            *_token_specs(D_MODEL),
            pl.BlockSpec((None, 1, 6, D_MODEL), lambda b, t: (l, _mod_row(b, t), 0, 0)),
            _layer_spec(l, (4, D_MODEL)),
            _layer_spec(l, (D_MODEL, IN_W)),
            full((ATTN_W, ATTN_W)),
            full((KV_W, KV_W)),
            _layer_spec(l, (1, ATTN_W)),
            _layer_spec(l, (1, KV_W)),
            tab, tab, tab,
        ],
        out_specs=[
            pl.BlockSpec((1, N_Q_HEADS, T, HEAD_DIM), lambda b, t: (b, 0, t, 0)),
            pl.BlockSpec((1, N_KV_HEADS, T, HEAD_DIM), lambda b, t: (b, 0, t, 0)),
            pl.BlockSpec((1, N_KV_HEADS, 1, HEAD_DIM, T), lambda b, t: (b, 0, t, 0, 0)),
            pl.BlockSpec((1, T, REST_W), lambda b, t: (b, t, 0)),
        ],
        out_shape=[
            jax.ShapeDtypeStruct((BATCH, N_Q_HEADS, LALL, HEAD_DIM), BF16),
            jax.ShapeDtypeStruct((BATCH, N_KV_HEADS, LALL, HEAD_DIM), BF16),
            jax.ShapeDtypeStruct((BATCH, N_KV_HEADS, N_TILES, HEAD_DIM, T), BF16),
            jax.ShapeDtypeStruct((BATCH, LALL, REST_W), F32),
        ],
        compiler_params=pltpu.CompilerParams(
            dimension_semantics=("arbitrary", "arbitrary"),
            vmem_limit_bytes=_vmem_limit(40 * 1024 * 1024)),
        name="in_projection",
    )(ctx_src, x_src, mod, norm_g, w_in, sq, sk, gq, gk, cos, sin_a, sin_b)


def _attn_kernel(q_ref, k_ref, vt_ref, o_ref, s_ref, m_ref, acc_ref, *, first_tile):
    T = TOKEN_TILE
    n_q = Q_GROUP * T
    t = pl.program_id(2) + first_tile
    q = q_ref[0].reshape(n_q, HEAD_DIM)
    chunks = [(0, 1)] + [(j, KEY_CHUNK_TILES) for j in range(1, N_TILES, KEY_CHUNK_TILES)]

    def scores(c):
        j, n = chunks[c]
        s_ref[c % 2, 0:n * T] = lax.dot_general(
            k_ref[0, 0, j * T:(j + n) * T, :], q, (((1,), (1,)), ((), ())),
            preferred_element_type=F32)

    def update(c):
        j, n = chunks[c]
        vt = jnp.concatenate([vt_ref[0, 0, j + i] for i in range(n)], axis=1)
        vt1 = jnp.concatenate([vt, jnp.ones((ONES_ROWS, n * T), BF16)], axis=0)
        for g in range(Q_GROUP):
            cols = slice(g * T, (g + 1) * T)
            s = s_ref[c % 2, 0:n * T, cols]
            m = m_ref[:, cols]
            m_new = jnp.maximum(m, jnp.max(s, axis=0, keepdims=True))
            alpha = jnp.exp2(m - m_new)
            p = jnp.exp2((s - m_new).astype(BF16))
            m_ref[:, cols] = m_new
            acc_ref[:, cols] = alpha * acc_ref[:, cols] + _dot(vt1, p)

    def latent_chunks():
        for c in range(1, len(chunks)):
            if c + 1 < len(chunks):
                scores(c + 1)
            update(c)

    m_ref[...] = jnp.full((1, n_q), -1e30, F32)
    acc_ref[...] = jnp.zeros((HEAD_DIM + ONES_ROWS, n_q), F32)
    scores(0)
    scores(1)
    update(0)
    if first_tile == 0:
        pl.when(t != 0)(latent_chunks)
    else:
        latent_chunks()
    o = acc_ref[0:HEAD_DIM] / acc_ref[HEAD_DIM:HEAD_DIM + 1]
    o = jnp.concatenate([o[:, g * T:(g + 1) * T] for g in range(Q_GROUP)], axis=0)
    o_ref[0] = o.T.astype(BF16)


def _attention(q, k, vt, first_tile):
    T = TOKEN_TILE
    return pl.pallas_call(
        functools.partial(_attn_kernel, first_tile=first_tile),
        grid=(BATCH, N_KV_HEADS, N_TILES - first_tile),
        in_specs=[
            pl.BlockSpec((1, Q_GROUP, T, HEAD_DIM), lambda b, h, t: (b, h, t + first_tile, 0)),
            pl.BlockSpec((1, 1, LALL, HEAD_DIM), lambda b, h, t: (b, h, 0, 0)),
            pl.BlockSpec((1, 1, N_TILES, HEAD_DIM, T), lambda b, h, t: (b, h, 0, 0, 0)),
        ],
        out_specs=pl.BlockSpec((1, T, Q_GROUP * HEAD_DIM), lambda b, h, t: (b, t, h)),
        out_shape=jax.ShapeDtypeStruct((BATCH, (N_TILES - first_tile) * T, ATTN_W), BF16),
        scratch_shapes=[
            pltpu.VMEM((2, KEY_CHUNK_TILES * T, Q_GROUP * T), F32),
            pltpu.VMEM((1, Q_GROUP * T), F32),
            pltpu.VMEM((HEAD_DIM + ONES_ROWS, Q_GROUP * T), F32),
        ],
        compiler_params=pltpu.CompilerParams(
            dimension_semantics=("arbitrary", "arbitrary", "arbitrary"),
            vmem_limit_bytes=_vmem_limit(32 * 1024 * 1024)),
        name="attention",
    )(q, k, vt)


def _softplus(z):
    return jnp.maximum(z, 0.0) + jnp.log1p(jnp.exp(-jnp.abs(z)))


def _with_halo(ref, col, c):
    T = TOKEN_TILE
    r0 = pl.multiple_of(c * T, T)
    main = ref[0, pl.ds(r0, T), col:col + LRU_W]
    p0 = pl.multiple_of(jnp.maximum(r0 - V7X_SUBLANES, 0), V7X_SUBLANES)
    n0 = pl.multiple_of(jnp.minimum(r0 + T, LALL - V7X_SUBLANES), V7X_SUBLANES)
    prev = ref[0, pl.ds(p0, V7X_SUBLANES), col:col + LRU_W]
    nxt = ref[0, pl.ds(n0, V7X_SUBLANES), col:col + LRU_W]
    prev = jnp.where(c >= 2, prev, 0.0)
    nxt = jnp.where((c >= 1) & (c <= N_TILES - 2), nxt, 0.0)
    return main, jnp.concatenate([prev, main, nxt], axis=0)


def _shift_rows(ext, k):
    n = ext.shape[0]
    if k == 0:
        return ext[V7X_SUBLANES:V7X_SUBLANES + TOKEN_TILE]
    return pltpu.roll(ext, (-k) % n, 0)[V7X_SUBLANES:V7X_SUBLANES + TOKEN_TILE]


def _lru_pool_kernel(rest_ref, cw_ref, cb_ref, gw_ref, gb_ref, lam_ref, pw_ref, pb_ref, ps_ref,
                     rec_ref, pool_ref, hf_ref, hb_ref, a_ref, u_ref, conv_ref):
    T = TOKEN_TILE
    S = V7X_SUBLANES
    grouped = (GROUPS_PER_TILE, S, LRU_W)
    row_in_group = lax.broadcasted_iota(jnp.int32, grouped, 1)

    def conv_step(c, _):
        main, ext = _with_halo(rest_ref, 0, c)
        conv_ref[pl.ds(pl.multiple_of(c * T, T), T), :] = cb_ref[...] + (
            cw_ref[0:1] * _shift_rows(ext, -2) + cw_ref[1:2] * _shift_rows(ext, -1)
            + cw_ref[2:3] * main + cw_ref[3:4] * _shift_rows(ext, 1))
        return 0

    lax.fori_loop(0, N_TILES, conv_step, 0)

    def local_scan(c, d):
        u = conv_ref[pl.ds(pl.multiple_of(c * T, T), T), :]
        g = _dot(u.astype(BF16), gw_ref[:, d * 2 * LRU_W:(d + 1) * 2 * LRU_W])
        g = jax.nn.sigmoid(g + gb_ref[:, d * 2 * LRU_W:(d + 1) * 2 * LRU_W])
        log_a = (-LRU_C * g[:, 0:LRU_W]) * _softplus(-lam_ref[d:d + 1])
        a = jnp.exp(log_a)
        neg_expm1 = (1.0 - a) * (1.0 + a)
        root = jnp.where(neg_expm1 > 0.0, neg_expm1 * lax.rsqrt(neg_expm1), 0.0)
        v = root * (g[:, LRU_W:2 * LRU_W] * u)
        a, v = a.reshape(grouped), v.reshape(grouped)
        for step in (1, 2, 4):
            if d == 0:
                a_s, v_s = pltpu.roll(a, step, 1), pltpu.roll(v, step, 1)
                keep = row_in_group >= step
            else:
                a_s, v_s = pltpu.roll(a, S - step, 1), pltpu.roll(v, S - step, 1)
                keep = row_in_group < S - step
            v = jnp.where(keep, a * v_s + v, v)
            a = jnp.where(keep, a * a_s, a)
        a_ref[d] = a.reshape(T, LRU_W)
        u_ref[d] = v.reshape(T, LRU_W)

    def tile_step(i, carry):
        h_f, h_b = carry
        c_f = i
        c_b = jnp.where(i == 0, 0, N_TILES - i)
        local_scan(c_f, 0)
        local_scan(c_b, 1)
        r_f = pl.multiple_of(c_f * T, T)
        r_b = pl.multiple_of(c_b * T, T)
        for gi in range(GROUPS_PER_TILE):
            lo = gi * S
            a, v = a_ref[0, lo:lo + S], u_ref[0, lo:lo + S]
            hf_ref[pl.ds(r_f + lo, S), :] = v + a * h_f
            h_f = (jnp.broadcast_to(v[S - 1:S], (S, LRU_W))
                   + jnp.broadcast_to(a[S - 1:S], (S, LRU_W)) * h_f)
            lo = (GROUPS_PER_TILE - 1 - gi) * S
            a, v = a_ref[1, lo:lo + S], u_ref[1, lo:lo + S]
            hb_ref[pl.ds(r_b + lo, S), :] = v + a * h_b
            h_b = (jnp.broadcast_to(v[0:1], (S, LRU_W))
                   + jnp.broadcast_to(a[0:1], (S, LRU_W)) * h_b)
        return h_f, h_b

    zeros = jnp.zeros((S, LRU_W), F32)
    lax.fori_loop(0, N_TILES, tile_step, (zeros, zeros))

    lane = lax.broadcasted_iota(jnp.int32, (T, POOL_W), 1)
    half_win = jnp.where(lane < POOL_GW, POOL_WINDOWS[0] // 2,
                         jnp.where(lane < 2 * POOL_GW, POOL_WINDOWS[1] // 2,
                                   jnp.where(lane < 3 * POOL_GW, POOL_WINDOWS[2] // 2,
                                             POOL_WINDOWS[3] // 2)))
    row = lax.broadcasted_iota(jnp.int32, (T, POOL_W), 0)

    def out_step(c, _):
        r0 = pl.multiple_of(c * T, T)
        gate = rest_ref[0, pl.ds(r0, T), LRU_W:2 * LRU_W]
        h = hf_ref[pl.ds(r0, T), :] + hb_ref[pl.ds(r0, T), :]
        rec_ref[0, pl.ds(r0, T), :] = (jax.nn.gelu(gate) * h).astype(BF16)

        px, ext = _with_halo(rest_ref, 2 * LRU_W, c)
        p2 = ext + pltpu.roll(ext, 1, 0)
        p4 = p2 + pltpu.roll(p2, 2, 0)
        p8 = p4 + pltpu.roll(p4, 4, 0)
        p16 = p8 + pltpu.roll(p8, 8, 0)
        win = jnp.where(lane < POOL_GW, _shift_rows(p2, 0),
                        jnp.where(lane < 2 * POOL_GW, _shift_rows(p4, 1),
                                  jnp.where(lane < 3 * POOL_GW, _shift_rows(p8, 3),
                                            _shift_rows(p16, 7))))
        seg_len = jnp.where(c == 0, CTX_LEN, SEQ)
        pos = row + jnp.where(c == 0, 0, r0 - CTX_LEN)
        cnt = jnp.minimum(pos + half_win, seg_len) - jnp.maximum(pos - half_win, 0)
        d = win / cnt.astype(F32) - px
        y = _dot(d.astype(BF16), pw_ref[...]) + pb_ref[...]
        pool_ref[0, pl.ds(r0, T), :] = (y * ps_ref[...]).astype(BF16)
        return 0

    lax.fori_loop(0, N_TILES, out_step, 0)


def _lru_pool(l, rest, conv_w, conv_b, gate_w, gate_b, lam, pool_w, pool_b, pool_scale):
    seq = lambda w: pl.BlockSpec((1, LALL, w), lambda b: (b, 0, 0))
    return pl.pallas_call(
        _lru_pool_kernel,
        grid=(BATCH,),
        in_specs=[
            seq(REST_W),
            _layer_spec(l, (CONV_W, LRU_W)), _layer_spec(l, (1, LRU_W)),
            _layer_spec(l, (LRU_W, 4 * LRU_W)), _layer_spec(l, (1, 4 * LRU_W)), _layer_spec(l, (2, LRU_W)),
            _layer_spec(l, (POOL_W, POOL_W)), _layer_spec(l, (1, POOL_W)), _layer_spec(l, (1, POOL_W)),
        ],
        out_specs=[seq(LRU_W), seq(POOL_W)],
        out_shape=[jax.ShapeDtypeStruct((BATCH, LALL, LRU_W), BF16),
                   jax.ShapeDtypeStruct((BATCH, LALL, POOL_W), BF16)],
        scratch_shapes=[
            pltpu.VMEM((LALL, LRU_W), F32), pltpu.VMEM((LALL, LRU_W), F32),
            pltpu.VMEM((2, TOKEN_TILE, LRU_W), F32), pltpu.VMEM((2, TOKEN_TILE, LRU_W), F32),
            pltpu.VMEM((LALL, LRU_W), F32),
        ],
        compiler_params=pltpu.CompilerParams(
            dimension_semantics=("arbitrary",),
            vmem_limit_bytes=_vmem_limit(56 * 1024 * 1024)),
        name="lru_pool",
    )(rest, conv_w, conv_b, gate_w, gate_b, lam, pool_w, pool_b, pool_scale)


def _out_ffn_kernel(*refs, n_sub):
    T = TOKEN_TILE
    att, rec, pool = refs[0:n_sub], refs[n_sub:2 * n_sub], refs[2 * n_sub:3 * n_sub]
    x_ref, mod_ref, g_ref, wo_ref, wi_ref, wf_ref, o_ref = refs[3 * n_sub:]
    mod = mod_ref[0]
    ys = [_dot(att[k][0], wo_ref[0:ATTN_W])
          + _dot(rec[k][0], wo_ref[ATTN_W:ATTN_W + LRU_W])
          + _dot(pool[k][0], wo_ref[ATTN_W + LRU_W:MIX_W]) for k in range(n_sub)]
    for k in range(n_sub):
        rows = slice(k * T, (k + 1) * T)
        x = x_ref[0, rows] + mod[2:3] * (_rms(ys[k]) * g_ref[1:2])
        h = ((_rms(x) * g_ref[2:3]) * (1.0 + mod[4:5]) + mod[3:4]).astype(BF16)
        gate_up = lambda lo, n: _dot(h, wi_ref[:, 2 * lo:2 * (lo + n)])
        gu = gate_up(*FFN_CHUNKS[0])
        f = None
        for i, (lo, n) in enumerate(FFN_CHUNKS):
            gu_next = gate_up(*FFN_CHUNKS[i + 1]) if i + 1 < len(FFN_CHUNKS) else None
            g, u = gu[:, 0:n], gu[:, n:2 * n]
            part = _dot((g * jax.nn.sigmoid(g) * u).astype(BF16), wf_ref[lo:lo + n])
            f = part if f is None else f + part
            gu = gu_next
        o_ref[0, rows] = x + mod[5:6] * (_rms(f) * g_ref[3:4])


def _out_ffn(l, att, rec, pool, x_src, mod, norm_g, w_out, w_ffn_in, w_ffn_out, *, context, att_has_ctx):
    T = TOKEN_TILE
    n_sub = 1 if context else FFN_TILES_PER_STEP
    n_steps = 1 if context else (N_TILES - 1) // n_sub
    tile = lambda k, skip: (lambda b, t: (b, 0 if context else n_sub * t + k + skip, 0))
    weight = lambda shape: _layer_spec(l, shape, pipeline_mode=pl.Buffered(1))
    subs = range(n_sub)
    return pl.pallas_call(
        functools.partial(_out_ffn_kernel, n_sub=n_sub),
        grid=(BATCH, n_steps),
        in_specs=[
            *[pl.BlockSpec((1, T, ATTN_W), tile(k, 1 if att_has_ctx else 0)) for k in subs],
            *[pl.BlockSpec((1, T, LRU_W), tile(k, 1)) for k in subs],
            *[pl.BlockSpec((1, T, POOL_W), tile(k, 1)) for k in subs],
            pl.BlockSpec((1, n_sub * T, D_MODEL), lambda b, t: (b, t, 0)),
            pl.BlockSpec((None, 1, 6, D_MODEL), lambda b, t: (l, BATCH if context else b, 0, 0)),
            weight((4, D_MODEL)),
            weight((MIX_W, D_MODEL)),
            weight((D_MODEL, 2 * FFN_HIDDEN)),
            weight((FFN_HIDDEN, D_MODEL)),
        ],
        out_specs=pl.BlockSpec((1, n_sub * T, D_MODEL), lambda b, t: (b, t, 0)),
        out_shape=jax.ShapeDtypeStruct((BATCH, n_steps * n_sub * T, D_MODEL), F32),
        compiler_params=pltpu.CompilerParams(
            dimension_semantics=("arbitrary", "arbitrary"),
            vmem_limit_bytes=_vmem_limit(52 * 1024 * 1024)),
        name="out_ffn_ctx" if context else "out_ffn",
    )(*([att] * n_sub), *([rec] * n_sub), *([pool] * n_sub), x_src, mod, norm_g,
      w_out, w_ffn_in, w_ffn_out)


def _position_tables():
    half = ROPE_HALF
    freq = (ROPE_THETA ** (-np.arange(half, dtype=np.float32) / half)).astype(np.float32)
    p = np.arange(SEQ)
    ang_row = (p // GRID_W).astype(np.float32)[:, None] * freq
    ang_col = (p % GRID_W).astype(np.float32)[:, None] * freq
    ang = np.concatenate([ang_row, ang_row, ang_col, ang_col], axis=1)
    first = np.tile(np.arange(HEAD_DIM) % (2 * half) < half, (SEQ, 1))
    cos = np.cos(ang)
    sin_a = np.where(first, -np.sin(ang), 0.0)
    sin_b = np.where(first, 0.0, np.sin(ang))
    pad = lambda a, v: np.concatenate([np.full((CTX_LEN, HEAD_DIM), v), a], axis=0)
    two = lambda a: np.tile(a, (1, V7X_LANES // HEAD_DIM)).astype(np.float32)
    return two(pad(cos, 1.0)), two(pad(sin_a, 0.0)), two(pad(sin_b, 0.0))


def _head_mean_matrix(width):
    blk = np.kron(np.eye(width // HEAD_DIM), np.full((HEAD_DIM, HEAD_DIM), 1.0 / HEAD_DIM))
    return jnp.asarray(blk, dtype=BF16)


def _block_diag(w):
    n, c, d = w.shape[-3:]
    on_diag = jnp.eye(n, dtype=bool)[:, None, :, None]
    out = jnp.where(on_diag, w[..., :, :, None, :], 0.0)
    return out.reshape(w.shape[:-3] + (n * c, n * d))


def kernel(x, c, ctx, c_ctx, w_mod, b_mod, norm_g, w_in, q_norm_g, k_norm_g, lru_conv_w, lru_conv_b,
           lru_gate_w, lru_gate_b, lru_lambda, pool_w, pool_b, pool_scale, w_out, w_ffn_in, w_ffn_out):
    assert x.shape == (BATCH, SEQ, D_MODEL) and ctx.shape == (BATCH, CTX_LEN, D_MODEL)
    cos, sin_a, sin_b = (jnp.asarray(a) for a in _position_tables())
    consts = (_head_mean_matrix(ATTN_W), _head_mean_matrix(KV_W), cos, sin_a, sin_b)

    c_all = jnp.zeros((MOD_ROWS, D_MODEL), F32).at[0:BATCH].set(c).at[BATCH].set(c_ctx)
    mod = _modulation(c_all, w_mod, b_mod).reshape(DEPTH, MOD_ROWS, 6, D_MODEL)

    w_in_b, w_out_b = w_in.astype(BF16), w_out.astype(BF16)
    w_ffn_in_b = jnp.concatenate(
        [w_ffn_in[:, :, off + lo:off + lo + n] for lo, n in FFN_CHUNKS for off in (0, FFN_HIDDEN)],
        axis=2).astype(BF16)
    w_ffn_out_b = w_ffn_out.astype(BF16)
    gq = jnp.tile(q_norm_g, (1, N_Q_HEADS))[:, None, :]
    gk = jnp.tile(k_norm_g, (1, N_KV_HEADS))[:, None, :]
    gate_w = _block_diag(lru_gate_w)
    gate_w = gate_w.transpose(0, 3, 1, 2, 4).reshape(DEPTH, LRU_W, 4 * LRU_W).astype(BF16)
    gate_b = lru_gate_b.reshape(DEPTH, 1, 4 * LRU_W)
    pool_w_b = _block_diag(pool_w).astype(BF16)
    conv_b, pool_b3, pool_s3 = lru_conv_b[:, None, :], pool_b[:, None, :], pool_scale[:, None, :]

    for l in range(DEPTH):
        last = l == DEPTH - 1
        q, k, vt, rest = _in_projection(l, ctx, x, mod, norm_g, w_in_b, consts, gq, gk)
        att = _attention(q, k, vt, 1 if last else 0)
        rec, pool = _lru_pool(l, rest, lru_conv_w, conv_b, gate_w, gate_b, lru_lambda,
                              pool_w_b, pool_b3, pool_s3)
        weights = (mod, norm_g, w_out_b, w_ffn_in_b, w_ffn_out_b)
        if not last:
            ctx_next = _out_ffn(l, att, rec, pool, ctx, *weights, context=True, att_has_ctx=True)
        x = _out_ffn(l, att, rec, pool, x, *weights, context=False, att_has_ctx=not last)
        if not last:
            ctx = ctx_next
    return x
```

```python
import functools

import numpy as np
import jax
import jax.numpy as jnp
from jax import lax
from jax.experimental import pallas as pl
from jax.experimental.pallas import tpu as pltpu

D_MODEL = 1024
BATCH = 4
SEQ = 4096
DEPTH = 2
GRID_W = 64
CTX_LEN = 256
LALL = CTX_LEN + SEQ

N_Q_HEADS = 8
N_KV_HEADS = 2
HEAD_DIM = 64
Q_GROUP = N_Q_HEADS // N_KV_HEADS
ATTN_W = N_Q_HEADS * HEAD_DIM
KV_W = N_KV_HEADS * HEAD_DIM
ROPE_THETA = 10000.0
LRU_W = D_MODEL // 4
LRU_BLOCKS = 4
LRU_BW = LRU_W // LRU_BLOCKS
CONV_W = 4
LRU_C = 8.0
POOL_W = D_MODEL // 4
POOL_GROUPS = 4
POOL_GW = POOL_W // POOL_GROUPS
POOL_WINDOWS = (2, 4, 8, 16)
MIX_W = ATTN_W + LRU_W + POOL_W
IN_W = ATTN_W + 2 * KV_W + 2 * LRU_W + POOL_W
REST_W = 2 * LRU_W + POOL_W
FFN_HIDDEN = -(-8 * D_MODEL // 768) * 256
RMS_EPS = 1e-6

V7X_LANES = 128
V7X_SUBLANES = 8
V7X_VMEM_BYTES = 64 * 1024 * 1024

TOKEN_TILE = CTX_LEN
N_TILES = LALL // TOKEN_TILE
MOD_ROWS = 8
MOD_TILE_N = 1536
ROPE_HALF = HEAD_DIM // 4
GROUPS_PER_TILE = TOKEN_TILE // V7X_SUBLANES
ONES_ROWS = 16
KEY_CHUNK_TILES = 1
SCORE_SLOTS = 3
FFN_TILES_PER_STEP = 2
FFN_CHUNK = 512
FFN_CHUNKS = [(lo, min(FFN_CHUNK, FFN_HIDDEN - lo)) for lo in range(0, FFN_HIDDEN, FFN_CHUNK)]
Q_SCALE = HEAD_DIM ** -0.5 * float(np.log2(np.e))

F32 = jnp.float32
BF16 = jnp.bfloat16


def _vmem_limit(nbytes):
    return int(min(max(nbytes, 16 * 1024 * 1024), V7X_VMEM_BYTES - 6 * 1024 * 1024))


def _rms(x):
    return x * lax.rsqrt(jnp.mean(x * x, axis=-1, keepdims=True) + RMS_EPS)


def _dot(a, b):
    return jnp.dot(a, b, preferred_element_type=F32)


def _layer_spec(l, shape, **kw):
    return pl.BlockSpec((None,) + shape, lambda *_: (l,) + (0,) * len(shape), **kw)


def _mod_row(b, t):
    return jnp.where(t == 0, BATCH, b)


def _token_specs(width):
    T = TOKEN_TILE
    ctx_spec = pl.BlockSpec((1, T, width), lambda b, t: (b, 0, 0))
    x_spec = pl.BlockSpec((1, T, width), lambda b, t: (b, jnp.maximum(t - 1, 0), 0))
    return ctx_spec, x_spec


def _mod_kernel(c_ref, w_ref, b_ref, o_ref):
    c = c_ref[...]
    h = (c * jax.nn.sigmoid(c)).astype(BF16)
    o_ref[0] = _dot(h, w_ref[0].astype(BF16)) + b_ref[0]


def _modulation(c_all, w_mod, b_mod):
    n = 6 * D_MODEL
    return pl.pallas_call(
        _mod_kernel,
        grid=(DEPTH, n // MOD_TILE_N),
        in_specs=[
            pl.BlockSpec((MOD_ROWS, D_MODEL), lambda l, j: (0, 0)),
            pl.BlockSpec((1, D_MODEL, MOD_TILE_N), lambda l, j: (l, 0, j)),
            pl.BlockSpec((1, 1, MOD_TILE_N), lambda l, j: (l, 0, j)),
        ],
        out_specs=pl.BlockSpec((1, MOD_ROWS, MOD_TILE_N), lambda l, j: (l, 0, j)),
        out_shape=jax.ShapeDtypeStruct((DEPTH, MOD_ROWS, n), F32),
        compiler_params=pltpu.CompilerParams(
            dimension_semantics=("arbitrary", "arbitrary"),
            vmem_limit_bytes=_vmem_limit(4 * D_MODEL * MOD_TILE_N * 4)),
        name="modulation",
    )(c_all, w_mod, b_mod.reshape(DEPTH, 1, n))


def _head_norm_rope(z, s_ref, g, cos, sin_a, sin_b):
    z2 = z * z
    hi = z2.astype(BF16)
    lo = (z2 - hi.astype(F32)).astype(BF16)
    ms = _dot(hi, s_ref[...]) + _dot(lo, s_ref[...])
    zn = (z * lax.rsqrt(ms + RMS_EPS)) * g
    outs = []
    for c in range(z.shape[1] // V7X_LANES):
        zc = zn[:, c * V7X_LANES:(c + 1) * V7X_LANES]
        up = pltpu.roll(zc, V7X_LANES - ROPE_HALF, 1)
        dn = pltpu.roll(zc, ROPE_HALF, 1)
        outs.append(zc * cos + up * sin_a + dn * sin_b)
    return outs


def _inproj_kernel(ctx_ref, x_ref, mod_ref, g_ref, w_ref, sq_ref, sk_ref, gq_ref, gk_ref,
                   cos_ref, sa_ref, sb_ref, q_ref, k_ref, vt_ref, rest_ref):
    x = jnp.where(pl.program_id(1) == 0, ctx_ref[0], x_ref[0])
    mod = mod_ref[0]
    h = ((_rms(x) * g_ref[0:1]) * (1.0 + mod[1:2]) + mod[0:1]).astype(BF16)
    cos, sin_a, sin_b = cos_ref[...], sa_ref[...], sb_ref[...]
    qkv_w = ATTN_W + 2 * KV_W
    y = _dot(h, w_ref[:, 0:qkv_w])
    q_cols = _head_norm_rope(y[:, 0:ATTN_W], sq_ref, gq_ref[...], cos, sin_a, sin_b)
    (kc,) = _head_norm_rope(y[:, ATTN_W:ATTN_W + KV_W], sk_ref, gk_ref[...], cos, sin_a, sin_b)
    rest_ref[0] = _dot(h, w_ref[:, qkv_w:IN_W])

    for c, qc in enumerate(q_cols):
        qt = (qc * Q_SCALE).T.astype(BF16)
        q_ref[0, 2 * c, 0] = qt[0:HEAD_DIM]
        q_ref[0, 2 * c + 1, 0] = qt[HEAD_DIM:2 * HEAD_DIM]

    kc = kc.astype(BF16)
    k_ref[0, 0] = kc[:, 0:HEAD_DIM]
    k_ref[0, 1] = kc[:, HEAD_DIM:2 * HEAD_DIM]

    vt = y[:, ATTN_W + KV_W:qkv_w].T.astype(BF16)
    vt_ref[0, 0, 0] = vt[0:HEAD_DIM]
    vt_ref[0, 1, 0] = vt[HEAD_DIM:2 * HEAD_DIM]


def _in_projection(l, ctx_src, x_src, mod, norm_g, w_in, consts, gq, gk):
    sq, sk, cos, sin_a, sin_b = consts
    T = TOKEN_TILE
    full = lambda shape: pl.BlockSpec(shape, lambda b, t: (0,) * len(shape))
    tab = pl.BlockSpec((T, V7X_LANES), lambda b, t: (t, 0))
    return pl.pallas_call(
        _inproj_kernel,
        grid=(BATCH, N_TILES),
        in_specs=[
            *_token_specs(D_MODEL),
            pl.BlockSpec((None, 1, 6, D_MODEL), lambda b, t: (l, _mod_row(b, t), 0, 0)),
            _layer_spec(l, (4, D_MODEL)),
            _layer_spec(l, (D_MODEL, IN_W)),
            full((ATTN_W, ATTN_W)),
            full((KV_W, KV_W)),
            _layer_spec(l, (1, ATTN_W)),
            _layer_spec(l, (1, KV_W)),
            tab, tab, tab,
        ],
        out_specs=[
            pl.BlockSpec((1, N_Q_HEADS, 1, HEAD_DIM, T), lambda b, t: (b, 0, t, 0, 0)),
            pl.BlockSpec((1, N_KV_HEADS, T, HEAD_DIM), lambda b, t: (b, 0, t, 0)),
            pl.BlockSpec((1, N_KV_HEADS, 1, HEAD_DIM, T), lambda b, t: (b, 0, t, 0, 0)),
            pl.BlockSpec((1, T, REST_W), lambda b, t: (b, t, 0)),
        ],
        out_shape=[
            jax.ShapeDtypeStruct((BATCH, N_Q_HEADS, N_TILES, HEAD_DIM, T), BF16),
            jax.ShapeDtypeStruct((BATCH, N_KV_HEADS, LALL, HEAD_DIM), BF16),
            jax.ShapeDtypeStruct((BATCH, N_KV_HEADS, N_TILES, HEAD_DIM, T), BF16),
            jax.ShapeDtypeStruct((BATCH, LALL, REST_W), F32),
        ],
        compiler_params=pltpu.CompilerParams(
            dimension_semantics=("arbitrary", "arbitrary"),
            vmem_limit_bytes=_vmem_limit(40 * 1024 * 1024)),
        name="in_projection",
    )(ctx_src, x_src, mod, norm_g, w_in, sq, sk, gq, gk, cos, sin_a, sin_b)


def _attn_kernel(q_ref, k_ref, vt_ref, o_ref, s_ref, m_ref, acc_ref, *, first_tile):
    T = TOKEN_TILE
    n_q = Q_GROUP * T
    t = pl.program_id(2) + first_tile
    qt = jnp.concatenate([q_ref[0, g, 0] for g in range(Q_GROUP)], axis=1)
    chunks = [(0, 1)] + [(j, KEY_CHUNK_TILES) for j in range(1, N_TILES, KEY_CHUNK_TILES)]

    def scores(c, half):
        j, n = chunks[c]
        cols = slice(half * n_q // 2, (half + 1) * n_q // 2)
        s_ref[c % SCORE_SLOTS, 0:n * T, cols] = _dot(k_ref[0, 0, j * T:(j + n) * T, :], qt[:, cols])

    def update(c, half):
        j, n = chunks[c]
        vt = jnp.concatenate([vt_ref[0, 0, j + i] for i in range(n)], axis=1)
        vt1 = jnp.concatenate([vt, jnp.ones((ONES_ROWS, n * T), BF16)], axis=0)
        for g in range(half * Q_GROUP // 2, (half + 1) * Q_GROUP // 2):
            cols = slice(g * T, (g + 1) * T)
            s = s_ref[c % SCORE_SLOTS, 0:n * T, cols]
            m = m_ref[:, cols]
            m_new = jnp.maximum(m, jnp.max(s, axis=0, keepdims=True))
            alpha = jnp.exp2(m - m_new)
            p = jnp.exp2((s - m_new).astype(BF16))
            m_ref[:, cols] = m_new
            acc_ref[:, cols] = alpha * acc_ref[:, cols] + _dot(vt1, p)

    def latent_chunks():
        for c in range(1, len(chunks)):
            for half in range(2):
                if c + SCORE_SLOTS - 1 < len(chunks):
                    scores(c + SCORE_SLOTS - 1, half)
                update(c, half)

    m_ref[...] = jnp.full((1, n_q), -1e30, F32)
    acc_ref[...] = jnp.zeros((HEAD_DIM + ONES_ROWS, n_q), F32)
    for c in range(SCORE_SLOTS):
        for half in range(2):
            scores(c, half)
    for half in range(2):
        update(0, half)
    pl.when(t != 0)(latent_chunks)
    o = acc_ref[0:HEAD_DIM] / acc_ref[HEAD_DIM:HEAD_DIM + 1]
    o = jnp.concatenate([o[:, g * T:(g + 1) * T] for g in range(Q_GROUP)], axis=0)
    o_ref[0] = o.T.astype(BF16)


def _attention(q, k, vt, first_tile):
    T = TOKEN_TILE
    return pl.pallas_call(
        functools.partial(_attn_kernel, first_tile=first_tile),
        grid=(BATCH, N_KV_HEADS, N_TILES - first_tile),
        in_specs=[
            pl.BlockSpec((1, Q_GROUP, 1, HEAD_DIM, T), lambda b, h, t: (b, h, t + first_tile, 0, 0)),
            pl.BlockSpec((1, 1, LALL, HEAD_DIM), lambda b, h, t: (b, h, 0, 0)),
            pl.BlockSpec((1, 1, N_TILES, HEAD_DIM, T), lambda b, h, t: (b, h, 0, 0, 0)),
        ],
        out_specs=pl.BlockSpec((1, T, Q_GROUP * HEAD_DIM), lambda b, h, t: (b, t, h)),
        out_shape=jax.ShapeDtypeStruct((BATCH, (N_TILES - first_tile) * T, ATTN_W), BF16),
        scratch_shapes=[
            pltpu.VMEM((SCORE_SLOTS, KEY_CHUNK_TILES * T, Q_GROUP * T), F32),
            pltpu.VMEM((1, Q_GROUP * T), F32),
            pltpu.VMEM((HEAD_DIM + ONES_ROWS, Q_GROUP * T), F32),
        ],
        compiler_params=pltpu.CompilerParams(
            dimension_semantics=("arbitrary", "arbitrary", "arbitrary"),
            vmem_limit_bytes=_vmem_limit(32 * 1024 * 1024)),
        name="attention",
    )(q, k, vt)


def _softplus(z):
    return jnp.maximum(z, 0.0) + jnp.log1p(jnp.exp(-jnp.abs(z)))


def _with_halo(ref, col, c):
    T = TOKEN_TILE
    r0 = pl.multiple_of(c * T, T)
    main = ref[0, pl.ds(r0, T), col:col + LRU_W]
    p0 = pl.multiple_of(jnp.maximum(r0 - V7X_SUBLANES, 0), V7X_SUBLANES)
    n0 = pl.multiple_of(jnp.minimum(r0 + T, LALL - V7X_SUBLANES), V7X_SUBLANES)
    prev = ref[0, pl.ds(p0, V7X_SUBLANES), col:col + LRU_W]
    nxt = ref[0, pl.ds(n0, V7X_SUBLANES), col:col + LRU_W]
    prev = jnp.where(c >= 2, prev, 0.0)
    nxt = jnp.where((c >= 1) & (c <= N_TILES - 2), nxt, 0.0)
    return main, jnp.concatenate([prev, main, nxt], axis=0)


def _shift_rows(ext, k):
    n = ext.shape[0]
    if k == 0:
        return ext[V7X_SUBLANES:V7X_SUBLANES + TOKEN_TILE]
    return pltpu.roll(ext, (-k) % n, 0)[V7X_SUBLANES:V7X_SUBLANES + TOKEN_TILE]


def _lru_pool_kernel(rest_ref, cw_ref, cb_ref, gw_ref, gb_ref, lam_ref, pw_ref, pb_ref, ps_ref,
                     rec_ref, pool_ref, hf_ref, hb_ref, a_ref, u_ref, conv_ref):
    T = TOKEN_TILE
    S = V7X_SUBLANES
    grouped = (GROUPS_PER_TILE, S, LRU_W)
    row_in_group = lax.broadcasted_iota(jnp.int32, grouped, 1)

    def conv_step(c, _):
        main, ext = _with_halo(rest_ref, 0, c)
        conv_ref[pl.ds(pl.multiple_of(c * T, T), T), :] = cb_ref[...] + (
            cw_ref[0:1] * _shift_rows(ext, -2) + cw_ref[1:2] * _shift_rows(ext, -1)
            + cw_ref[2:3] * main + cw_ref[3:4] * _shift_rows(ext, 1))
        return 0

    lax.fori_loop(0, N_TILES, conv_step, 0)

    def local_scan(c, d):
        u = conv_ref[pl.ds(pl.multiple_of(c * T, T), T), :]
        g = _dot(u.astype(BF16), gw_ref[:, d * 2 * LRU_W:(d + 1) * 2 * LRU_W])
        g = jax.nn.sigmoid(g + gb_ref[:, d * 2 * LRU_W:(d + 1) * 2 * LRU_W])
        log_a = (-LRU_C * g[:, 0:LRU_W]) * _softplus(-lam_ref[d:d + 1])
        a = jnp.exp(log_a)
        neg_expm1 = (1.0 - a) * (1.0 + a)
        root = jnp.where(neg_expm1 > 0.0, neg_expm1 * lax.rsqrt(neg_expm1), 0.0)
        v = root * (g[:, LRU_W:2 * LRU_W] * u)
        a, v = a.reshape(grouped), v.reshape(grouped)
        for step in (1, 2, 4):
            if d == 0:
                a_s, v_s = pltpu.roll(a, step, 1), pltpu.roll(v, step, 1)
                keep = row_in_group >= step
            else:
                a_s, v_s = pltpu.roll(a, S - step, 1), pltpu.roll(v, S - step, 1)
                keep = row_in_group < S - step
            v = jnp.where(keep, a * v_s + v, v)
            a = jnp.where(keep, a * a_s, a)
        a_ref[d] = a.reshape(T, LRU_W)
        u_ref[d] = v.reshape(T, LRU_W)

    def tile_step(i, carry):
        h_f, h_b = carry
        c_f = i
        c_b = jnp.where(i == 0, 0, N_TILES - i)
        local_scan(c_f, 0)
        local_scan(c_b, 1)
        r_f = pl.multiple_of(c_f * T, T)
        r_b = pl.multiple_of(c_b * T, T)
        for gi in range(GROUPS_PER_TILE):
            lo = gi * S
            a, v = a_ref[0, lo:lo + S], u_ref[0, lo:lo + S]
            hf_ref[pl.ds(r_f + lo, S), :] = v + a * h_f
            h_f = (jnp.broadcast_to(v[S - 1:S], (S, LRU_W))
                   + jnp.broadcast_to(a[S - 1:S], (S, LRU_W)) * h_f)
            lo = (GROUPS_PER_TILE - 1 - gi) * S
            a, v = a_ref[1, lo:lo + S], u_ref[1, lo:lo + S]
            hb_ref[pl.ds(r_b + lo, S), :] = v + a * h_b
            h_b = (jnp.broadcast_to(v[0:1], (S, LRU_W))
                   + jnp.broadcast_to(a[0:1], (S, LRU_W)) * h_b)
        return h_f, h_b

    zeros = jnp.zeros((S, LRU_W), F32)
    lax.fori_loop(0, N_TILES, tile_step, (zeros, zeros))

    lane = lax.broadcasted_iota(jnp.int32, (T, POOL_W), 1)
    half_win = jnp.where(lane < POOL_GW, POOL_WINDOWS[0] // 2,
                         jnp.where(lane < 2 * POOL_GW, POOL_WINDOWS[1] // 2,
                                   jnp.where(lane < 3 * POOL_GW, POOL_WINDOWS[2] // 2,
                                             POOL_WINDOWS[3] // 2)))
    row = lax.broadcasted_iota(jnp.int32, (T, POOL_W), 0)

    def out_step(c, _):
        r0 = pl.multiple_of(c * T, T)
        gate = rest_ref[0, pl.ds(r0, T), LRU_W:2 * LRU_W]
        h = hf_ref[pl.ds(r0, T), :] + hb_ref[pl.ds(r0, T), :]
        rec_ref[0, pl.ds(r0, T), :] = (jax.nn.gelu(gate) * h).astype(BF16)

        px, ext = _with_halo(rest_ref, 2 * LRU_W, c)
        p2 = ext + pltpu.roll(ext, 1, 0)
        p4 = p2 + pltpu.roll(p2, 2, 0)
        p8 = p4 + pltpu.roll(p4, 4, 0)
        p16 = p8 + pltpu.roll(p8, 8, 0)
        win = jnp.where(lane < POOL_GW, _shift_rows(p2, 0),
                        jnp.where(lane < 2 * POOL_GW, _shift_rows(p4, 1),
                                  jnp.where(lane < 3 * POOL_GW, _shift_rows(p8, 3),
                                            _shift_rows(p16, 7))))
        seg_len = jnp.where(c == 0, CTX_LEN, SEQ)
        pos = row + jnp.where(c == 0, 0, r0 - CTX_LEN)
        cnt = jnp.minimum(pos + half_win, seg_len) - jnp.maximum(pos - half_win, 0)
        d = win / cnt.astype(F32) - px
        y = _dot(d.astype(BF16), pw_ref[...]) + pb_ref[...]
        pool_ref[0, pl.ds(r0, T), :] = (y * ps_ref[...]).astype(BF16)
        return 0

    lax.fori_loop(0, N_TILES, out_step, 0)


def _lru_pool(l, rest, conv_w, conv_b, gate_w, gate_b, lam, pool_w, pool_b, pool_scale):
    seq = lambda w: pl.BlockSpec((1, LALL, w), lambda b: (b, 0, 0))
    return pl.pallas_call(
        _lru_pool_kernel,
        grid=(BATCH,),
        in_specs=[
            seq(REST_W),
            _layer_spec(l, (CONV_W, LRU_W)), _layer_spec(l, (1, LRU_W)),
            _layer_spec(l, (LRU_W, 4 * LRU_W)), _layer_spec(l, (1, 4 * LRU_W)), _layer_spec(l, (2, LRU_W)),
            _layer_spec(l, (POOL_W, POOL_W)), _layer_spec(l, (1, POOL_W)), _layer_spec(l, (1, POOL_W)),
        ],
        out_specs=[seq(LRU_W), seq(POOL_W)],
        out_shape=[jax.ShapeDtypeStruct((BATCH, LALL, LRU_W), BF16),
                   jax.ShapeDtypeStruct((BATCH, LALL, POOL_W), BF16)],
        scratch_shapes=[
            pltpu.VMEM((LALL, LRU_W), F32), pltpu.VMEM((LALL, LRU_W), F32),
            pltpu.VMEM((2, TOKEN_TILE, LRU_W), F32), pltpu.VMEM((2, TOKEN_TILE, LRU_W), F32),
            pltpu.VMEM((LALL, LRU_W), F32),
        ],
        compiler_params=pltpu.CompilerParams(
            dimension_semantics=("arbitrary",),
            vmem_limit_bytes=_vmem_limit(56 * 1024 * 1024)),
        name="lru_pool",
    )(rest, conv_w, conv_b, gate_w, gate_b, lam, pool_w, pool_b, pool_scale)


def _out_ffn_kernel(*refs, n_sub):
    T = TOKEN_TILE
    att, rec, pool = refs[0:n_sub], refs[n_sub:2 * n_sub], refs[2 * n_sub:3 * n_sub]
    x_ref, mod_ref, g_ref, wo_ref, wi_ref, wf_ref, o_ref = refs[3 * n_sub:]
    mod = mod_ref[0]
    ys = [_dot(att[k][0], wo_ref[0:ATTN_W])
          + _dot(rec[k][0], wo_ref[ATTN_W:ATTN_W + LRU_W])
          + _dot(pool[k][0], wo_ref[ATTN_W + LRU_W:MIX_W]) for k in range(n_sub)]
    for k in range(n_sub):
        rows = slice(k * T, (k + 1) * T)
        x = x_ref[0, rows] + mod[2:3] * (_rms(ys[k]) * g_ref[1:2])
        h = ((_rms(x) * g_ref[2:3]) * (1.0 + mod[4:5]) + mod[3:4]).astype(BF16)
        gate_up = lambda lo, n: (_dot(h, wi_ref[:, lo:lo + n]),
                                 _dot(h, wi_ref[:, FFN_HIDDEN + lo:FFN_HIDDEN + lo + n]))
        gu = gate_up(*FFN_CHUNKS[0])
        f = None
        for i, (lo, n) in enumerate(FFN_CHUNKS):
            gu_next = gate_up(*FFN_CHUNKS[i + 1]) if i + 1 < len(FFN_CHUNKS) else None
            g, u = gu
            part = _dot((g * jax.nn.sigmoid(g) * u).astype(BF16), wf_ref[lo:lo + n])
            f = part if f is None else f + part
            gu = gu_next
        o_ref[0, rows] = x + mod[5:6] * (_rms(f) * g_ref[3:4])


def _out_ffn(l, att, rec, pool, x_src, mod, norm_g, w_out, w_ffn_in, w_ffn_out, *, context, att_has_ctx):
    T = TOKEN_TILE
    n_sub = 1 if context else FFN_TILES_PER_STEP
    n_steps = 1 if context else (N_TILES - 1) // n_sub
    tile = lambda k, skip: (lambda b, t: (b, 0 if context else n_sub * t + k + skip, 0))
    weight = lambda shape: _layer_spec(l, shape, pipeline_mode=pl.Buffered(1))
    subs = range(n_sub)
    return pl.pallas_call(
        functools.partial(_out_ffn_kernel, n_sub=n_sub),
        grid=(BATCH, n_steps),
        in_specs=[
            *[pl.BlockSpec((1, T, ATTN_W), tile(k, 1 if att_has_ctx else 0)) for k in subs],
            *[pl.BlockSpec((1, T, LRU_W), tile(k, 1)) for k in subs],
            *[pl.BlockSpec((1, T, POOL_W), tile(k, 1)) for k in subs],
            pl.BlockSpec((1, n_sub * T, D_MODEL), lambda b, t: (b, t, 0)),
            pl.BlockSpec((None, 1, 6, D_MODEL), lambda b, t: (l, BATCH if context else b, 0, 0)),
            weight((4, D_MODEL)),
            weight((MIX_W, D_MODEL)),
            weight((D_MODEL, 2 * FFN_HIDDEN)),
            weight((FFN_HIDDEN, D_MODEL)),
        ],
        out_specs=pl.BlockSpec((1, n_sub * T, D_MODEL), lambda b, t: (b, t, 0)),
        out_shape=jax.ShapeDtypeStruct((BATCH, n_steps * n_sub * T, D_MODEL), F32),
        compiler_params=pltpu.CompilerParams(
            dimension_semantics=("arbitrary", "arbitrary"),
            vmem_limit_bytes=_vmem_limit(52 * 1024 * 1024)),
        name="out_ffn_ctx" if context else "out_ffn",
    )(*([att] * n_sub), *([rec] * n_sub), *([pool] * n_sub), x_src, mod, norm_g,
      w_out, w_ffn_in, w_ffn_out)


def _position_tables():
    half = ROPE_HALF
    freq = (ROPE_THETA ** (-np.arange(half, dtype=np.float32) / half)).astype(np.float32)
    p = np.arange(SEQ)
    ang_row = (p // GRID_W).astype(np.float32)[:, None] * freq
    ang_col = (p % GRID_W).astype(np.float32)[:, None] * freq
    ang = np.concatenate([ang_row, ang_row, ang_col, ang_col], axis=1)
    first = np.tile(np.arange(HEAD_DIM) % (2 * half) < half, (SEQ, 1))
    cos = np.cos(ang)
    sin_a = np.where(first, -np.sin(ang), 0.0)
    sin_b = np.where(first, 0.0, np.sin(ang))
    pad = lambda a, v: np.concatenate([np.full((CTX_LEN, HEAD_DIM), v), a], axis=0)
    two = lambda a: np.tile(a, (1, V7X_LANES // HEAD_DIM)).astype(np.float32)
    return two(pad(cos, 1.0)), two(pad(sin_a, 0.0)), two(pad(sin_b, 0.0))


def _head_mean_matrix(width):
    blk = np.kron(np.eye(width // HEAD_DIM), np.full((HEAD_DIM, HEAD_DIM), 1.0 / HEAD_DIM))
    return jnp.asarray(blk, dtype=BF16)


def _block_diag(w):
    n, c, d = w.shape[-3:]
    on_diag = jnp.eye(n, dtype=bool)[:, None, :, None]
    out = jnp.where(on_diag, w[..., :, :, None, :], 0.0)
    return out.reshape(w.shape[:-3] + (n * c, n * d))


def kernel(x, c, ctx, c_ctx, w_mod, b_mod, norm_g, w_in, q_norm_g, k_norm_g, lru_conv_w, lru_conv_b,
           lru_gate_w, lru_gate_b, lru_lambda, pool_w, pool_b, pool_scale, w_out, w_ffn_in, w_ffn_out):
    assert x.shape == (BATCH, SEQ, D_MODEL) and ctx.shape == (BATCH, CTX_LEN, D_MODEL)
    cos, sin_a, sin_b = (jnp.asarray(a) for a in _position_tables())
    consts = (_head_mean_matrix(ATTN_W), _head_mean_matrix(KV_W), cos, sin_a, sin_b)

    c_all = jnp.zeros((MOD_ROWS, D_MODEL), F32).at[0:BATCH].set(c).at[BATCH].set(c_ctx)
    mod = _modulation(c_all, w_mod, b_mod).reshape(DEPTH, MOD_ROWS, 6, D_MODEL)

    w_in_b, w_out_b = w_in.astype(BF16), w_out.astype(BF16)
    w_ffn_in_b, w_ffn_out_b = w_ffn_in.astype(BF16), w_ffn_out.astype(BF16)
    gq = jnp.tile(q_norm_g, (1, N_Q_HEADS))[:, None, :]
    gk = jnp.tile(k_norm_g, (1, N_KV_HEADS))[:, None, :]
    gate_w = _block_diag(lru_gate_w)
    gate_w = gate_w.transpose(0, 3, 1, 2, 4).reshape(DEPTH, LRU_W, 4 * LRU_W).astype(BF16)
    gate_b = lru_gate_b.reshape(DEPTH, 1, 4 * LRU_W)
    pool_w_b = _block_diag(pool_w).astype(BF16)
    conv_b, pool_b3, pool_s3 = lru_conv_b[:, None, :], pool_b[:, None, :], pool_scale[:, None, :]

    for l in range(DEPTH):
        last = l == DEPTH - 1
        q, k, vt, rest = _in_projection(l, ctx, x, mod, norm_g, w_in_b, consts, gq, gk)
        att = _attention(q, k, vt, 1 if last else 0)
        rec, pool = _lru_pool(l, rest, lru_conv_w, conv_b, gate_w, gate_b, lru_lambda,
                              pool_w_b, pool_b3, pool_s3)
        weights = (mod, norm_g, w_out_b, w_ffn_in_b, w_ffn_out_b)
        if not last:
            ctx_next = _out_ffn(l, att, rec, pool, ctx, *weights, context=True, att_has_ctx=True)
        x = _out_ffn(l, att, rec, pool, x, *weights, context=False, att_has_ctx=not last)
        if not last:
            ctx = ctx_next
    return x
```

```python
import functools

import numpy as np
import jax
import jax.numpy as jnp
from jax import lax
from jax.experimental import pallas as pl
from jax.experimental.pallas import tpu as pltpu

D_MODEL = 1024
BATCH = 4
SEQ = 4096
DEPTH = 2
GRID_W = 64
CTX_LEN = 256
LALL = CTX_LEN + SEQ

N_Q_HEADS = 8
N_KV_HEADS = 2
HEAD_DIM = 64
Q_GROUP = N_Q_HEADS // N_KV_HEADS
ATTN_W = N_Q_HEADS * HEAD_DIM
KV_W = N_KV_HEADS * HEAD_DIM
ROPE_THETA = 10000.0
LRU_W = D_MODEL // 4
LRU_BLOCKS = 4
LRU_BW = LRU_W // LRU_BLOCKS
CONV_W = 4
LRU_C = 8.0
POOL_W = D_MODEL // 4
POOL_GROUPS = 4
POOL_GW = POOL_W // POOL_GROUPS
POOL_WINDOWS = (2, 4, 8, 16)
MIX_W = ATTN_W + LRU_W + POOL_W
IN_W = ATTN_W + 2 * KV_W + 2 * LRU_W + POOL_W
REST_W = 2 * LRU_W + POOL_W
FFN_HIDDEN = -(-8 * D_MODEL // 768) * 256
RMS_EPS = 1e-6

V7X_LANES = 128
V7X_SUBLANES = 8
V7X_VMEM_BYTES = 64 * 1024 * 1024

TOKEN_TILE = CTX_LEN
N_TILES = LALL // TOKEN_TILE
MOD_ROWS = 8
MOD_TILE_N = 1536
ROPE_HALF = HEAD_DIM // 4
GROUPS_PER_TILE = TOKEN_TILE // V7X_SUBLANES
ONES_ROWS = 16
ATTN_Q_TILES = 1
KEY_CHUNK = TOKEN_TILE
SCORE_SLOTS = 3
FFN_TILES_PER_STEP = 2
FFN_CHUNK = 512
FFN_CHUNKS = [(lo, min(FFN_CHUNK, FFN_HIDDEN - lo)) for lo in range(0, FFN_HIDDEN, FFN_CHUNK)]
Q_SCALE = HEAD_DIM ** -0.5 * float(np.log2(np.e))

F32 = jnp.float32
BF16 = jnp.bfloat16


def _vmem_limit(nbytes):
    return int(min(max(nbytes, 16 * 1024 * 1024), V7X_VMEM_BYTES - 6 * 1024 * 1024))


def _rms(x):
    return x * lax.rsqrt(jnp.mean(x * x, axis=-1, keepdims=True) + RMS_EPS)


def _dot(a, b):
    return jnp.dot(a, b, preferred_element_type=F32)


def _layer_spec(l, shape, **kw):
    return pl.BlockSpec((None,) + shape, lambda *_: (l,) + (0,) * len(shape), **kw)


def _mod_row(b, t):
    return jnp.where(t == 0, BATCH, b)


def _token_specs(width):
    T = TOKEN_TILE
    ctx_spec = pl.BlockSpec((1, T, width), lambda b, t: (b, 0, 0))
    x_spec = pl.BlockSpec((1, T, width), lambda b, t: (b, jnp.maximum(t - 1, 0), 0))
    return ctx_spec, x_spec


def _mod_kernel(c_ref, w_ref, b_ref, o_ref):
    c = c_ref[...]
    h = (c * jax.nn.sigmoid(c)).astype(BF16)
    o_ref[0] = _dot(h, w_ref[0].astype(BF16)) + b_ref[0]


def _modulation(c_all, w_mod, b_mod):
    n = 6 * D_MODEL
    return pl.pallas_call(
        _mod_kernel,
        grid=(DEPTH, n // MOD_TILE_N),
        in_specs=[
            pl.BlockSpec((MOD_ROWS, D_MODEL), lambda l, j: (0, 0)),
            pl.BlockSpec((1, D_MODEL, MOD_TILE_N), lambda l, j: (l, 0, j)),
            pl.BlockSpec((1, 1, MOD_TILE_N), lambda l, j: (l, 0, j)),
        ],
        out_specs=pl.BlockSpec((1, MOD_ROWS, MOD_TILE_N), lambda l, j: (l, 0, j)),
        out_shape=jax.ShapeDtypeStruct((DEPTH, MOD_ROWS, n), F32),
        compiler_params=pltpu.CompilerParams(
            dimension_semantics=("arbitrary", "arbitrary"),
            vmem_limit_bytes=_vmem_limit(4 * D_MODEL * MOD_TILE_N * 4)),
        name="modulation",
    )(c_all, w_mod, b_mod.reshape(DEPTH, 1, n))


def _head_norm_rope(z, s_ref, g, cos, sin_a, sin_b):
    z2 = z * z
    hi = z2.astype(BF16)
    lo = (z2 - hi.astype(F32)).astype(BF16)
    ms = _dot(hi, s_ref[...]) + _dot(lo, s_ref[...])
    zn = (z * lax.rsqrt(ms + RMS_EPS)) * g
    outs = []
    for c in range(z.shape[1] // V7X_LANES):
        zc = zn[:, c * V7X_LANES:(c + 1) * V7X_LANES]
        up = pltpu.roll(zc, V7X_LANES - ROPE_HALF, 1)
        dn = pltpu.roll(zc, ROPE_HALF, 1)
        outs.append(zc * cos + up * sin_a + dn * sin_b)
    return outs


def _head_norm_rope_t(zt, g, cos_t, sin_t):
    zn = (zt * lax.rsqrt(jnp.mean(zt * zt, axis=0, keepdims=True) + RMS_EPS)) * g
    h = ROPE_HALF
    partner = jnp.concatenate([zn[h:2 * h], zn[0:h], zn[3 * h:4 * h], zn[2 * h:3 * h]], axis=0)
    return zn * cos_t + partner * sin_t


def _inproj_kernel(ctx_ref, x_ref, mod_ref, g_ref, w_ref, sk_ref, gq_ref, gk_ref,
                   cos_ref, sa_ref, sb_ref, cost_ref, sint_ref, q_ref, k_ref, vt_ref, rest_ref):
    x = jnp.where(pl.program_id(1) == 0, ctx_ref[0], x_ref[0])
    mod = mod_ref[0]
    h = ((_rms(x) * g_ref[0:1]) * (1.0 + mod[1:2]) + mod[0:1]).astype(BF16)
    qkv_w = ATTN_W + 2 * KV_W
    first_w = qkv_w + LRU_W
    y = _dot(h, w_ref[:, 0:first_w])
    (kc,) = _head_norm_rope(y[:, ATTN_W:ATTN_W + KV_W], sk_ref, gk_ref[...],
                            cos_ref[...], sa_ref[...], sb_ref[...])
    rest_ref[0, :, 0:LRU_W] = y[:, qkv_w:first_w]
    rest_ref[0, :, LRU_W:REST_W] = _dot(h, w_ref[:, first_w:IN_W])

    for c in range(ATTN_W // V7X_LANES):
        zt = y[:, c * V7X_LANES:(c + 1) * V7X_LANES].T
        for i in range(V7X_LANES // HEAD_DIM):
            qt = _head_norm_rope_t(zt[i * HEAD_DIM:(i + 1) * HEAD_DIM], gq_ref[...],
                                   cost_ref[...], sint_ref[...])
            q_ref[0, 2 * c + i, 0] = (qt * Q_SCALE).astype(BF16)

    kc = kc.astype(BF16)
    k_ref[0, 0] = kc[:, 0:HEAD_DIM]
    k_ref[0, 1] = kc[:, HEAD_DIM:2 * HEAD_DIM]

    vt = y[:, ATTN_W + KV_W:qkv_w].T.astype(BF16)
    vt_ref[0, 0, 0] = vt[0:HEAD_DIM]
    vt_ref[0, 1, 0] = vt[HEAD_DIM:2 * HEAD_DIM]


def _in_projection(l, ctx_src, x_src, mod, norm_g, w_in, consts, gq, gk):
    sk, cos, sin_a, sin_b, cos_t, sin_t = consts
    T = TOKEN_TILE
    full = lambda shape: pl.BlockSpec(shape, lambda b, t: (0,) * len(shape))
    tab = pl.BlockSpec((T, V7X_LANES), lambda b, t: (t, 0))
    return pl.pallas_call(
        _inproj_kernel,
        grid=(BATCH, N_TILES),
        in_specs=[
            *_token_specs(D_MODEL),
            pl.BlockSpec((None, 1, 6, D_MODEL), lambda b, t: (l, _mod_row(b, t), 0, 0)),
            _layer_spec(l, (4, D_MODEL)),
            _layer_spec(l, (D_MODEL, IN_W)),
            full((KV_W, KV_W)),
            _layer_spec(l, (HEAD_DIM, T)),
            _layer_spec(l, (1, KV_W)),
            tab, tab, tab,
            pl.BlockSpec((HEAD_DIM, T), lambda b, t: (0, t)),
            pl.BlockSpec((HEAD_DIM, T), lambda b, t: (0, t)),
        ],
        out_specs=[
            pl.BlockSpec((1, N_Q_HEADS, 1, HEAD_DIM, T), lambda b, t: (b, 0, t, 0, 0)),
            pl.BlockSpec((1, N_KV_HEADS, T, HEAD_DIM), lambda b, t: (b, 0, t, 0)),
            pl.BlockSpec((1, N_KV_HEADS, 1, HEAD_DIM, T), lambda b, t: (b, 0, t, 0, 0)),
            pl.BlockSpec((1, T, REST_W), lambda b, t: (b, t, 0)),
        ],
        out_shape=[
            jax.ShapeDtypeStruct((BATCH, N_Q_HEADS, N_TILES, HEAD_DIM, T), BF16),
            jax.ShapeDtypeStruct((BATCH, N_KV_HEADS, LALL, HEAD_DIM), BF16),
            jax.ShapeDtypeStruct((BATCH, N_KV_HEADS, N_TILES, HEAD_DIM, T), BF16),
            jax.ShapeDtypeStruct((BATCH, LALL, REST_W), F32),
        ],
        compiler_params=pltpu.CompilerParams(
            dimension_semantics=("arbitrary", "arbitrary"),
            vmem_limit_bytes=_vmem_limit(40 * 1024 * 1024)),
        name="in_projection",
    )(ctx_src, x_src, mod, norm_g, w_in, sk, gq, gk, cos, sin_a, sin_b, cos_t, sin_t)


def _attn_kernel(*refs, q_tiles, n_chunks):
    T = TOKEN_TILE
    q_refs = refs[:q_tiles]
    k_ref, vt_ref, o_ref, s_ref, m_ref, acc_ref = refs[q_tiles:]
    n_groups = q_tiles * Q_GROUP
    qt = jnp.concatenate([q_refs[i][0, g, 0] for i in range(q_tiles) for g in range(Q_GROUP)],
                         axis=1)
    pieces = n_groups // 2

    def scores(c, piece):
        cols = slice(2 * piece * T, 2 * (piece + 1) * T)
        s_ref[c % SCORE_SLOTS, :, cols] = _dot(k_ref[0, 0, c * KEY_CHUNK:(c + 1) * KEY_CHUNK, :], qt[:, cols])

    def update(c, piece):
        tile, off = divmod(c * KEY_CHUNK, T)
        vt1 = jnp.concatenate([vt_ref[0, 0, tile][:, off:off + KEY_CHUNK],
                               jnp.ones((ONES_ROWS, KEY_CHUNK), BF16)], axis=0)
        for g in (2 * piece, 2 * piece + 1):
            cols = slice(g * T, (g + 1) * T)
            s = s_ref[c % SCORE_SLOTS, :, cols]
            m = m_ref[:, cols]
            m_new = jnp.maximum(m, jnp.max(s, axis=0, keepdims=True))
            alpha = jnp.exp2(m - m_new)
            p = jnp.exp2((s - m_new).astype(BF16))
            m_ref[:, cols] = m_new
            acc_ref[:, cols] = alpha * acc_ref[:, cols] + _dot(vt1, p)

    m_ref[...] = jnp.full(m_ref.shape, -1e30, F32)
    acc_ref[...] = jnp.zeros(acc_ref.shape, F32)
    for c in range(min(SCORE_SLOTS - 1, n_chunks)):
        for piece in range(pieces):
            scores(c, piece)
    @pl.when(pl.program_id(2) >= 0)
    def _():
        for c in range(n_chunks):
            for piece in range(pieces):
                if c + SCORE_SLOTS - 1 < n_chunks:
                    scores(c + SCORE_SLOTS - 1, piece)
                update(c, piece)
    for i in range(q_tiles):
        cols = slice(i * Q_GROUP * T, (i + 1) * Q_GROUP * T)
        o = acc_ref[0:HEAD_DIM, cols] / acc_ref[HEAD_DIM:HEAD_DIM + 1, cols]
        o = jnp.concatenate([o[:, g * T:(g + 1) * T] for g in range(Q_GROUP)], axis=0)
        o_ref[0, i * T:(i + 1) * T, :] = o.T.astype(BF16)


def _attention(q, k, vt, *, context):
    T = TOKEN_TILE
    q_tiles = 1 if context else ATTN_Q_TILES
    n_steps = 1 if context else (N_TILES - 1) // q_tiles
    n_cols = q_tiles * Q_GROUP * T
    q_spec = lambda i: pl.BlockSpec(
        (1, Q_GROUP, 1, HEAD_DIM, T), lambda b, h, t: (b, h, 0 if context else q_tiles * t + i + 1, 0, 0))
    return pl.pallas_call(
        functools.partial(_attn_kernel, q_tiles=q_tiles, n_chunks=(CTX_LEN if context else LALL) // KEY_CHUNK),
        grid=(BATCH, N_KV_HEADS, n_steps),
        in_specs=[
            *[q_spec(i) for i in range(q_tiles)],
            pl.BlockSpec((1, 1, LALL, HEAD_DIM), lambda b, h, t: (b, h, 0, 0)),
            pl.BlockSpec((1, 1, N_TILES, HEAD_DIM, T), lambda b, h, t: (b, h, 0, 0, 0)),
        ],
        out_specs=pl.BlockSpec((1, q_tiles * T, Q_GROUP * HEAD_DIM), lambda b, h, t: (b, t, h)),
        out_shape=jax.ShapeDtypeStruct((BATCH, n_steps * q_tiles * T, ATTN_W), BF16),
        scratch_shapes=[
            pltpu.VMEM((SCORE_SLOTS, KEY_CHUNK, n_cols), F32),
            pltpu.VMEM((1, n_cols), F32),
            pltpu.VMEM((HEAD_DIM + ONES_ROWS, n_cols), F32),
        ],
        compiler_params=pltpu.CompilerParams(
            dimension_semantics=("arbitrary", "arbitrary", "arbitrary"),
            vmem_limit_bytes=_vmem_limit(40 * 1024 * 1024)),
        name="attention_ctx" if context else "attention",
    )(*([q] * q_tiles), k, vt)


def _softplus(z):
    return jnp.maximum(z, 0.0) + jnp.log1p(jnp.exp(-jnp.abs(z)))


def _with_halo(ref, col, c):
    T = TOKEN_TILE
    r0 = pl.multiple_of(c * T, T)
    main = ref[0, pl.ds(r0, T), col:col + LRU_W]
    p0 = pl.multiple_of(jnp.maximum(r0 - V7X_SUBLANES, 0), V7X_SUBLANES)
    n0 = pl.multiple_of(jnp.minimum(r0 + T, LALL - V7X_SUBLANES), V7X_SUBLANES)
    prev = ref[0, pl.ds(p0, V7X_SUBLANES), col:col + LRU_W]
    nxt = ref[0, pl.ds(n0, V7X_SUBLANES), col:col + LRU_W]
    prev = jnp.where(c >= 2, prev, 0.0)
    nxt = jnp.where((c >= 1) & (c <= N_TILES - 2), nxt, 0.0)
    return main, jnp.concatenate([prev, main, nxt], axis=0)


def _shift_rows(ext, k):
    n = ext.shape[0]
    if k == 0:
        return ext[V7X_SUBLANES:V7X_SUBLANES + TOKEN_TILE]
    return pltpu.roll(ext, (-k) % n, 0)[V7X_SUBLANES:V7X_SUBLANES + TOKEN_TILE]


def _lru_pool_kernel(rest_ref, cw_ref, cb_ref, gw_ref, gb_ref, lam_ref, pw_ref, pb_ref, ps_ref,
                     rec_ref, pool_ref, hf_ref, hb_ref, a_ref, u_ref, conv_ref):
    T = TOKEN_TILE
    S = V7X_SUBLANES
    grouped = (GROUPS_PER_TILE, S, LRU_W)
    row_in_group = lax.broadcasted_iota(jnp.int32, grouped, 1)

    def conv_step(c, _):
        main, ext = _with_halo(rest_ref, 0, c)
        conv_ref[pl.ds(pl.multiple_of(c * T, T), T), :] = cb_ref[...] + (
            cw_ref[0:1] * _shift_rows(ext, -2) + cw_ref[1:2] * _shift_rows(ext, -1)
            + cw_ref[2:3] * main + cw_ref[3:4] * _shift_rows(ext, 1))
        return 0

    lax.fori_loop(0, N_TILES, conv_step, 0)

    def local_scan(c, d):
        u = conv_ref[pl.ds(pl.multiple_of(c * T, T), T), :]
        g = _dot(u.astype(BF16), gw_ref[:, d * 2 * LRU_W:(d + 1) * 2 * LRU_W])
        g = jax.nn.sigmoid(g + gb_ref[:, d * 2 * LRU_W:(d + 1) * 2 * LRU_W])
        log_a = (-LRU_C * g[:, 0:LRU_W]) * _softplus(-lam_ref[d:d + 1])
        a = jnp.exp(log_a)
        neg_expm1 = (1.0 - a) * (1.0 + a)
        root = jnp.where(neg_expm1 > 0.0, neg_expm1 * lax.rsqrt(neg_expm1), 0.0)
        v = root * (g[:, LRU_W:2 * LRU_W] * u)
        a, v = a.reshape(grouped), v.reshape(grouped)
        for step in (1, 2, 4):
            if d == 0:
                a_s, v_s = pltpu.roll(a, step, 1), pltpu.roll(v, step, 1)
                keep = row_in_group >= step
            else:
                a_s, v_s = pltpu.roll(a, S - step, 1), pltpu.roll(v, S - step, 1)
                keep = row_in_group < S - step
            v = jnp.where(keep, a * v_s + v, v)
            a = jnp.where(keep, a * a_s, a)
        a_ref[d] = a.reshape(T, LRU_W)
        u_ref[d] = v.reshape(T, LRU_W)

    def tile_step(i, carry):
        h_f, h_b = carry
        c_f = i
        c_b = jnp.where(i == 0, 0, N_TILES - i)
        local_scan(c_f, 0)
        local_scan(c_b, 1)
        r_f = pl.multiple_of(c_f * T, T)
        r_b = pl.multiple_of(c_b * T, T)
        for gi in range(GROUPS_PER_TILE):
            lo = gi * S
            a, v = a_ref[0, lo:lo + S], u_ref[0, lo:lo + S]
            hf_ref[pl.ds(r_f + lo, S), :] = v + a * h_f
            h_f = (jnp.broadcast_to(v[S - 1:S], (S, LRU_W))
                   + jnp.broadcast_to(a[S - 1:S], (S, LRU_W)) * h_f)
            lo = (GROUPS_PER_TILE - 1 - gi) * S
            a, v = a_ref[1, lo:lo + S], u_ref[1, lo:lo + S]
            hb_ref[pl.ds(r_b + lo, S), :] = v + a * h_b
            h_b = (jnp.broadcast_to(v[0:1], (S, LRU_W))
                   + jnp.broadcast_to(a[0:1], (S, LRU_W)) * h_b)
        return h_f, h_b

    zeros = jnp.zeros((S, LRU_W), F32)
    lax.fori_loop(0, N_TILES, tile_step, (zeros, zeros))

    lane = lax.broadcasted_iota(jnp.int32, (T, POOL_W), 1)
    half_win = jnp.where(lane < POOL_GW, POOL_WINDOWS[0] // 2,
                         jnp.where(lane < 2 * POOL_GW, POOL_WINDOWS[1] // 2,
                                   jnp.where(lane < 3 * POOL_GW, POOL_WINDOWS[2] // 2,
                                             POOL_WINDOWS[3] // 2)))
    row = lax.broadcasted_iota(jnp.int32, (T, POOL_W), 0)

    def out_step(c, _):
        r0 = pl.multiple_of(c * T, T)
        gate = rest_ref[0, pl.ds(r0, T), LRU_W:2 * LRU_W]
        h = hf_ref[pl.ds(r0, T), :] + hb_ref[pl.ds(r0, T), :]
        rec_ref[0, pl.ds(r0, T), :] = (jax.nn.gelu(gate) * h).astype(BF16)

        px, ext = _with_halo(rest_ref, 2 * LRU_W, c)
        p2 = ext + pltpu.roll(ext, 1, 0)
        p4 = p2 + pltpu.roll(p2, 2, 0)
        p8 = p4 + pltpu.roll(p4, 4, 0)
        p16 = p8 + pltpu.roll(p8, 8, 0)
        win = jnp.where(lane < POOL_GW, _shift_rows(p2, 0),
                        jnp.where(lane < 2 * POOL_GW, _shift_rows(p4, 1),
                                  jnp.where(lane < 3 * POOL_GW, _shift_rows(p8, 3),
                                            _shift_rows(p16, 7))))
        seg_len = jnp.where(c == 0, CTX_LEN, SEQ)
        pos = row + jnp.where(c == 0, 0, r0 - CTX_LEN)
        cnt = jnp.minimum(pos + half_win, seg_len) - jnp.maximum(pos - half_win, 0)
        d = win / cnt.astype(F32) - px
        y = _dot(d.astype(BF16), pw_ref[...]) + pb_ref[...]
        pool_ref[0, pl.ds(r0, T), :] = (y * ps_ref[...]).astype(BF16)
        return 0

    lax.fori_loop(0, N_TILES, out_step, 0)


def _lru_pool(l, rest, conv_w, conv_b, gate_w, gate_b, lam, pool_w, pool_b, pool_scale):
    seq = lambda w: pl.BlockSpec((1, LALL, w), lambda b: (b, 0, 0))
    return pl.pallas_call(
        _lru_pool_kernel,
        grid=(BATCH,),
        in_specs=[
            seq(REST_W),
            _layer_spec(l, (CONV_W, LRU_W)), _layer_spec(l, (1, LRU_W)),
            _layer_spec(l, (LRU_W, 4 * LRU_W)), _layer_spec(l, (1, 4 * LRU_W)), _layer_spec(l, (2, LRU_W)),
            _layer_spec(l, (POOL_W, POOL_W)), _layer_spec(l, (1, POOL_W)), _layer_spec(l, (1, POOL_W)),
        ],
        out_specs=[seq(LRU_W), seq(POOL_W)],
        out_shape=[jax.ShapeDtypeStruct((BATCH, LALL, LRU_W), BF16),
                   jax.ShapeDtypeStruct((BATCH, LALL, POOL_W), BF16)],
        scratch_shapes=[
            pltpu.VMEM((LALL, LRU_W), F32), pltpu.VMEM((LALL, LRU_W), F32),
            pltpu.VMEM((2, TOKEN_TILE, LRU_W), F32), pltpu.VMEM((2, TOKEN_TILE, LRU_W), F32),
            pltpu.VMEM((LALL, LRU_W), F32),
        ],
        compiler_params=pltpu.CompilerParams(
            dimension_semantics=("arbitrary",),
            vmem_limit_bytes=_vmem_limit(56 * 1024 * 1024)),
        name="lru_pool",
    )(rest, conv_w, conv_b, gate_w, gate_b, lam, pool_w, pool_b, pool_scale)


def _out_ffn_kernel(*refs, n_sub):
    T = TOKEN_TILE
    att, rec, pool = refs[0:n_sub], refs[n_sub:2 * n_sub], refs[2 * n_sub:3 * n_sub]
    x_ref, mod_ref, g_ref, wo_ref, wi_ref, wf_ref, o_ref = refs[3 * n_sub:]
    mod = mod_ref[0]
    ys = [_dot(att[k][0], wo_ref[0:ATTN_W])
          + _dot(rec[k][0], wo_ref[ATTN_W:ATTN_W + LRU_W])
          + _dot(pool[k][0], wo_ref[ATTN_W + LRU_W:MIX_W]) for k in range(n_sub)]
    for k in range(n_sub):
        rows = slice(k * T, (k + 1) * T)
        x = x_ref[0, rows] + mod[2:3] * (_rms(ys[k]) * g_ref[1:2])
        h = ((_rms(x) * g_ref[2:3]) * (1.0 + mod[4:5]) + mod[3:4]).astype(BF16)
        gate_up = lambda lo, n: (_dot(h, wi_ref[:, lo:lo + n]),
                                 _dot(h, wi_ref[:, FFN_HIDDEN + lo:FFN_HIDDEN + lo + n]))
        gu = gate_up(*FFN_CHUNKS[0])
        f = None
        for i, (lo, n) in enumerate(FFN_CHUNKS):
            gu_next = gate_up(*FFN_CHUNKS[i + 1]) if i + 1 < len(FFN_CHUNKS) else None
            g, u = gu
            part = _dot((g * jax.nn.sigmoid(g) * u).astype(BF16), wf_ref[lo:lo + n])
            f = part if f is None else f + part
            gu = gu_next
        o_ref[0, rows] = x + mod[5:6] * (_rms(f) * g_ref[3:4])


def _out_ffn(l, att, rec, pool, x_src, mod, norm_g, w_out, w_ffn_in, w_ffn_out, *, context):
    T = TOKEN_TILE
    n_sub = 1 if context else FFN_TILES_PER_STEP
    n_steps = 1 if context else (N_TILES - 1) // n_sub
    tile = lambda k, skip: (lambda b, t: (b, 0 if context else n_sub * t + k + skip, 0))
    weight = lambda shape: _layer_spec(l, shape, pipeline_mode=pl.Buffered(1))
    subs = range(n_sub)
    return pl.pallas_call(
        functools.partial(_out_ffn_kernel, n_sub=n_sub),
        grid=(BATCH, n_steps),
        in_specs=[
            *[pl.BlockSpec((1, T, ATTN_W), tile(k, 0)) for k in subs],
            *[pl.BlockSpec((1, T, LRU_W), tile(k, 1)) for k in subs],
            *[pl.BlockSpec((1, T, POOL_W), tile(k, 1)) for k in subs],
            pl.BlockSpec((1, n_sub * T, D_MODEL), lambda b, t: (b, t, 0)),
            pl.BlockSpec((None, 1, 6, D_MODEL), lambda b, t: (l, BATCH if context else b, 0, 0)),
            weight((4, D_MODEL)),
            weight((MIX_W, D_MODEL)),
            weight((D_MODEL, 2 * FFN_HIDDEN)),
            weight((FFN_HIDDEN, D_MODEL)),
        ],
        out_specs=pl.BlockSpec((1, n_sub * T, D_MODEL), lambda b, t: (b, t, 0)),
        out_shape=jax.ShapeDtypeStruct((BATCH, n_steps * n_sub * T, D_MODEL), F32),
        compiler_params=pltpu.CompilerParams(
            dimension_semantics=("arbitrary", "arbitrary"),
            vmem_limit_bytes=_vmem_limit(52 * 1024 * 1024)),
        name="out_ffn_ctx" if context else "out_ffn",
    )(*([att] * n_sub), *([rec] * n_sub), *([pool] * n_sub), x_src, mod, norm_g,
      w_out, w_ffn_in, w_ffn_out)


def _position_tables():
    half = ROPE_HALF
    freq = (ROPE_THETA ** (-np.arange(half, dtype=np.float32) / half)).astype(np.float32)
    p = np.arange(SEQ)
    ang_row = (p // GRID_W).astype(np.float32)[:, None] * freq
    ang_col = (p % GRID_W).astype(np.float32)[:, None] * freq
    ang = np.concatenate([ang_row, ang_row, ang_col, ang_col], axis=1)
    first = np.tile(np.arange(HEAD_DIM) % (2 * half) < half, (SEQ, 1))
    cos = np.cos(ang)
    sin_a = np.where(first, -np.sin(ang), 0.0)
    sin_b = np.where(first, 0.0, np.sin(ang))
    pad = lambda a, v: np.concatenate([np.full((CTX_LEN, HEAD_DIM), v), a], axis=0)
    two = lambda a: np.tile(a, (1, V7X_LANES // HEAD_DIM)).astype(np.float32)
    return two(pad(cos, 1.0)), two(pad(sin_a, 0.0)), two(pad(sin_b, 0.0))


def _head_mean_matrix(width):
    blk = np.kron(np.eye(width // HEAD_DIM), np.full((HEAD_DIM, HEAD_DIM), 1.0 / HEAD_DIM))
    return jnp.asarray(blk, dtype=BF16)


def _block_diag(w):
    n, c, d = w.shape[-3:]
    on_diag = jnp.eye(n, dtype=bool)[:, None, :, None]
    out = jnp.where(on_diag, w[..., :, :, None, :], 0.0)
    return out.reshape(w.shape[:-3] + (n * c, n * d))


def kernel(x, c, ctx, c_ctx, w_mod, b_mod, norm_g, w_in, q_norm_g, k_norm_g, lru_conv_w, lru_conv_b,
           lru_gate_w, lru_gate_b, lru_lambda, pool_w, pool_b, pool_scale, w_out, w_ffn_in, w_ffn_out):
    assert x.shape == (BATCH, SEQ, D_MODEL) and ctx.shape == (BATCH, CTX_LEN, D_MODEL)
    cos, sin_a, sin_b = (jnp.asarray(a) for a in _position_tables())
    cos_t, sin_t = cos[:, 0:HEAD_DIM].T, (sin_a + sin_b)[:, 0:HEAD_DIM].T
    consts = (_head_mean_matrix(KV_W), cos, sin_a, sin_b, cos_t, sin_t)

    c_all = jnp.zeros((MOD_ROWS, D_MODEL), F32).at[0:BATCH].set(c).at[BATCH].set(c_ctx)
    mod = _modulation(c_all, w_mod, b_mod).reshape(DEPTH, MOD_ROWS, 6, D_MODEL)

    w_in_b, w_out_b = w_in.astype(BF16), w_out.astype(BF16)
    w_ffn_in_b, w_ffn_out_b = w_ffn_in.astype(BF16), w_ffn_out.astype(BF16)
    gq = jnp.broadcast_to(q_norm_g[:, :, None], (DEPTH, HEAD_DIM, TOKEN_TILE))
    gk = jnp.tile(k_norm_g, (1, N_KV_HEADS))[:, None, :]
    gate_w = _block_diag(lru_gate_w)
    gate_w = gate_w.transpose(0, 3, 1, 2, 4).reshape(DEPTH, LRU_W, 4 * LRU_W).astype(BF16)
    gate_b = lru_gate_b.reshape(DEPTH, 1, 4 * LRU_W)
    pool_w_b = _block_diag(pool_w).astype(BF16)
    conv_b, pool_b3, pool_s3 = lru_conv_b[:, None, :], pool_b[:, None, :], pool_scale[:, None, :]

    for l in range(DEPTH):
        last = l == DEPTH - 1
        q, k, vt, rest = _in_projection(l, ctx, x, mod, norm_g, w_in_b, consts, gq, gk)
        att = _attention(q, k, vt, context=False)
        rec, pool = _lru_pool(l, rest, lru_conv_w, conv_b, gate_w, gate_b, lru_lambda,
                              pool_w_b, pool_b3, pool_s3)
        weights = (mod, norm_g, w_out_b, w_ffn_in_b, w_ffn_out_b)
        x_next = _out_ffn(l, att, rec, pool, x, *weights, context=False)
        if not last:
            att_ctx = _attention(q, k, vt, context=True)
            ctx = _out_ffn(l, att_ctx, rec, pool, ctx, *weights, context=True)
        x = x_next
    return x
```

```python
import functools

import numpy as np
import jax
import jax.numpy as jnp
from jax import lax
from jax.experimental import pallas as pl
from jax.experimental.pallas import tpu as pltpu

D_MODEL = 1024
BATCH = 4
SEQ = 4096
DEPTH = 2
GRID_W = 64
CTX_LEN = 256
LALL = CTX_LEN + SEQ

N_Q_HEADS = 8
N_KV_HEADS = 2
HEAD_DIM = 64
Q_GROUP = N_Q_HEADS // N_KV_HEADS
ATTN_W = N_Q_HEADS * HEAD_DIM
KV_W = N_KV_HEADS * HEAD_DIM
ROPE_THETA = 10000.0
LRU_W = D_MODEL // 4
LRU_BLOCKS = 4
LRU_BW = LRU_W // LRU_BLOCKS
CONV_W = 4
LRU_C = 8.0
POOL_W = D_MODEL // 4
POOL_GROUPS = 4
POOL_GW = POOL_W // POOL_GROUPS
POOL_WINDOWS = (2, 4, 8, 16)
MIX_W = ATTN_W + LRU_W + POOL_W
IN_W = ATTN_W + 2 * KV_W + 2 * LRU_W + POOL_W
REST_W = 2 * LRU_W + POOL_W
FFN_HIDDEN = -(-8 * D_MODEL // 768) * 256
RMS_EPS = 1e-6

V7X_LANES = 128
V7X_SUBLANES = 8
V7X_VMEM_BYTES = 64 * 1024 * 1024

TOKEN_TILE = CTX_LEN
N_TILES = LALL // TOKEN_TILE
MOD_ROWS = 8
MOD_TILE_N = 1536
ROPE_HALF = HEAD_DIM // 4
GROUPS_PER_TILE = TOKEN_TILE // V7X_SUBLANES
ONES_ROWS = 16
ATTN_Q_TILES = 2
KEY_CHUNK = TOKEN_TILE
SCORE_SLOTS = 3
FFN_TILES_PER_STEP = 4
FFN_CHUNK = 512
FFN_CHUNKS = [(lo, min(FFN_CHUNK, FFN_HIDDEN - lo)) for lo in range(0, FFN_HIDDEN, FFN_CHUNK)]
Q_SCALE = HEAD_DIM ** -0.5 * float(np.log2(np.e))

F32 = jnp.float32
BF16 = jnp.bfloat16


def _vmem_limit(nbytes):
    return int(min(max(nbytes, 16 * 1024 * 1024), V7X_VMEM_BYTES - 6 * 1024 * 1024))


def _rms(x):
    return x * lax.rsqrt(jnp.mean(x * x, axis=-1, keepdims=True) + RMS_EPS)


def _dot(a, b):
    return jnp.dot(a, b, preferred_element_type=F32)


def _layer_spec(l, shape, **kw):
    return pl.BlockSpec((None,) + shape, lambda *_: (l,) + (0,) * len(shape), **kw)


def _mod_row(b, t):
    return jnp.where(t == 0, BATCH, b)


def _token_specs(width):
    T = TOKEN_TILE
    ctx_spec = pl.BlockSpec((1, T, width), lambda b, t: (b, 0, 0))
    x_spec = pl.BlockSpec((1, T, width), lambda b, t: (b, jnp.maximum(t - 1, 0), 0))
    return ctx_spec, x_spec


def _mod_kernel(c_ref, w_ref, b_ref, o_ref):
    c = c_ref[...]
    h = (c * jax.nn.sigmoid(c)).astype(BF16)
    o_ref[0] = _dot(h, w_ref[0].astype(BF16)) + b_ref[0]


def _modulation(c_all, w_mod, b_mod):
    n = 6 * D_MODEL
    return pl.pallas_call(
        _mod_kernel,
        grid=(DEPTH, n // MOD_TILE_N),
        in_specs=[
            pl.BlockSpec((MOD_ROWS, D_MODEL), lambda l, j: (0, 0)),
            pl.BlockSpec((1, D_MODEL, MOD_TILE_N), lambda l, j: (l, 0, j)),
            pl.BlockSpec((1, 1, MOD_TILE_N), lambda l, j: (l, 0, j)),
        ],
        out_specs=pl.BlockSpec((1, MOD_ROWS, MOD_TILE_N), lambda l, j: (l, 0, j)),
        out_shape=jax.ShapeDtypeStruct((DEPTH, MOD_ROWS, n), F32),
        compiler_params=pltpu.CompilerParams(
            dimension_semantics=("arbitrary", "arbitrary"),
            vmem_limit_bytes=_vmem_limit(4 * D_MODEL * MOD_TILE_N * 4)),
        name="modulation",
    )(c_all, w_mod, b_mod.reshape(DEPTH, 1, n))


def _head_norm_rope(z, s_ref, g, cos, sin_a, sin_b):
    z2 = z * z
    hi = z2.astype(BF16)
    lo = (z2 - hi.astype(F32)).astype(BF16)
    ms = _dot(hi, s_ref[...]) + _dot(lo, s_ref[...])
    zn = (z * lax.rsqrt(ms + RMS_EPS)) * g
    outs = []
    for c in range(z.shape[1] // V7X_LANES):
        zc = zn[:, c * V7X_LANES:(c + 1) * V7X_LANES]
        up = pltpu.roll(zc, V7X_LANES - ROPE_HALF, 1)
        dn = pltpu.roll(zc, ROPE_HALF, 1)
        outs.append(zc * cos + up * sin_a + dn * sin_b)
    return outs


def _head_norm_rope_t(zt, g, cos_t, sin_t):
    zn = (zt * lax.rsqrt(jnp.mean(zt * zt, axis=0, keepdims=True) + RMS_EPS)) * g
    h = ROPE_HALF
    partner = jnp.concatenate([zn[h:2 * h], zn[0:h], zn[3 * h:4 * h], zn[2 * h:3 * h]], axis=0)
    return zn * cos_t + partner * sin_t


def _inproj_kernel(ctx_ref, x_ref, mod_ref, g_ref, w_ref, sk_ref, gq_ref, gk_ref,
                   cos_ref, sa_ref, sb_ref, cost_ref, sint_ref, q_ref, k_ref, vt_ref, rest_ref):
    x = jnp.where(pl.program_id(1) == 0, ctx_ref[0], x_ref[0])
    mod = mod_ref[0]
    h = ((_rms(x) * g_ref[0:1]) * (1.0 + mod[1:2]) + mod[0:1]).astype(BF16)
    qkv_w = ATTN_W + 2 * KV_W
    first_w = qkv_w + LRU_W
    y = _dot(h, w_ref[:, 0:first_w])
    (kc,) = _head_norm_rope(y[:, ATTN_W:ATTN_W + KV_W], sk_ref, gk_ref[...],
                            cos_ref[...], sa_ref[...], sb_ref[...])
    rest_ref[0, :, 0:LRU_W] = y[:, qkv_w:first_w]
    rest_ref[0, :, LRU_W:REST_W] = _dot(h, w_ref[:, first_w:IN_W])

    for c in range(ATTN_W // V7X_LANES):
        zt = y[:, c * V7X_LANES:(c + 1) * V7X_LANES].T
        for i in range(V7X_LANES // HEAD_DIM):
            qt = _head_norm_rope_t(zt[i * HEAD_DIM:(i + 1) * HEAD_DIM], gq_ref[...],
                                   cost_ref[...], sint_ref[...])
            q_ref[0, 2 * c + i, 0] = (qt * Q_SCALE).astype(BF16)

    kc = kc.astype(BF16)
    k_ref[0, 0] = kc[:, 0:HEAD_DIM]
    k_ref[0, 1] = kc[:, HEAD_DIM:2 * HEAD_DIM]

    vt = y[:, ATTN_W + KV_W:qkv_w].T.astype(BF16)
    vt_ref[0, 0, 0] = vt[0:HEAD_DIM]
    vt_ref[0, 1, 0] = vt[HEAD_DIM:2 * HEAD_DIM]


def _in_projection(l, ctx_src, x_src, mod, norm_g, w_in, consts, gq, gk):
    sk, cos, sin_a, sin_b, cos_t, sin_t = consts
    T = TOKEN_TILE
    full = lambda shape: pl.BlockSpec(shape, lambda b, t: (0,) * len(shape))
    tab = pl.BlockSpec((T, V7X_LANES), lambda b, t: (t, 0))
    return pl.pallas_call(
        _inproj_kernel,
        grid=(BATCH, N_TILES),
        in_specs=[
            *_token_specs(D_MODEL),
            pl.BlockSpec((None, 1, 6, D_MODEL), lambda b, t: (l, _mod_row(b, t), 0, 0)),
            _layer_spec(l, (4, D_MODEL)),
            _layer_spec(l, (D_MODEL, IN_W)),
            full((KV_W, KV_W)),
            _layer_spec(l, (HEAD_DIM, T)),
            _layer_spec(l, (1, KV_W)),
            tab, tab, tab,
            pl.BlockSpec((HEAD_DIM, T), lambda b, t: (0, t)),
            pl.BlockSpec((HEAD_DIM, T), lambda b, t: (0, t)),
        ],
        out_specs=[
            pl.BlockSpec((1, N_Q_HEADS, 1, HEAD_DIM, T), lambda b, t: (b, 0, t, 0, 0)),
            pl.BlockSpec((1, N_KV_HEADS, T, HEAD_DIM), lambda b, t: (b, 0, t, 0)),
            pl.BlockSpec((1, N_KV_HEADS, 1, HEAD_DIM, T), lambda b, t: (b, 0, t, 0, 0)),
            pl.BlockSpec((1, T, REST_W), lambda b, t: (b, t, 0)),
        ],
        out_shape=[
            jax.ShapeDtypeStruct((BATCH, N_Q_HEADS, N_TILES, HEAD_DIM, T), BF16),
            jax.ShapeDtypeStruct((BATCH, N_KV_HEADS, LALL, HEAD_DIM), BF16),
            jax.ShapeDtypeStruct((BATCH, N_KV_HEADS, N_TILES, HEAD_DIM, T), BF16),
            jax.ShapeDtypeStruct((BATCH, LALL, REST_W), F32),
        ],
        compiler_params=pltpu.CompilerParams(
            dimension_semantics=("arbitrary", "arbitrary"),
            vmem_limit_bytes=_vmem_limit(40 * 1024 * 1024)),
        name="in_projection",
    )(ctx_src, x_src, mod, norm_g, w_in, sk, gq, gk, cos, sin_a, sin_b, cos_t, sin_t)


def _attn_kernel(*refs, q_tiles, n_chunks):
    T = TOKEN_TILE
    q_refs = refs[:q_tiles]
    k_ref, vt_ref, o_ref, s_ref, m_ref, acc_ref = refs[q_tiles:]
    pieces = Q_GROUP // 2

    for i in range(q_tiles):
        qt = jnp.concatenate([q_refs[i][0, g, 0] for g in range(Q_GROUP)], axis=1)

        def scores(c, piece):
            cols = slice(2 * piece * T, 2 * (piece + 1) * T)
            s_ref[c % SCORE_SLOTS, :, cols] = _dot(k_ref[0, 0, c * KEY_CHUNK:(c + 1) * KEY_CHUNK, :],
                                                   qt[:, cols])

        def update(c, piece):
            tile, off = divmod(c * KEY_CHUNK, T)
            vt1 = jnp.concatenate([vt_ref[0, 0, tile][:, off:off + KEY_CHUNK],
                                   jnp.ones((ONES_ROWS, KEY_CHUNK), BF16)], axis=0)
            for g in (2 * piece, 2 * piece + 1):
                cols = slice(g * T, (g + 1) * T)
                s = s_ref[c % SCORE_SLOTS, :, cols]
                m = m_ref[i, :, cols]
                m_new = jnp.maximum(m, jnp.max(s, axis=0, keepdims=True))
                alpha = jnp.exp2(m - m_new)
                p = jnp.exp2((s - m_new).astype(BF16))
                m_ref[i, :, cols] = m_new
                acc_ref[i, :, cols] = alpha * acc_ref[i, :, cols] + _dot(vt1, p)

        m_ref[i] = jnp.full(m_ref.shape[1:], -1e30, F32)
        acc_ref[i] = jnp.zeros(acc_ref.shape[1:], F32)
        for c in range(min(SCORE_SLOTS - 1, n_chunks)):
            for piece in range(pieces):
                scores(c, piece)

        @pl.when(pl.program_id(2) >= 0)
        def _():
            for c in range(n_chunks):
                for piece in range(pieces):
                    if c + SCORE_SLOTS - 1 < n_chunks:
                        scores(c + SCORE_SLOTS - 1, piece)
                    update(c, piece)

        o = acc_ref[i, 0:HEAD_DIM] / acc_ref[i, HEAD_DIM:HEAD_DIM + 1]
        o = jnp.concatenate([o[:, g * T:(g + 1) * T] for g in range(Q_GROUP)], axis=0)
        o_ref[0, i * T:(i + 1) * T, :] = o.T.astype(BF16)


def _attention(q, k, vt, *, context):
    T = TOKEN_TILE
    q_tiles = 1 if context else ATTN_Q_TILES
    n_steps = 1 if context else (N_TILES - 1) // q_tiles
    n_cols = Q_GROUP * T
    q_spec = lambda i: pl.BlockSpec(
        (1, Q_GROUP, 1, HEAD_DIM, T), lambda b, h, t: (b, h, 0 if context else q_tiles * t + i + 1, 0, 0))
    return pl.pallas_call(
        functools.partial(_attn_kernel, q_tiles=q_tiles, n_chunks=(CTX_LEN if context else LALL) // KEY_CHUNK),
        grid=(BATCH, N_KV_HEADS, n_steps),
        in_specs=[
            *[q_spec(i) for i in range(q_tiles)],
            pl.BlockSpec((1, 1, LALL, HEAD_DIM), lambda b, h, t: (b, h, 0, 0)),
            pl.BlockSpec((1, 1, N_TILES, HEAD_DIM, T), lambda b, h, t: (b, h, 0, 0, 0)),
        ],
        out_specs=pl.BlockSpec((1, q_tiles * T, Q_GROUP * HEAD_DIM), lambda b, h, t: (b, t, h)),
        out_shape=jax.ShapeDtypeStruct((BATCH, n_steps * q_tiles * T, ATTN_W), BF16),
        scratch_shapes=[
            pltpu.VMEM((SCORE_SLOTS, KEY_CHUNK, n_cols), F32),
            pltpu.VMEM((q_tiles, 1, n_cols), F32),
            pltpu.VMEM((q_tiles, HEAD_DIM + ONES_ROWS, n_cols), F32),
        ],
        compiler_params=pltpu.CompilerParams(
            dimension_semantics=("arbitrary", "arbitrary", "arbitrary"),
            vmem_limit_bytes=_vmem_limit(40 * 1024 * 1024)),
        name="attention_ctx" if context else "attention",
    )(*([q] * q_tiles), k, vt)


def _softplus(z):
    return jnp.maximum(z, 0.0) + jnp.log1p(jnp.exp(-jnp.abs(z)))


def _with_halo(ref, col, c):
    T = TOKEN_TILE
    r0 = pl.multiple_of(c * T, T)
    main = ref[0, pl.ds(r0, T), col:col + LRU_W]
    p0 = pl.multiple_of(jnp.maximum(r0 - V7X_SUBLANES, 0), V7X_SUBLANES)
    n0 = pl.multiple_of(jnp.minimum(r0 + T, LALL - V7X_SUBLANES), V7X_SUBLANES)
    prev = ref[0, pl.ds(p0, V7X_SUBLANES), col:col + LRU_W]
    nxt = ref[0, pl.ds(n0, V7X_SUBLANES), col:col + LRU_W]
    prev = jnp.where(c >= 2, prev, 0.0)
    nxt = jnp.where((c >= 1) & (c <= N_TILES - 2), nxt, 0.0)
    return main, jnp.concatenate([prev, main, nxt], axis=0)


def _shift_rows(ext, k):
    n = ext.shape[0]
    if k == 0:
        return ext[V7X_SUBLANES:V7X_SUBLANES + TOKEN_TILE]
    return pltpu.roll(ext, (-k) % n, 0)[V7X_SUBLANES:V7X_SUBLANES + TOKEN_TILE]


def _lru_pool_kernel(rest_ref, cw_ref, cb_ref, gw_ref, gb_ref, lam_ref, pw_ref, pb_ref, ps_ref,
                     rec_ref, pool_ref, hf_ref, hb_ref, a_ref, u_ref, conv_ref):
    T = TOKEN_TILE
    S = V7X_SUBLANES
    grouped = (GROUPS_PER_TILE, S, LRU_W)
    row_in_group = lax.broadcasted_iota(jnp.int32, grouped, 1)

    def conv_step(c, _):
        main, ext = _with_halo(rest_ref, 0, c)
        conv_ref[pl.ds(pl.multiple_of(c * T, T), T), :] = cb_ref[...] + (
            cw_ref[0:1] * _shift_rows(ext, -2) + cw_ref[1:2] * _shift_rows(ext, -1)
            + cw_ref[2:3] * main + cw_ref[3:4] * _shift_rows(ext, 1))
        return 0

    lax.fori_loop(0, N_TILES, conv_step, 0)

    def local_scan(c, d):
        u = conv_ref[pl.ds(pl.multiple_of(c * T, T), T), :]
        g = _dot(u.astype(BF16), gw_ref[:, d * 2 * LRU_W:(d + 1) * 2 * LRU_W])
        g = jax.nn.sigmoid(g + gb_ref[:, d * 2 * LRU_W:(d + 1) * 2 * LRU_W])
        log_a = (-LRU_C * g[:, 0:LRU_W]) * _softplus(-lam_ref[d:d + 1])
        a = jnp.exp(log_a)
        neg_expm1 = (1.0 - a) * (1.0 + a)
        root = jnp.where(neg_expm1 > 0.0, neg_expm1 * lax.rsqrt(neg_expm1), 0.0)
        v = root * (g[:, LRU_W:2 * LRU_W] * u)
        a, v = a.reshape(grouped), v.reshape(grouped)
        for step in (1, 2, 4):
            if d == 0:
                a_s, v_s = pltpu.roll(a, step, 1), pltpu.roll(v, step, 1)
                keep = row_in_group >= step
            else:
                a_s, v_s = pltpu.roll(a, S - step, 1), pltpu.roll(v, S - step, 1)
                keep = row_in_group < S - step
            v = jnp.where(keep, a * v_s + v, v)
            a = jnp.where(keep, a * a_s, a)
        a_ref[d] = a.reshape(T, LRU_W)
        u_ref[d] = v.reshape(T, LRU_W)

    def tile_step(i, carry):
        h_f, h_b = carry
        c_f = i
        c_b = jnp.where(i == 0, 0, N_TILES - i)
        local_scan(c_f, 0)
        local_scan(c_b, 1)
        r_f = pl.multiple_of(c_f * T, T)
        r_b = pl.multiple_of(c_b * T, T)
        for gi in range(GROUPS_PER_TILE):
            lo = gi * S
            a, v = a_ref[0, lo:lo + S], u_ref[0, lo:lo + S]
            hf_ref[pl.ds(r_f + lo, S), :] = v + a * h_f
            h_f = (jnp.broadcast_to(v[S - 1:S], (S, LRU_W))
                   + jnp.broadcast_to(a[S - 1:S], (S, LRU_W)) * h_f)
            lo = (GROUPS_PER_TILE - 1 - gi) * S
            a, v = a_ref[1, lo:lo + S], u_ref[1, lo:lo + S]
            hb_ref[pl.ds(r_b + lo, S), :] = v + a * h_b
            h_b = (jnp.broadcast_to(v[0:1], (S, LRU_W))
                   + jnp.broadcast_to(a[0:1], (S, LRU_W)) * h_b)
        return h_f, h_b

    zeros = jnp.zeros((S, LRU_W), F32)
    lax.fori_loop(0, N_TILES, tile_step, (zeros, zeros))

    lane = lax.broadcasted_iota(jnp.int32, (T, POOL_W), 1)
    half_win = jnp.where(lane < POOL_GW, POOL_WINDOWS[0] // 2,
                         jnp.where(lane < 2 * POOL_GW, POOL_WINDOWS[1] // 2,
                                   jnp.where(lane < 3 * POOL_GW, POOL_WINDOWS[2] // 2,
                                             POOL_WINDOWS[3] // 2)))
    row = lax.broadcasted_iota(jnp.int32, (T, POOL_W), 0)

    def out_step(c, _):
        r0 = pl.multiple_of(c * T, T)
        gate = rest_ref[0, pl.ds(r0, T), LRU_W:2 * LRU_W]
        h = hf_ref[pl.ds(r0, T), :] + hb_ref[pl.ds(r0, T), :]
        rec_ref[0, pl.ds(r0, T), :] = (jax.nn.gelu(gate) * h).astype(BF16)

        px, ext = _with_halo(rest_ref, 2 * LRU_W, c)
        p2 = ext + pltpu.roll(ext, 1, 0)
        p4 = p2 + pltpu.roll(p2, 2, 0)
        p8 = p4 + pltpu.roll(p4, 4, 0)
        p16 = p8 + pltpu.roll(p8, 8, 0)
        win = jnp.where(lane < POOL_GW, _shift_rows(p2, 0),
                        jnp.where(lane < 2 * POOL_GW, _shift_rows(p4, 1),
                                  jnp.where(lane < 3 * POOL_GW, _shift_rows(p8, 3),
                                            _shift_rows(p16, 7))))
        seg_len = jnp.where(c == 0, CTX_LEN, SEQ)
        pos = row + jnp.where(c == 0, 0, r0 - CTX_LEN)
        cnt = jnp.minimum(pos + half_win, seg_len) - jnp.maximum(pos - half_win, 0)
        d = win / cnt.astype(F32) - px
        y = _dot(d.astype(BF16), pw_ref[...]) + pb_ref[...]
        pool_ref[0, pl.ds(r0, T), :] = (y * ps_ref[...]).astype(BF16)
        return 0

    lax.fori_loop(0, N_TILES, out_step, 0)


def _lru_pool(l, rest, conv_w, conv_b, gate_w, gate_b, lam, pool_w, pool_b, pool_scale):
    seq = lambda w: pl.BlockSpec((1, LALL, w), lambda b: (b, 0, 0))
    return pl.pallas_call(
        _lru_pool_kernel,
        grid=(BATCH,),
        in_specs=[
            seq(REST_W),
            _layer_spec(l, (CONV_W, LRU_W)), _layer_spec(l, (1, LRU_W)),
            _layer_spec(l, (LRU_W, 4 * LRU_W)), _layer_spec(l, (1, 4 * LRU_W)), _layer_spec(l, (2, LRU_W)),
            _layer_spec(l, (POOL_W, POOL_W)), _layer_spec(l, (1, POOL_W)), _layer_spec(l, (1, POOL_W)),
        ],
        out_specs=[seq(LRU_W), seq(POOL_W)],
        out_shape=[jax.ShapeDtypeStruct((BATCH, LALL, LRU_W), BF16),
                   jax.ShapeDtypeStruct((BATCH, LALL, POOL_W), BF16)],
        scratch_shapes=[
            pltpu.VMEM((LALL, LRU_W), F32), pltpu.VMEM((LALL, LRU_W), F32),
            pltpu.VMEM((2, TOKEN_TILE, LRU_W), F32), pltpu.VMEM((2, TOKEN_TILE, LRU_W), F32),
            pltpu.VMEM((LALL, LRU_W), F32),
        ],
        compiler_params=pltpu.CompilerParams(
            dimension_semantics=("arbitrary",),
            vmem_limit_bytes=_vmem_limit(56 * 1024 * 1024)),
        name="lru_pool",
    )(rest, conv_w, conv_b, gate_w, gate_b, lam, pool_w, pool_b, pool_scale)


def _out_ffn_kernel(*refs, n_sub):
    T = TOKEN_TILE
    att, rec, pool = refs[0:n_sub], refs[n_sub:2 * n_sub], refs[2 * n_sub:3 * n_sub]
    x_ref, mod_ref, g_ref, wo_ref, wi_ref, wf_ref, o_ref = refs[3 * n_sub:]
    mod = mod_ref[0]
    ys = [_dot(att[k][0], wo_ref[0:ATTN_W])
          + _dot(rec[k][0], wo_ref[ATTN_W:ATTN_W + LRU_W])
          + _dot(pool[k][0], wo_ref[ATTN_W + LRU_W:MIX_W]) for k in range(n_sub)]
    for k in range(n_sub):
        rows = slice(k * T, (k + 1) * T)
        x = x_ref[0, rows] + mod[2:3] * (_rms(ys[k]) * g_ref[1:2])
        h = ((_rms(x) * g_ref[2:3]) * (1.0 + mod[4:5]) + mod[3:4]).astype(BF16)
        gate_up = lambda lo, n: (_dot(h, wi_ref[:, lo:lo + n]),
                                 _dot(h, wi_ref[:, FFN_HIDDEN + lo:FFN_HIDDEN + lo + n]))
        gu = gate_up(*FFN_CHUNKS[0])
        f = None
        for i, (lo, n) in enumerate(FFN_CHUNKS):
            gu_next = gate_up(*FFN_CHUNKS[i + 1]) if i + 1 < len(FFN_CHUNKS) else None
            g, u = gu
            part = _dot((g * jax.nn.sigmoid(g) * u).astype(BF16), wf_ref[lo:lo + n])
            f = part if f is None else f + part
            gu = gu_next
        o_ref[0, rows] = x + mod[5:6] * (_rms(f) * g_ref[3:4])


def _out_ffn(l, att, rec, pool, x_src, mod, norm_g, w_out, w_ffn_in, w_ffn_out, *, context):
    T = TOKEN_TILE
    n_sub = 1 if context else FFN_TILES_PER_STEP
    n_steps = 1 if context else (N_TILES - 1) // n_sub
    tile = lambda k, skip: (lambda b, t: (b, 0 if context else n_sub * t + k + skip, 0))
    weight = lambda shape: _layer_spec(l, shape, pipeline_mode=pl.Buffered(1))
    subs = range(n_sub)
    return pl.pallas_call(
        functools.partial(_out_ffn_kernel, n_sub=n_sub),
        grid=(BATCH, n_steps),
        in_specs=[
            *[pl.BlockSpec((1, T, ATTN_W), tile(k, 0)) for k in subs],
            *[pl.BlockSpec((1, T, LRU_W), tile(k, 1)) for k in subs],
            *[pl.BlockSpec((1, T, POOL_W), tile(k, 1)) for k in subs],
            pl.BlockSpec((1, n_sub * T, D_MODEL), lambda b, t: (b, t, 0)),
            pl.BlockSpec((None, 1, 6, D_MODEL), lambda b, t: (l, BATCH if context else b, 0, 0)),
            weight((4, D_MODEL)),
            weight((MIX_W, D_MODEL)),
            weight((D_MODEL, 2 * FFN_HIDDEN)),
            weight((FFN_HIDDEN, D_MODEL)),
        ],
        out_specs=pl.BlockSpec((1, n_sub * T, D_MODEL), lambda b, t: (b, t, 0)),
        out_shape=jax.ShapeDtypeStruct((BATCH, n_steps * n_sub * T, D_MODEL), F32),
        compiler_params=pltpu.CompilerParams(
            dimension_semantics=("arbitrary", "arbitrary"),
            vmem_limit_bytes=_vmem_limit(52 * 1024 * 1024)),
        name="out_ffn_ctx" if context else "out_ffn",
    )(*([att] * n_sub), *([rec] * n_sub), *([pool] * n_sub), x_src, mod, norm_g,
      w_out, w_ffn_in, w_ffn_out)


def _position_tables():
    half = ROPE_HALF
    freq = (ROPE_THETA ** (-np.arange(half, dtype=np.float32) / half)).astype(np.float32)
    p = np.arange(SEQ)
    ang_row = (p // GRID_W).astype(np.float32)[:, None] * freq
    ang_col = (p % GRID_W).astype(np.float32)[:, None] * freq
    ang = np.concatenate([ang_row, ang_row, ang_col, ang_col], axis=1)
    first = np.tile(np.arange(HEAD_DIM) % (2 * half) < half, (SEQ, 1))
    cos = np.cos(ang)
    sin_a = np.where(first, -np.sin(ang), 0.0)
    sin_b = np.where(first, 0.0, np.sin(ang))
    pad = lambda a, v: np.concatenate([np.full((CTX_LEN, HEAD_DIM), v), a], axis=0)
    two = lambda a: np.tile(a, (1, V7X_LANES // HEAD_DIM)).astype(np.float32)
    return two(pad(cos, 1.0)), two(pad(sin_a, 0.0)), two(pad(sin_b, 0.0))


def _head_mean_matrix(width):
    blk = np.kron(np.eye(width // HEAD_DIM), np.full((HEAD_DIM, HEAD_DIM), 1.0 / HEAD_DIM))
    return jnp.asarray(blk, dtype=BF16)


def _block_diag(w):
    n, c, d = w.shape[-3:]
    on_diag = jnp.eye(n, dtype=bool)[:, None, :, None]
    out = jnp.where(on_diag, w[..., :, :, None, :], 0.0)
    return out.reshape(w.shape[:-3] + (n * c, n * d))


def kernel(x, c, ctx, c_ctx, w_mod, b_mod, norm_g, w_in, q_norm_g, k_norm_g, lru_conv_w, lru_conv_b,
           lru_gate_w, lru_gate_b, lru_lambda, pool_w, pool_b, pool_scale, w_out, w_ffn_in, w_ffn_out):
    assert x.shape == (BATCH, SEQ, D_MODEL) and ctx.shape == (BATCH, CTX_LEN, D_MODEL)
    cos, sin_a, sin_b = (jnp.asarray(a) for a in _position_tables())
    cos_t, sin_t = cos[:, 0:HEAD_DIM].T, (sin_a + sin_b)[:, 0:HEAD_DIM].T
    consts = (_head_mean_matrix(KV_W), cos, sin_a, sin_b, cos_t, sin_t)

    c_all = jnp.zeros((MOD_ROWS, D_MODEL), F32).at[0:BATCH].set(c).at[BATCH].set(c_ctx)
    mod = _modulation(c_all, w_mod, b_mod).reshape(DEPTH, MOD_ROWS, 6, D_MODEL)

    w_in_b, w_out_b = w_in.astype(BF16), w_out.astype(BF16)
    w_ffn_in_b, w_ffn_out_b = w_ffn_in.astype(BF16), w_ffn_out.astype(BF16)
    gq = jnp.broadcast_to(q_norm_g[:, :, None], (DEPTH, HEAD_DIM, TOKEN_TILE))
    gk = jnp.tile(k_norm_g, (1, N_KV_HEADS))[:, None, :]
    gate_w = _block_diag(lru_gate_w)
    gate_w = gate_w.transpose(0, 3, 1, 2, 4).reshape(DEPTH, LRU_W, 4 * LRU_W).astype(BF16)
    gate_b = lru_gate_b.reshape(DEPTH, 1, 4 * LRU_W)
    pool_w_b = _block_diag(pool_w).astype(BF16)
    conv_b, pool_b3, pool_s3 = lru_conv_b[:, None, :], pool_b[:, None, :], pool_scale[:, None, :]

    for l in range(DEPTH):
        last = l == DEPTH - 1
        q, k, vt, rest = _in_projection(l, ctx, x, mod, norm_g, w_in_b, consts, gq, gk)
        att = _attention(q, k, vt, context=False)
        rec, pool = _lru_pool(l, rest, lru_conv_w, conv_b, gate_w, gate_b, lru_lambda,
                              pool_w_b, pool_b3, pool_s3)
        weights = (mod, norm_g, w_out_b, w_ffn_in_b, w_ffn_out_b)
        x_next = _out_ffn(l, att, rec, pool, x, *weights, context=False)
        if not last:
            att_ctx = _attention(q, k, vt, context=True)
            ctx = _out_ffn(l, att_ctx, rec, pool, ctx, *weights, context=True)
        x = x_next
    return x
```

```python
import functools

import numpy as np
import jax
import jax.numpy as jnp
from jax import lax
from jax.experimental import pallas as pl
from jax.experimental.pallas import tpu as pltpu

D_MODEL = 1024
BATCH = 4
SEQ = 4096
DEPTH = 2
GRID_W = 64
CTX_LEN = 256
LALL = CTX_LEN + SEQ

N_Q_HEADS = 8
N_KV_HEADS = 2
HEAD_DIM = 64
Q_GROUP = N_Q_HEADS // N_KV_HEADS
ATTN_W = N_Q_HEADS * HEAD_DIM
KV_W = N_KV_HEADS * HEAD_DIM
ROPE_THETA = 10000.0
LRU_W = D_MODEL // 4
LRU_BLOCKS = 4
LRU_BW = LRU_W // LRU_BLOCKS
CONV_W = 4
LRU_C = 8.0
POOL_W = D_MODEL // 4
POOL_GROUPS = 4
POOL_GW = POOL_W // POOL_GROUPS
POOL_WINDOWS = (2, 4, 8, 16)
MIX_W = ATTN_W + LRU_W + POOL_W
IN_W = ATTN_W + 2 * KV_W + 2 * LRU_W + POOL_W
REST_W = 2 * LRU_W + POOL_W
FFN_HIDDEN = -(-8 * D_MODEL // 768) * 256
RMS_EPS = 1e-6

V7X_LANES = 128
V7X_SUBLANES = 8
V7X_VMEM_BYTES = 64 * 1024 * 1024

TOKEN_TILE = CTX_LEN
N_TILES = LALL // TOKEN_TILE
MOD_ROWS = 8
MOD_TILE_N = 1536
ROPE_HALF = HEAD_DIM // 4
GROUPS_PER_TILE = TOKEN_TILE // V7X_SUBLANES
ONES_ROWS = 16
ATTN_Q_TILES = 2
KEY_CHUNK = TOKEN_TILE
SCORE_SLOTS = 3
FFN_TILES_PER_STEP = 2
FFN_CHUNK = 512
FFN_CHUNKS = [(lo, min(FFN_CHUNK, FFN_HIDDEN - lo)) for lo in range(0, FFN_HIDDEN, FFN_CHUNK)]
Q_SCALE = HEAD_DIM ** -0.5 * float(np.log2(np.e))

F32 = jnp.float32
BF16 = jnp.bfloat16


def _vmem_limit(nbytes):
    return int(min(max(nbytes, 16 * 1024 * 1024), V7X_VMEM_BYTES - 6 * 1024 * 1024))


def _rms(x):
    return x * lax.rsqrt(jnp.mean(x * x, axis=-1, keepdims=True) + RMS_EPS)


def _dot(a, b):
    return jnp.dot(a, b, preferred_element_type=F32)


def _layer_spec(l, shape, **kw):
    return pl.BlockSpec((None,) + shape, lambda *_: (l,) + (0,) * len(shape), **kw)


def _mod_row(b, t):
    return jnp.where(t == 0, BATCH, b)


def _token_specs(width):
    T = TOKEN_TILE
    ctx_spec = pl.BlockSpec((1, T, width), lambda b, t: (b, 0, 0))
    x_spec = pl.BlockSpec((1, T, width), lambda b, t: (b, jnp.maximum(t - 1, 0), 0))
    return ctx_spec, x_spec


def _mod_kernel(c_ref, w_ref, b_ref, o_ref):
    c = c_ref[...]
    h = (c * jax.nn.sigmoid(c)).astype(BF16)
    o_ref[0] = _dot(h, w_ref[0].astype(BF16)) + b_ref[0]


def _modulation(c_all, w_mod, b_mod):
    n = 6 * D_MODEL
    return pl.pallas_call(
        _mod_kernel,
        grid=(DEPTH, n // MOD_TILE_N),
        in_specs=[
            pl.BlockSpec((MOD_ROWS, D_MODEL), lambda l, j: (0, 0)),
            pl.BlockSpec((1, D_MODEL, MOD_TILE_N), lambda l, j: (l, 0, j)),
            pl.BlockSpec((1, 1, MOD_TILE_N), lambda l, j: (l, 0, j)),
        ],
        out_specs=pl.BlockSpec((1, MOD_ROWS, MOD_TILE_N), lambda l, j: (l, 0, j)),
        out_shape=jax.ShapeDtypeStruct((DEPTH, MOD_ROWS, n), F32),
        compiler_params=pltpu.CompilerParams(
            dimension_semantics=("arbitrary", "arbitrary"),
            vmem_limit_bytes=_vmem_limit(4 * D_MODEL * MOD_TILE_N * 4)),
        name="modulation",
    )(c_all, w_mod, b_mod.reshape(DEPTH, 1, n))


def _head_norm_rope(z, s_ref, g, cos, sin_a, sin_b):
    z2 = z * z
    hi = z2.astype(BF16)
    lo = (z2 - hi.astype(F32)).astype(BF16)
    ms = _dot(hi, s_ref[...]) + _dot(lo, s_ref[...])
    zn = (z * lax.rsqrt(ms + RMS_EPS)) * g
    outs = []
    for c in range(z.shape[1] // V7X_LANES):
        zc = zn[:, c * V7X_LANES:(c + 1) * V7X_LANES]
        up = pltpu.roll(zc, V7X_LANES - ROPE_HALF, 1)
        dn = pltpu.roll(zc, ROPE_HALF, 1)
        outs.append(zc * cos + up * sin_a + dn * sin_b)
    return outs


def _head_norm_rope_t(zt, g, cos_t, sin_t):
    zn = (zt * lax.rsqrt(jnp.mean(zt * zt, axis=0, keepdims=True) + RMS_EPS)) * g
    h = ROPE_HALF
    partner = jnp.concatenate([zn[h:2 * h], zn[0:h], zn[3 * h:4 * h], zn[2 * h:3 * h]], axis=0)
    return zn * cos_t + partner * sin_t


def _first_step():
    return functools.reduce(jnp.logical_and, [pl.program_id(a) == 0 for a in range(2)])


def _inproj_kernel(ctx_ref, x_ref, mod_ref, g_ref, w32_ref, sk_ref, gq_ref, gk_ref,
                   cos_ref, sa_ref, sb_ref, cost_ref, sint_ref, q_ref, k_ref, vt_ref, rest_ref, w_ref):
    @pl.when(_first_step())
    def _():
        w_ref[...] = w32_ref[...].astype(BF16)

    x = jnp.where(pl.program_id(1) == 0, ctx_ref[0], x_ref[0])
    mod = mod_ref[0]
    h = ((_rms(x) * g_ref[0:1]) * (1.0 + mod[1:2]) + mod[0:1]).astype(BF16)
    qkv_w = ATTN_W + 2 * KV_W
    first_w = qkv_w + LRU_W
    y = _dot(h, w_ref[:, 0:first_w])
    (kc,) = _head_norm_rope(y[:, ATTN_W:ATTN_W + KV_W], sk_ref, gk_ref[...],
                            cos_ref[...], sa_ref[...], sb_ref[...])
    rest_ref[0, :, 0:LRU_W] = y[:, qkv_w:first_w]
    rest_ref[0, :, LRU_W:REST_W] = _dot(h, w_ref[:, first_w:IN_W])

    for c in range(ATTN_W // V7X_LANES):
        zt = y[:, c * V7X_LANES:(c + 1) * V7X_LANES].T
        for i in range(V7X_LANES // HEAD_DIM):
            qt = _head_norm_rope_t(zt[i * HEAD_DIM:(i + 1) * HEAD_DIM], gq_ref[...],
                                   cost_ref[...], sint_ref[...])
            q_ref[0, 2 * c + i, 0] = (qt * Q_SCALE).astype(BF16)

    kc = kc.astype(BF16)
    k_ref[0, 0] = kc[:, 0:HEAD_DIM]
    k_ref[0, 1] = kc[:, HEAD_DIM:2 * HEAD_DIM]

    vt = y[:, ATTN_W + KV_W:qkv_w].T.astype(BF16)
    vt_ref[0, 0, 0] = vt[0:HEAD_DIM]
    vt_ref[0, 1, 0] = vt[HEAD_DIM:2 * HEAD_DIM]


def _in_projection(l, ctx_src, x_src, mod, norm_g, w_in, consts, gq, gk):
    sk, cos, sin_a, sin_b, cos_t, sin_t = consts
    T = TOKEN_TILE
    full = lambda shape: pl.BlockSpec(shape, lambda b, t: (0,) * len(shape))
    tab = pl.BlockSpec((T, V7X_LANES), lambda b, t: (t, 0))
    return pl.pallas_call(
        _inproj_kernel,
        grid=(BATCH, N_TILES),
        in_specs=[
            *_token_specs(D_MODEL),
            pl.BlockSpec((None, 1, 6, D_MODEL), lambda b, t: (l, _mod_row(b, t), 0, 0)),
            _layer_spec(l, (4, D_MODEL)),
            _layer_spec(l, (D_MODEL, IN_W), pipeline_mode=pl.Buffered(1)),
            full((KV_W, KV_W)),
            _layer_spec(l, (HEAD_DIM, T)),
            _layer_spec(l, (1, KV_W)),
            tab, tab, tab,
            pl.BlockSpec((HEAD_DIM, T), lambda b, t: (0, t)),
            pl.BlockSpec((HEAD_DIM, T), lambda b, t: (0, t)),
        ],
        out_specs=[
            pl.BlockSpec((1, N_Q_HEADS, 1, HEAD_DIM, T), lambda b, t: (b, 0, t, 0, 0)),
            pl.BlockSpec((1, N_KV_HEADS, T, HEAD_DIM), lambda b, t: (b, 0, t, 0)),
            pl.BlockSpec((1, N_KV_HEADS, 1, HEAD_DIM, T), lambda b, t: (b, 0, t, 0, 0)),
            pl.BlockSpec((1, T, REST_W), lambda b, t: (b, t, 0)),
        ],
        out_shape=[
            jax.ShapeDtypeStruct((BATCH, N_Q_HEADS, N_TILES, HEAD_DIM, T), BF16),
            jax.ShapeDtypeStruct((BATCH, N_KV_HEADS, LALL, HEAD_DIM), BF16),
            jax.ShapeDtypeStruct((BATCH, N_KV_HEADS, N_TILES, HEAD_DIM, T), BF16),
            jax.ShapeDtypeStruct((BATCH, LALL, REST_W), F32),
        ],
        scratch_shapes=[pltpu.VMEM((D_MODEL, IN_W), BF16)],
        compiler_params=pltpu.CompilerParams(
            dimension_semantics=("arbitrary", "arbitrary"),
            vmem_limit_bytes=_vmem_limit(40 * 1024 * 1024)),
        name="in_projection",
    )(ctx_src, x_src, mod, norm_g, w_in, sk, gq, gk, cos, sin_a, sin_b, cos_t, sin_t)


def _attn_kernel(*refs, q_tiles, n_chunks):
    T = TOKEN_TILE
    q_refs = refs[:q_tiles]
    k_ref, vt_ref, o_ref, s_ref, m_ref, acc_ref = refs[q_tiles:]
    pieces = Q_GROUP // 2

    for i in range(q_tiles):
        qt = jnp.concatenate([q_refs[i][0, g, 0] for g in range(Q_GROUP)], axis=1)

        def scores(c, piece):
            cols = slice(2 * piece * T, 2 * (piece + 1) * T)
            s_ref[c % SCORE_SLOTS, :, cols] = _dot(k_ref[0, 0, c * KEY_CHUNK:(c + 1) * KEY_CHUNK, :],
                                                   qt[:, cols])

        def update(c, piece):
            tile, off = divmod(c * KEY_CHUNK, T)
            vt1 = jnp.concatenate([vt_ref[0, 0, tile][:, off:off + KEY_CHUNK],
                                   jnp.ones((ONES_ROWS, KEY_CHUNK), BF16)], axis=0)
            for g in (2 * piece, 2 * piece + 1):
                cols = slice(g * T, (g + 1) * T)
                s = s_ref[c % SCORE_SLOTS, :, cols]
                m = m_ref[i, :, cols]
                m_new = jnp.maximum(m, jnp.max(s, axis=0, keepdims=True))
                alpha = jnp.exp2(m - m_new)
                p = jnp.exp2((s - m_new).astype(BF16))
                m_ref[i, :, cols] = m_new
                acc_ref[i, :, cols] = alpha * acc_ref[i, :, cols] + _dot(vt1, p)

        m_ref[i] = jnp.full(m_ref.shape[1:], -1e30, F32)
        acc_ref[i] = jnp.zeros(acc_ref.shape[1:], F32)
        for c in range(min(SCORE_SLOTS - 1, n_chunks)):
            for piece in range(pieces):
                scores(c, piece)

        @pl.when(pl.program_id(2) >= 0)
        def _():
            for c in range(n_chunks):
                for piece in range(pieces):
                    if c + SCORE_SLOTS - 1 < n_chunks:
                        scores(c + SCORE_SLOTS - 1, piece)
                    update(c, piece)

        o = acc_ref[i, 0:HEAD_DIM] / acc_ref[i, HEAD_DIM:HEAD_DIM + 1]
        o = jnp.concatenate([o[:, g * T:(g + 1) * T] for g in range(Q_GROUP)], axis=0)
        o_ref[0, i * T:(i + 1) * T, :] = o.T.astype(BF16)


def _attention(q, k, vt, *, context):
    T = TOKEN_TILE
    q_tiles = 1 if context else ATTN_Q_TILES
    n_steps = 1 if context else (N_TILES - 1) // q_tiles
    n_cols = Q_GROUP * T
    q_spec = lambda i: pl.BlockSpec(
        (1, Q_GROUP, 1, HEAD_DIM, T), lambda b, h, t: (b, h, 0 if context else q_tiles * t + i + 1, 0, 0))
    return pl.pallas_call(
        functools.partial(_attn_kernel, q_tiles=q_tiles, n_chunks=(CTX_LEN if context else LALL) // KEY_CHUNK),
        grid=(BATCH, N_KV_HEADS, n_steps),
        in_specs=[
            *[q_spec(i) for i in range(q_tiles)],
            pl.BlockSpec((1, 1, LALL, HEAD_DIM), lambda b, h, t: (b, h, 0, 0)),
            pl.BlockSpec((1, 1, N_TILES, HEAD_DIM, T), lambda b, h, t: (b, h, 0, 0, 0)),
        ],
        out_specs=pl.BlockSpec((1, q_tiles * T, Q_GROUP * HEAD_DIM), lambda b, h, t: (b, t, h)),
        out_shape=jax.ShapeDtypeStruct((BATCH, n_steps * q_tiles * T, ATTN_W), BF16),
        scratch_shapes=[
            pltpu.VMEM((SCORE_SLOTS, KEY_CHUNK, n_cols), F32),
            pltpu.VMEM((q_tiles, 1, n_cols), F32),
            pltpu.VMEM((q_tiles, HEAD_DIM + ONES_ROWS, n_cols), F32),
        ],
        compiler_params=pltpu.CompilerParams(
            dimension_semantics=("arbitrary", "arbitrary", "arbitrary"),
            vmem_limit_bytes=_vmem_limit(40 * 1024 * 1024)),
        name="attention_ctx" if context else "attention",
    )(*([q] * q_tiles), k, vt)


def _softplus(z):
    return jnp.maximum(z, 0.0) + jnp.log1p(jnp.exp(-jnp.abs(z)))


def _with_halo(ref, col, c):
    T = TOKEN_TILE
    r0 = pl.multiple_of(c * T, T)
    main = ref[0, pl.ds(r0, T), col:col + LRU_W]
    p0 = pl.multiple_of(jnp.maximum(r0 - V7X_SUBLANES, 0), V7X_SUBLANES)
    n0 = pl.multiple_of(jnp.minimum(r0 + T, LALL - V7X_SUBLANES), V7X_SUBLANES)
    prev = ref[0, pl.ds(p0, V7X_SUBLANES), col:col + LRU_W]
    nxt = ref[0, pl.ds(n0, V7X_SUBLANES), col:col + LRU_W]
    prev = jnp.where(c >= 2, prev, 0.0)
    nxt = jnp.where((c >= 1) & (c <= N_TILES - 2), nxt, 0.0)
    return main, jnp.concatenate([prev, main, nxt], axis=0)


def _shift_rows(ext, k):
    n = ext.shape[0]
    if k == 0:
        return ext[V7X_SUBLANES:V7X_SUBLANES + TOKEN_TILE]
    return pltpu.roll(ext, (-k) % n, 0)[V7X_SUBLANES:V7X_SUBLANES + TOKEN_TILE]


def _lru_pool_kernel(rest_ref, cw_ref, cb_ref, gw_ref, gb_ref, lam_ref, pw_ref, pb_ref, ps_ref,
                     rec_ref, pool_ref, hf_ref, hb_ref, a_ref, u_ref, conv_ref):
    T = TOKEN_TILE
    S = V7X_SUBLANES
    grouped = (GROUPS_PER_TILE, S, LRU_W)
    row_in_group = lax.broadcasted_iota(jnp.int32, grouped, 1)

    def conv_step(c, _):
        main, ext = _with_halo(rest_ref, 0, c)
        conv_ref[pl.ds(pl.multiple_of(c * T, T), T), :] = cb_ref[...] + (
            cw_ref[0:1] * _shift_rows(ext, -2) + cw_ref[1:2] * _shift_rows(ext, -1)
            + cw_ref[2:3] * main + cw_ref[3:4] * _shift_rows(ext, 1))
        return 0

    lax.fori_loop(0, N_TILES, conv_step, 0)

    def local_scan(c, d):
        u = conv_ref[pl.ds(pl.multiple_of(c * T, T), T), :]
        g = _dot(u.astype(BF16), gw_ref[:, d * 2 * LRU_W:(d + 1) * 2 * LRU_W])
        g = jax.nn.sigmoid(g + gb_ref[:, d * 2 * LRU_W:(d + 1) * 2 * LRU_W])
        log_a = (-LRU_C * g[:, 0:LRU_W]) * _softplus(-lam_ref[d:d + 1])
        a = jnp.exp(log_a)
        neg_expm1 = (1.0 - a) * (1.0 + a)
        root = jnp.where(neg_expm1 > 0.0, neg_expm1 * lax.rsqrt(neg_expm1), 0.0)
        v = root * (g[:, LRU_W:2 * LRU_W] * u)
        a, v = a.reshape(grouped), v.reshape(grouped)
        for step in (1, 2, 4):
            if d == 0:
                a_s, v_s = pltpu.roll(a, step, 1), pltpu.roll(v, step, 1)
                keep = row_in_group >= step
            else:
                a_s, v_s = pltpu.roll(a, S - step, 1), pltpu.roll(v, S - step, 1)
                keep = row_in_group < S - step
            v = jnp.where(keep, a * v_s + v, v)
            a = jnp.where(keep, a * a_s, a)
        a_ref[d] = a.reshape(T, LRU_W)
        u_ref[d] = v.reshape(T, LRU_W)

    def tile_step(i, carry):
        h_f, h_b = carry
        c_f = i
        c_b = jnp.where(i == 0, 0, N_TILES - i)
        local_scan(c_f, 0)
        local_scan(c_b, 1)
        r_f = pl.multiple_of(c_f * T, T)
        r_b = pl.multiple_of(c_b * T, T)
        for gi in range(GROUPS_PER_TILE):
            lo = gi * S
            a, v = a_ref[0, lo:lo + S], u_ref[0, lo:lo + S]
            hf_ref[pl.ds(r_f + lo, S), :] = v + a * h_f
            h_f = (jnp.broadcast_to(v[S - 1:S], (S, LRU_W))
                   + jnp.broadcast_to(a[S - 1:S], (S, LRU_W)) * h_f)
            lo = (GROUPS_PER_TILE - 1 - gi) * S
            a, v = a_ref[1, lo:lo + S], u_ref[1, lo:lo + S]
            hb_ref[pl.ds(r_b + lo, S), :] = v + a * h_b
            h_b = (jnp.broadcast_to(v[0:1], (S, LRU_W))
                   + jnp.broadcast_to(a[0:1], (S, LRU_W)) * h_b)
        return h_f, h_b

    zeros = jnp.zeros((S, LRU_W), F32)
    lax.fori_loop(0, N_TILES, tile_step, (zeros, zeros))

    lane = lax.broadcasted_iota(jnp.int32, (T, POOL_W), 1)
    half_win = jnp.where(lane < POOL_GW, POOL_WINDOWS[0] // 2,
                         jnp.where(lane < 2 * POOL_GW, POOL_WINDOWS[1] // 2,
                                   jnp.where(lane < 3 * POOL_GW, POOL_WINDOWS[2] // 2,
                                             POOL_WINDOWS[3] // 2)))
    row = lax.broadcasted_iota(jnp.int32, (T, POOL_W), 0)

    def out_step(c, _):
        r0 = pl.multiple_of(c * T, T)
        gate = rest_ref[0, pl.ds(r0, T), LRU_W:2 * LRU_W]
        h = hf_ref[pl.ds(r0, T), :] + hb_ref[pl.ds(r0, T), :]
        rec_ref[0, pl.ds(r0, T), :] = (jax.nn.gelu(gate) * h).astype(BF16)

        px, ext = _with_halo(rest_ref, 2 * LRU_W, c)
        p2 = ext + pltpu.roll(ext, 1, 0)
        p4 = p2 + pltpu.roll(p2, 2, 0)
        p8 = p4 + pltpu.roll(p4, 4, 0)
        p16 = p8 + pltpu.roll(p8, 8, 0)
        win = jnp.where(lane < POOL_GW, _shift_rows(p2, 0),
                        jnp.where(lane < 2 * POOL_GW, _shift_rows(p4, 1),
                                  jnp.where(lane < 3 * POOL_GW, _shift_rows(p8, 3),
                                            _shift_rows(p16, 7))))
        seg_len = jnp.where(c == 0, CTX_LEN, SEQ)
        pos = row + jnp.where(c == 0, 0, r0 - CTX_LEN)
        cnt = jnp.minimum(pos + half_win, seg_len) - jnp.maximum(pos - half_win, 0)
        d = win / cnt.astype(F32) - px
        y = _dot(d.astype(BF16), pw_ref[...]) + pb_ref[...]
        pool_ref[0, pl.ds(r0, T), :] = (y * ps_ref[...]).astype(BF16)
        return 0

    lax.fori_loop(0, N_TILES, out_step, 0)


def _lru_pool(l, rest, conv_w, conv_b, gate_w, gate_b, lam, pool_w, pool_b, pool_scale):
    seq = lambda w: pl.BlockSpec((1, LALL, w), lambda b: (b, 0, 0))
    return pl.pallas_call(
        _lru_pool_kernel,
        grid=(BATCH,),
        in_specs=[
            seq(REST_W),
            _layer_spec(l, (CONV_W, LRU_W)), _layer_spec(l, (1, LRU_W)),
            _layer_spec(l, (LRU_W, 4 * LRU_W)), _layer_spec(l, (1, 4 * LRU_W)), _layer_spec(l, (2, LRU_W)),
            _layer_spec(l, (POOL_W, POOL_W)), _layer_spec(l, (1, POOL_W)), _layer_spec(l, (1, POOL_W)),
        ],
        out_specs=[seq(LRU_W), seq(POOL_W)],
        out_shape=[jax.ShapeDtypeStruct((BATCH, LALL, LRU_W), BF16),
                   jax.ShapeDtypeStruct((BATCH, LALL, POOL_W), BF16)],
        scratch_shapes=[
            pltpu.VMEM((LALL, LRU_W), F32), pltpu.VMEM((LALL, LRU_W), F32),
            pltpu.VMEM((2, TOKEN_TILE, LRU_W), F32), pltpu.VMEM((2, TOKEN_TILE, LRU_W), F32),
            pltpu.VMEM((LALL, LRU_W), F32),
        ],
        compiler_params=pltpu.CompilerParams(
            dimension_semantics=("arbitrary",),
            vmem_limit_bytes=_vmem_limit(56 * 1024 * 1024)),
        name="lru_pool",
    )(rest, conv_w, conv_b, gate_w, gate_b, lam, pool_w, pool_b, pool_scale)


def _out_ffn_kernel(*refs, n_sub):
    T = TOKEN_TILE
    att, rec, pool = refs[0:n_sub], refs[n_sub:2 * n_sub], refs[2 * n_sub:3 * n_sub]
    x_ref, mod_ref, g_ref, wo32_ref, wi_ref, wf32_ref, o_ref, wo_ref, wf_ref = refs[3 * n_sub:]

    @pl.when(_first_step())
    def _():
        wo_ref[...] = wo32_ref[...].astype(BF16)
        wf_ref[...] = wf32_ref[...].astype(BF16)

    mod = mod_ref[0]
    ys = [_dot(att[k][0], wo_ref[0:ATTN_W])
          + _dot(rec[k][0], wo_ref[ATTN_W:ATTN_W + LRU_W])
          + _dot(pool[k][0], wo_ref[ATTN_W + LRU_W:MIX_W]) for k in range(n_sub)]
    for k in range(n_sub):
        rows = slice(k * T, (k + 1) * T)
        x = x_ref[0, rows] + mod[2:3] * (_rms(ys[k]) * g_ref[1:2])
        h = ((_rms(x) * g_ref[2:3]) * (1.0 + mod[4:5]) + mod[3:4]).astype(BF16)
        gate_up = lambda lo, n: (_dot(h, wi_ref[:, lo:lo + n]),
                                 _dot(h, wi_ref[:, FFN_HIDDEN + lo:FFN_HIDDEN + lo + n]))
        gu = gate_up(*FFN_CHUNKS[0])
        f = None
        for i, (lo, n) in enumerate(FFN_CHUNKS):
            gu_next = gate_up(*FFN_CHUNKS[i + 1]) if i + 1 < len(FFN_CHUNKS) else None
            g, u = gu
            part = _dot((g * jax.nn.sigmoid(g) * u).astype(BF16), wf_ref[lo:lo + n])
            f = part if f is None else f + part
            gu = gu_next
        o_ref[0, rows] = x + mod[5:6] * (_rms(f) * g_ref[3:4])


def _out_ffn(l, att, rec, pool, x_src, mod, norm_g, w_out, w_ffn_in, w_ffn_out, *, context):
    T = TOKEN_TILE
    n_sub = 1 if context else FFN_TILES_PER_STEP
    n_steps = 1 if context else (N_TILES - 1) // n_sub
    tile = lambda k, skip: (lambda b, t: (b, 0 if context else n_sub * t + k + skip, 0))
    weight = lambda shape: _layer_spec(l, shape, pipeline_mode=pl.Buffered(1))
    subs = range(n_sub)
    return pl.pallas_call(
        functools.partial(_out_ffn_kernel, n_sub=n_sub),
        grid=(BATCH, n_steps),
        in_specs=[
            *[pl.BlockSpec((1, T, ATTN_W), tile(k, 0)) for k in subs],
            *[pl.BlockSpec((1, T, LRU_W), tile(k, 1)) for k in subs],
            *[pl.BlockSpec((1, T, POOL_W), tile(k, 1)) for k in subs],
            pl.BlockSpec((1, n_sub * T, D_MODEL), lambda b, t: (b, t, 0)),
            pl.BlockSpec((None, 1, 6, D_MODEL), lambda b, t: (l, BATCH if context else b, 0, 0)),
            weight((4, D_MODEL)),
            weight((MIX_W, D_MODEL)),
            weight((D_MODEL, 2 * FFN_HIDDEN)),
            weight((FFN_HIDDEN, D_MODEL)),
        ],
        out_specs=pl.BlockSpec((1, n_sub * T, D_MODEL), lambda b, t: (b, t, 0)),
        out_shape=jax.ShapeDtypeStruct((BATCH, n_steps * n_sub * T, D_MODEL), F32),
        scratch_shapes=[pltpu.VMEM((MIX_W, D_MODEL), BF16), pltpu.VMEM((FFN_HIDDEN, D_MODEL), BF16)],
        compiler_params=pltpu.CompilerParams(
            dimension_semantics=("arbitrary", "arbitrary"),
            vmem_limit_bytes=_vmem_limit(58 * 1024 * 1024)),
        name="out_ffn_ctx" if context else "out_ffn",
    )(*([att] * n_sub), *([rec] * n_sub), *([pool] * n_sub), x_src, mod, norm_g,
      w_out, w_ffn_in, w_ffn_out)


def _position_tables():
    half = ROPE_HALF
    freq = (ROPE_THETA ** (-np.arange(half, dtype=np.float32) / half)).astype(np.float32)
    p = np.arange(SEQ)
    ang_row = (p // GRID_W).astype(np.float32)[:, None] * freq
    ang_col = (p % GRID_W).astype(np.float32)[:, None] * freq
    ang = np.concatenate([ang_row, ang_row, ang_col, ang_col], axis=1)
    first = np.tile(np.arange(HEAD_DIM) % (2 * half) < half, (SEQ, 1))
    cos = np.cos(ang)
    sin_a = np.where(first, -np.sin(ang), 0.0)
    sin_b = np.where(first, 0.0, np.sin(ang))
    pad = lambda a, v: np.concatenate([np.full((CTX_LEN, HEAD_DIM), v), a], axis=0)
    two = lambda a: np.tile(a, (1, V7X_LANES // HEAD_DIM)).astype(np.float32)
    return two(pad(cos, 1.0)), two(pad(sin_a, 0.0)), two(pad(sin_b, 0.0))


def _head_mean_matrix(width):
    blk = np.kron(np.eye(width // HEAD_DIM), np.full((HEAD_DIM, HEAD_DIM), 1.0 / HEAD_DIM))
    return jnp.asarray(blk, dtype=BF16)


def _block_diag(w):
    n, c, d = w.shape[-3:]
    on_diag = jnp.eye(n, dtype=bool)[:, None, :, None]
    out = jnp.where(on_diag, w[..., :, :, None, :], 0.0)
    return out.reshape(w.shape[:-3] + (n * c, n * d))


def kernel(x, c, ctx, c_ctx, w_mod, b_mod, norm_g, w_in, q_norm_g, k_norm_g, lru_conv_w, lru_conv_b,
           lru_gate_w, lru_gate_b, lru_lambda, pool_w, pool_b, pool_scale, w_out, w_ffn_in, w_ffn_out):
    assert x.shape == (BATCH, SEQ, D_MODEL) and ctx.shape == (BATCH, CTX_LEN, D_MODEL)
    cos, sin_a, sin_b = (jnp.asarray(a) for a in _position_tables())
    cos_t, sin_t = cos[:, 0:HEAD_DIM].T, (sin_a + sin_b)[:, 0:HEAD_DIM].T
    consts = (_head_mean_matrix(KV_W), cos, sin_a, sin_b, cos_t, sin_t)

    c_all = jnp.zeros((MOD_ROWS, D_MODEL), F32).at[0:BATCH].set(c).at[BATCH].set(c_ctx)
    mod = _modulation(c_all, w_mod, b_mod).reshape(DEPTH, MOD_ROWS, 6, D_MODEL)

    w_ffn_in_b = w_ffn_in.astype(BF16)
    gq = jnp.broadcast_to(q_norm_g[:, :, None], (DEPTH, HEAD_DIM, TOKEN_TILE))
    gk = jnp.tile(k_norm_g, (1, N_KV_HEADS))[:, None, :]
    gate_w = _block_diag(lru_gate_w)
    gate_w = gate_w.transpose(0, 3, 1, 2, 4).reshape(DEPTH, LRU_W, 4 * LRU_W).astype(BF16)
    gate_b = lru_gate_b.reshape(DEPTH, 1, 4 * LRU_W)
    pool_w_b = _block_diag(pool_w).astype(BF16)
    conv_b, pool_b3, pool_s3 = lru_conv_b[:, None, :], pool_b[:, None, :], pool_scale[:, None, :]

    for l in range(DEPTH):
        last = l == DEPTH - 1
        q, k, vt, rest = _in_projection(l, ctx, x, mod, norm_g, w_in, consts, gq, gk)
        att = _attention(q, k, vt, context=False)
        rec, pool = _lru_pool(l, rest, lru_conv_w, conv_b, gate_w, gate_b, lru_lambda,
                              pool_w_b, pool_b3, pool_s3)
        weights = (mod, norm_g, w_out, w_ffn_in_b, w_ffn_out)
        x_next = _out_ffn(l, att, rec, pool, x, *weights, context=False)
        if not last:
            att_ctx = _attention(q, k, vt, context=True)
            ctx = _out_ffn(l, att_ctx, rec, pool, ctx, *weights, context=True)
        x = x_next
    return x
```

```python
import functools

import numpy as np
import jax
import jax.numpy as jnp
from jax import lax
from jax.experimental import pallas as pl
from jax.experimental.pallas import tpu as pltpu

D_MODEL = 1024
BATCH = 4
SEQ = 4096
DEPTH = 2
GRID_W = 64
CTX_LEN = 256
LALL = CTX_LEN + SEQ

N_Q_HEADS = 8
N_KV_HEADS = 2
HEAD_DIM = 64
Q_GROUP = N_Q_HEADS // N_KV_HEADS
ATTN_W = N_Q_HEADS * HEAD_DIM
KV_W = N_KV_HEADS * HEAD_DIM
ROPE_THETA = 10000.0
LRU_W = D_MODEL // 4
LRU_BLOCKS = 4
LRU_BW = LRU_W // LRU_BLOCKS
CONV_W = 4
LRU_C = 8.0
POOL_W = D_MODEL // 4
POOL_GROUPS = 4
POOL_GW = POOL_W // POOL_GROUPS
POOL_WINDOWS = (2, 4, 8, 16)
MIX_W = ATTN_W + LRU_W + POOL_W
IN_W = ATTN_W + 2 * KV_W + 2 * LRU_W + POOL_W
REST_W = 2 * LRU_W + POOL_W
FFN_HIDDEN = -(-8 * D_MODEL // 768) * 256
RMS_EPS = 1e-6

V7X_LANES = 128
V7X_SUBLANES = 8
V7X_VMEM_BYTES = 64 * 1024 * 1024

TOKEN_TILE = CTX_LEN
N_TILES = LALL // TOKEN_TILE
MOD_ROWS = 8
MOD_TILE_N = 1536
ROPE_HALF = HEAD_DIM // 4
GROUPS_PER_TILE = TOKEN_TILE // V7X_SUBLANES
ONES_ROWS = 16
ATTN_Q_TILES = 4
KEY_CHUNK = TOKEN_TILE
SCORE_SLOTS = 3
FFN_TILES_PER_STEP = 4
FFN_CHUNK = 512
FFN_CHUNKS = [(lo, min(FFN_CHUNK, FFN_HIDDEN - lo)) for lo in range(0, FFN_HIDDEN, FFN_CHUNK)]
Q_SCALE = HEAD_DIM ** -0.5 * float(np.log2(np.e))

F32 = jnp.float32
BF16 = jnp.bfloat16


def _vmem_limit(nbytes):
    return int(min(max(nbytes, 16 * 1024 * 1024), V7X_VMEM_BYTES - 6 * 1024 * 1024))


def _rms(x):
    return x * lax.rsqrt(jnp.mean(x * x, axis=-1, keepdims=True) + RMS_EPS)


def _dot(a, b):
    return jnp.dot(a, b, preferred_element_type=F32)


def _layer_spec(l, shape, **kw):
    return pl.BlockSpec((None,) + shape, lambda *_: (l,) + (0,) * len(shape), **kw)


def _mod_row(b, t):
    return jnp.where(t == 0, BATCH, b)


def _token_specs(width):
    T = TOKEN_TILE
    ctx_spec = pl.BlockSpec((1, T, width), lambda b, t: (b, 0, 0))
    x_spec = pl.BlockSpec((1, T, width), lambda b, t: (b, jnp.maximum(t - 1, 0), 0))
    return ctx_spec, x_spec


def _mod_kernel(c_ref, w_ref, b_ref, o_ref):
    c = c_ref[...]
    h = (c * jax.nn.sigmoid(c)).astype(BF16)
    o_ref[0] = _dot(h, w_ref[0].astype(BF16)) + b_ref[0]


def _modulation(c_all, w_mod, b_mod):
    n = 6 * D_MODEL
    return pl.pallas_call(
        _mod_kernel,
        grid=(DEPTH, n // MOD_TILE_N),
        in_specs=[
            pl.BlockSpec((MOD_ROWS, D_MODEL), lambda l, j: (0, 0)),
            pl.BlockSpec((1, D_MODEL, MOD_TILE_N), lambda l, j: (l, 0, j)),
            pl.BlockSpec((1, 1, MOD_TILE_N), lambda l, j: (l, 0, j)),
        ],
        out_specs=pl.BlockSpec((1, MOD_ROWS, MOD_TILE_N), lambda l, j: (l, 0, j)),
        out_shape=jax.ShapeDtypeStruct((DEPTH, MOD_ROWS, n), F32),
        compiler_params=pltpu.CompilerParams(
            dimension_semantics=("arbitrary", "arbitrary"),
            vmem_limit_bytes=_vmem_limit(4 * D_MODEL * MOD_TILE_N * 4)),
        name="modulation",
    )(c_all, w_mod, b_mod.reshape(DEPTH, 1, n))


def _head_norm_rope(z, s_ref, g, cos, sin_a, sin_b):
    z2 = z * z
    hi = z2.astype(BF16)
    lo = (z2 - hi.astype(F32)).astype(BF16)
    ms = _dot(hi, s_ref[...]) + _dot(lo, s_ref[...])
    zn = (z * lax.rsqrt(ms + RMS_EPS)) * g
    outs = []
    for c in range(z.shape[1] // V7X_LANES):
        zc = zn[:, c * V7X_LANES:(c + 1) * V7X_LANES]
        up = pltpu.roll(zc, V7X_LANES - ROPE_HALF, 1)
        dn = pltpu.roll(zc, ROPE_HALF, 1)
        outs.append(zc * cos + up * sin_a + dn * sin_b)
    return outs


def _head_norm_rope_t(zt, g, cos_t, sin_t):
    zn = (zt * lax.rsqrt(jnp.mean(zt * zt, axis=0, keepdims=True) + RMS_EPS)) * g
    h = ROPE_HALF
    partner = jnp.concatenate([zn[h:2 * h], zn[0:h], zn[3 * h:4 * h], zn[2 * h:3 * h]], axis=0)
    return zn * cos_t + partner * sin_t


def _inproj_kernel(ctx_ref, x_ref, mod_ref, g_ref, w_ref, sk_ref, gq_ref, gk_ref,
                   cos_ref, sa_ref, sb_ref, cost_ref, sint_ref, q_ref, k_ref, vt_ref, rest_ref):
    x = jnp.where(pl.program_id(1) == 0, ctx_ref[0], x_ref[0])
    mod = mod_ref[0]
    h = ((_rms(x) * g_ref[0:1]) * (1.0 + mod[1:2]) + mod[0:1]).astype(BF16)
    qkv_w = ATTN_W + 2 * KV_W
    first_w = qkv_w + LRU_W
    y = _dot(h, w_ref[:, 0:first_w])
    (kc,) = _head_norm_rope(y[:, ATTN_W:ATTN_W + KV_W], sk_ref, gk_ref[...],
                            cos_ref[...], sa_ref[...], sb_ref[...])
    rest_ref[0, :, 0:LRU_W] = y[:, qkv_w:first_w]
    rest_ref[0, :, LRU_W:REST_W] = _dot(h, w_ref[:, first_w:IN_W])

    for c in range(ATTN_W // V7X_LANES):
        zt = y[:, c * V7X_LANES:(c + 1) * V7X_LANES].T
        for i in range(V7X_LANES // HEAD_DIM):
            qt = _head_norm_rope_t(zt[i * HEAD_DIM:(i + 1) * HEAD_DIM], gq_ref[...],
                                   cost_ref[...], sint_ref[...])
            q_ref[0, 2 * c + i, 0] = (qt * Q_SCALE).astype(BF16)

    kc = kc.astype(BF16)
    k_ref[0, 0] = kc[:, 0:HEAD_DIM]
    k_ref[0, 1] = kc[:, HEAD_DIM:2 * HEAD_DIM]

    vt = y[:, ATTN_W + KV_W:qkv_w].T.astype(BF16)
    vt_ref[0, 0, 0] = vt[0:HEAD_DIM]
    vt_ref[0, 1, 0] = vt[HEAD_DIM:2 * HEAD_DIM]


def _in_projection(l, ctx_src, x_src, mod, norm_g, w_in, consts, gq, gk):
    sk, cos, sin_a, sin_b, cos_t, sin_t = consts
    T = TOKEN_TILE
    full = lambda shape: pl.BlockSpec(shape, lambda b, t: (0,) * len(shape))
    tab = pl.BlockSpec((T, V7X_LANES), lambda b, t: (t, 0))
    return pl.pallas_call(
        _inproj_kernel,
        grid=(BATCH, N_TILES),
        in_specs=[
            *_token_specs(D_MODEL),
            pl.BlockSpec((None, 1, 6, D_MODEL), lambda b, t: (l, _mod_row(b, t), 0, 0)),
            _layer_spec(l, (4, D_MODEL)),
            _layer_spec(l, (D_MODEL, IN_W)),
            full((KV_W, KV_W)),
            _layer_spec(l, (HEAD_DIM, T)),
            _layer_spec(l, (1, KV_W)),
            tab, tab, tab,
            pl.BlockSpec((HEAD_DIM, T), lambda b, t: (0, t)),
            pl.BlockSpec((HEAD_DIM, T), lambda b, t: (0, t)),
        ],
        out_specs=[
            pl.BlockSpec((1, N_Q_HEADS, 1, HEAD_DIM, T), lambda b, t: (b, 0, t, 0, 0)),
            pl.BlockSpec((1, N_KV_HEADS, T, HEAD_DIM), lambda b, t: (b, 0, t, 0)),
            pl.BlockSpec((1, N_KV_HEADS, 1, HEAD_DIM, T), lambda b, t: (b, 0, t, 0, 0)),
            pl.BlockSpec((1, T, REST_W), lambda b, t: (b, t, 0)),
        ],
        out_shape=[
            jax.ShapeDtypeStruct((BATCH, N_Q_HEADS, N_TILES, HEAD_DIM, T), BF16),
            jax.ShapeDtypeStruct((BATCH, N_KV_HEADS, LALL, HEAD_DIM), BF16),
            jax.ShapeDtypeStruct((BATCH, N_KV_HEADS, N_TILES, HEAD_DIM, T), BF16),
            jax.ShapeDtypeStruct((BATCH, LALL, REST_W), F32),
        ],
        compiler_params=pltpu.CompilerParams(
            dimension_semantics=("arbitrary", "arbitrary"),
            vmem_limit_bytes=_vmem_limit(40 * 1024 * 1024)),
        name="in_projection",
    )(ctx_src, x_src, mod, norm_g, w_in, sk, gq, gk, cos, sin_a, sin_b, cos_t, sin_t)


def _attn_kernel(*refs, q_tiles, n_chunks):
    T = TOKEN_TILE
    q_refs = refs[:q_tiles]
    k_ref, vt_ref, o_ref, s_ref, m_ref, acc_ref = refs[q_tiles:]
    pieces = Q_GROUP // 2

    def scores(n, piece):
        i, c = divmod(n, n_chunks)
        qt = jnp.concatenate([q_refs[i][0, g, 0] for g in (2 * piece, 2 * piece + 1)], axis=1)
        cols = slice(2 * piece * T, 2 * (piece + 1) * T)
        s_ref[n % SCORE_SLOTS, :, cols] = _dot(k_ref[0, 0, c * KEY_CHUNK:(c + 1) * KEY_CHUNK, :], qt)

    def update(n, piece):
        i, c = divmod(n, n_chunks)
        tile, off = divmod(c * KEY_CHUNK, T)
        vt1 = jnp.concatenate([vt_ref[0, 0, tile][:, off:off + KEY_CHUNK],
                               jnp.ones((ONES_ROWS, KEY_CHUNK), BF16)], axis=0)
        for g in (2 * piece, 2 * piece + 1):
            cols = slice(g * T, (g + 1) * T)
            s = s_ref[n % SCORE_SLOTS, :, cols]
            m = m_ref[i, :, cols]
            m_new = jnp.maximum(m, jnp.max(s, axis=0, keepdims=True))
            alpha = jnp.exp2(m - m_new)
            p = jnp.exp2((s - m_new).astype(BF16))
            m_ref[i, :, cols] = m_new
            acc_ref[i, :, cols] = alpha * acc_ref[i, :, cols] + _dot(vt1, p)

    def finish(i):
        o = acc_ref[i, 0:HEAD_DIM] / acc_ref[i, HEAD_DIM:HEAD_DIM + 1]
        o = jnp.concatenate([o[:, g * T:(g + 1) * T] for g in range(Q_GROUP)], axis=0)
        o_ref[0, i * T:(i + 1) * T, :] = o.T.astype(BF16)

    def tile_block(i):
        if i > 0:
            finish(i - 1)
        for n in range(i * n_chunks, (i + 1) * n_chunks):
            for piece in range(pieces):
                if n + SCORE_SLOTS - 1 < q_tiles * n_chunks:
                    scores(n + SCORE_SLOTS - 1, piece)
                update(n, piece)

    m_ref[...] = jnp.full(m_ref.shape, -1e30, F32)
    acc_ref[...] = jnp.zeros(acc_ref.shape, F32)
    for n in range(min(SCORE_SLOTS - 1, n_chunks)):
        for piece in range(pieces):
            scores(n, piece)
    for i in range(q_tiles):
        pl.when(pl.program_id(2) >= -i)(functools.partial(tile_block, i))
    finish(q_tiles - 1)


def _attention(q, k, vt, *, context):
    T = TOKEN_TILE
    q_tiles = 1 if context else ATTN_Q_TILES
    n_steps = 1 if context else (N_TILES - 1) // q_tiles
    n_cols = Q_GROUP * T
    q_spec = lambda i: pl.BlockSpec(
        (1, Q_GROUP, 1, HEAD_DIM, T), lambda b, h, t: (b, h, 0 if context else q_tiles * t + i + 1, 0, 0))
    return pl.pallas_call(
        functools.partial(_attn_kernel, q_tiles=q_tiles, n_chunks=(CTX_LEN if context else LALL) // KEY_CHUNK),
        grid=(BATCH, N_KV_HEADS, n_steps),
        in_specs=[
            *[q_spec(i) for i in range(q_tiles)],
            pl.BlockSpec((1, 1, LALL, HEAD_DIM), lambda b, h, t: (b, h, 0, 0)),
            pl.BlockSpec((1, 1, N_TILES, HEAD_DIM, T), lambda b, h, t: (b, h, 0, 0, 0)),
        ],
        out_specs=pl.BlockSpec((1, q_tiles * T, Q_GROUP * HEAD_DIM), lambda b, h, t: (b, t, h)),
        out_shape=jax.ShapeDtypeStruct((BATCH, n_steps * q_tiles * T, ATTN_W), BF16),
        scratch_shapes=[
            pltpu.VMEM((SCORE_SLOTS, KEY_CHUNK, n_cols), F32),
            pltpu.VMEM((q_tiles, 1, n_cols), F32),
            pltpu.VMEM((q_tiles, HEAD_DIM + ONES_ROWS, n_cols), F32),
        ],
        compiler_params=pltpu.CompilerParams(
            dimension_semantics=("arbitrary", "arbitrary", "arbitrary"),
            vmem_limit_bytes=_vmem_limit(40 * 1024 * 1024)),
        name="attention_ctx" if context else "attention",
    )(*([q] * q_tiles), k, vt)


def _softplus(z):
    return jnp.maximum(z, 0.0) + jnp.log1p(jnp.exp(-jnp.abs(z)))


def _with_halo(ref, col, c):
    T = TOKEN_TILE
    r0 = pl.multiple_of(c * T, T)
    main = ref[0, pl.ds(r0, T), col:col + LRU_W]
    p0 = pl.multiple_of(jnp.maximum(r0 - V7X_SUBLANES, 0), V7X_SUBLANES)
    n0 = pl.multiple_of(jnp.minimum(r0 + T, LALL - V7X_SUBLANES), V7X_SUBLANES)
    prev = ref[0, pl.ds(p0, V7X_SUBLANES), col:col + LRU_W]
    nxt = ref[0, pl.ds(n0, V7X_SUBLANES), col:col + LRU_W]
    prev = jnp.where(c >= 2, prev, 0.0)
    nxt = jnp.where((c >= 1) & (c <= N_TILES - 2), nxt, 0.0)
    return main, jnp.concatenate([prev, main, nxt], axis=0)


def _shift_rows(ext, k):
    n = ext.shape[0]
    if k == 0:
        return ext[V7X_SUBLANES:V7X_SUBLANES + TOKEN_TILE]
    return pltpu.roll(ext, (-k) % n, 0)[V7X_SUBLANES:V7X_SUBLANES + TOKEN_TILE]


def _lru_pool_kernel(rest_ref, cw_ref, cb_ref, gw_ref, gb_ref, lam_ref, pw_ref, pb_ref, ps_ref,
                     rec_ref, pool_ref, hf_ref, hb_ref, a_ref, u_ref, conv_ref):
    T = TOKEN_TILE
    S = V7X_SUBLANES
    grouped = (GROUPS_PER_TILE, S, LRU_W)
    row_in_group = lax.broadcasted_iota(jnp.int32, grouped, 1)

    def conv_step(c, _):
        main, ext = _with_halo(rest_ref, 0, c)
        conv_ref[pl.ds(pl.multiple_of(c * T, T), T), :] = cb_ref[...] + (
            cw_ref[0:1] * _shift_rows(ext, -2) + cw_ref[1:2] * _shift_rows(ext, -1)
            + cw_ref[2:3] * main + cw_ref[3:4] * _shift_rows(ext, 1))
        return 0

    lax.fori_loop(0, N_TILES, conv_step, 0)

    def local_scan(c, d):
        u = conv_ref[pl.ds(pl.multiple_of(c * T, T), T), :]
        g = _dot(u.astype(BF16), gw_ref[:, d * 2 * LRU_W:(d + 1) * 2 * LRU_W])
        g = jax.nn.sigmoid(g + gb_ref[:, d * 2 * LRU_W:(d + 1) * 2 * LRU_W])
        log_a = (-LRU_C * g[:, 0:LRU_W]) * _softplus(-lam_ref[d:d + 1])
        a = jnp.exp(log_a)
        neg_expm1 = (1.0 - a) * (1.0 + a)
        root = jnp.where(neg_expm1 > 0.0, neg_expm1 * lax.rsqrt(neg_expm1), 0.0)
        v = root * (g[:, LRU_W:2 * LRU_W] * u)
        a, v = a.reshape(grouped), v.reshape(grouped)
        for step in (1, 2, 4):
            if d == 0:
                a_s, v_s = pltpu.roll(a, step, 1), pltpu.roll(v, step, 1)
                keep = row_in_group >= step
            else:
                a_s, v_s = pltpu.roll(a, S - step, 1), pltpu.roll(v, S - step, 1)
                keep = row_in_group < S - step
            v = jnp.where(keep, a * v_s + v, v)
            a = jnp.where(keep, a * a_s, a)
        a_ref[d] = a.reshape(T, LRU_W)
        u_ref[d] = v.reshape(T, LRU_W)

    def tile_step(i, carry):
        h_f, h_b = carry
        c_f = i
        c_b = jnp.where(i == 0, 0, N_TILES - i)
        local_scan(c_f, 0)
        local_scan(c_b, 1)
        r_f = pl.multiple_of(c_f * T, T)
        r_b = pl.multiple_of(c_b * T, T)
        for gi in range(GROUPS_PER_TILE):
            lo = gi * S
            a, v = a_ref[0, lo:lo + S], u_ref[0, lo:lo + S]
            hf_ref[pl.ds(r_f + lo, S), :] = v + a * h_f
            h_f = (jnp.broadcast_to(v[S - 1:S], (S, LRU_W))
                   + jnp.broadcast_to(a[S - 1:S], (S, LRU_W)) * h_f)
            lo = (GROUPS_PER_TILE - 1 - gi) * S
            a, v = a_ref[1, lo:lo + S], u_ref[1, lo:lo + S]
            hb_ref[pl.ds(r_b + lo, S), :] = v + a * h_b
            h_b = (jnp.broadcast_to(v[0:1], (S, LRU_W))
                   + jnp.broadcast_to(a[0:1], (S, LRU_W)) * h_b)
        return h_f, h_b

    zeros = jnp.zeros((S, LRU_W), F32)
    lax.fori_loop(0, N_TILES, tile_step, (zeros, zeros))

    lane = lax.broadcasted_iota(jnp.int32, (T, POOL_W), 1)
    half_win = jnp.where(lane < POOL_GW, POOL_WINDOWS[0] // 2,
                         jnp.where(lane < 2 * POOL_GW, POOL_WINDOWS[1] // 2,
                                   jnp.where(lane < 3 * POOL_GW, POOL_WINDOWS[2] // 2,
                                             POOL_WINDOWS[3] // 2)))
    row = lax.broadcasted_iota(jnp.int32, (T, POOL_W), 0)

    def out_step(c, _):
        r0 = pl.multiple_of(c * T, T)
        gate = rest_ref[0, pl.ds(r0, T), LRU_W:2 * LRU_W]
        h = hf_ref[pl.ds(r0, T), :] + hb_ref[pl.ds(r0, T), :]
        rec_ref[0, pl.ds(r0, T), :] = (jax.nn.gelu(gate) * h).astype(BF16)

        px, ext = _with_halo(rest_ref, 2 * LRU_W, c)
        p2 = ext + pltpu.roll(ext, 1, 0)
        p4 = p2 + pltpu.roll(p2, 2, 0)
        p8 = p4 + pltpu.roll(p4, 4, 0)
        p16 = p8 + pltpu.roll(p8, 8, 0)
        win = jnp.where(lane < POOL_GW, _shift_rows(p2, 0),
                        jnp.where(lane < 2 * POOL_GW, _shift_rows(p4, 1),
                                  jnp.where(lane < 3 * POOL_GW, _shift_rows(p8, 3),
                                            _shift_rows(p16, 7))))
        seg_len = jnp.where(c == 0, CTX_LEN, SEQ)
        pos = row + jnp.where(c == 0, 0, r0 - CTX_LEN)
        cnt = jnp.minimum(pos + half_win, seg_len) - jnp.maximum(pos - half_win, 0)
        d = win / cnt.astype(F32) - px
        y = _dot(d.astype(BF16), pw_ref[...]) + pb_ref[...]
        pool_ref[0, pl.ds(r0, T), :] = (y * ps_ref[...]).astype(BF16)
        return 0

    lax.fori_loop(0, N_TILES, out_step, 0)


def _lru_pool(l, rest, conv_w, conv_b, gate_w, gate_b, lam, pool_w, pool_b, pool_scale):
    seq = lambda w: pl.BlockSpec((1, LALL, w), lambda b: (b, 0, 0))
    return pl.pallas_call(
        _lru_pool_kernel,
        grid=(BATCH,),
        in_specs=[
            seq(REST_W),
            _layer_spec(l, (CONV_W, LRU_W)), _layer_spec(l, (1, LRU_W)),
            _layer_spec(l, (LRU_W, 4 * LRU_W)), _layer_spec(l, (1, 4 * LRU_W)), _layer_spec(l, (2, LRU_W)),
            _layer_spec(l, (POOL_W, POOL_W)), _layer_spec(l, (1, POOL_W)), _layer_spec(l, (1, POOL_W)),
        ],
        out_specs=[seq(LRU_W), seq(POOL_W)],
        out_shape=[jax.ShapeDtypeStruct((BATCH, LALL, LRU_W), BF16),
                   jax.ShapeDtypeStruct((BATCH, LALL, POOL_W), BF16)],
        scratch_shapes=[
            pltpu.VMEM((LALL, LRU_W), F32), pltpu.VMEM((LALL, LRU_W), F32),
            pltpu.VMEM((2, TOKEN_TILE, LRU_W), F32), pltpu.VMEM((2, TOKEN_TILE, LRU_W), F32),
            pltpu.VMEM((LALL, LRU_W), F32),
        ],
        compiler_params=pltpu.CompilerParams(
            dimension_semantics=("arbitrary",),
            vmem_limit_bytes=_vmem_limit(56 * 1024 * 1024)),
        name="lru_pool",
    )(rest, conv_w, conv_b, gate_w, gate_b, lam, pool_w, pool_b, pool_scale)


def _out_ffn_kernel(*refs, n_sub):
    T = TOKEN_TILE
    att, rec, pool = refs[0:n_sub], refs[n_sub:2 * n_sub], refs[2 * n_sub:3 * n_sub]
    x_ref, mod_ref, g_ref, wo_ref, wi_ref, wf_ref, o_ref = refs[3 * n_sub:]
    mod = mod_ref[0]
    ys = [_dot(att[k][0], wo_ref[0:ATTN_W])
          + _dot(rec[k][0], wo_ref[ATTN_W:ATTN_W + LRU_W])
          + _dot(pool[k][0], wo_ref[ATTN_W + LRU_W:MIX_W]) for k in range(n_sub)]
    for k in range(n_sub):
        rows = slice(k * T, (k + 1) * T)
        x = x_ref[0, rows] + mod[2:3] * (_rms(ys[k]) * g_ref[1:2])
        h = ((_rms(x) * g_ref[2:3]) * (1.0 + mod[4:5]) + mod[3:4]).astype(BF16)
        gate_up = lambda lo, n: (_dot(h, wi_ref[:, lo:lo + n]),
                                 _dot(h, wi_ref[:, FFN_HIDDEN + lo:FFN_HIDDEN + lo + n]))
        gu = gate_up(*FFN_CHUNKS[0])
        f = None
        for i, (lo, n) in enumerate(FFN_CHUNKS):
            gu_next = gate_up(*FFN_CHUNKS[i + 1]) if i + 1 < len(FFN_CHUNKS) else None
            g, u = gu
            part = _dot((g * jax.nn.sigmoid(g) * u).astype(BF16), wf_ref[lo:lo + n])
            f = part if f is None else f + part
            gu = gu_next
        o_ref[0, rows] = x + mod[5:6] * (_rms(f) * g_ref[3:4])


def _out_ffn(l, att, rec, pool, x_src, mod, norm_g, w_out, w_ffn_in, w_ffn_out, *, context):
    T = TOKEN_TILE
    n_sub = 1 if context else FFN_TILES_PER_STEP
    n_steps = 1 if context else (N_TILES - 1) // n_sub
    tile = lambda k, skip: (lambda b, t: (b, 0 if context else n_sub * t + k + skip, 0))
    weight = lambda shape: _layer_spec(l, shape, pipeline_mode=pl.Buffered(1))
    subs = range(n_sub)
    return pl.pallas_call(
        functools.partial(_out_ffn_kernel, n_sub=n_sub),
        grid=(BATCH, n_steps),
        in_specs=[
            *[pl.BlockSpec((1, T, ATTN_W), tile(k, 0)) for k in subs],
            *[pl.BlockSpec((1, T, LRU_W), tile(k, 1)) for k in subs],
            *[pl.BlockSpec((1, T, POOL_W), tile(k, 1)) for k in subs],
            pl.BlockSpec((1, n_sub * T, D_MODEL), lambda b, t: (b, t, 0)),
            pl.BlockSpec((None, 1, 6, D_MODEL), lambda b, t: (l, BATCH if context else b, 0, 0)),
            weight((4, D_MODEL)),
            weight((MIX_W, D_MODEL)),
            weight((D_MODEL, 2 * FFN_HIDDEN)),
            weight((FFN_HIDDEN, D_MODEL)),
        ],
        out_specs=pl.BlockSpec((1, n_sub * T, D_MODEL), lambda b, t: (b, t, 0)),
        out_shape=jax.ShapeDtypeStruct((BATCH, n_steps * n_sub * T, D_MODEL), F32),
        compiler_params=pltpu.CompilerParams(
            dimension_semantics=("arbitrary", "arbitrary"),
            vmem_limit_bytes=_vmem_limit(52 * 1024 * 1024)),
        name="out_ffn_ctx" if context else "out_ffn",
    )(*([att] * n_sub), *([rec] * n_sub), *([pool] * n_sub), x_src, mod, norm_g,
      w_out, w_ffn_in, w_ffn_out)


def _position_tables():
    half = ROPE_HALF
    freq = (ROPE_THETA ** (-np.arange(half, dtype=np.float32) / half)).astype(np.float32)
    p = np.arange(SEQ)
    ang_row = (p // GRID_W).astype(np.float32)[:, None] * freq
    ang_col = (p % GRID_W).astype(np.float32)[:, None] * freq
    ang = np.concatenate([ang_row, ang_row, ang_col, ang_col], axis=1)
    first = np.tile(np.arange(HEAD_DIM) % (2 * half) < half, (SEQ, 1))
    cos = np.cos(ang)
    sin_a = np.where(first, -np.sin(ang), 0.0)
    sin_b = np.where(first, 0.0, np.sin(ang))
    pad = lambda a, v: np.concatenate([np.full((CTX_LEN, HEAD_DIM), v), a], axis=0)
    two = lambda a: np.tile(a, (1, V7X_LANES // HEAD_DIM)).astype(np.float32)
    return two(pad(cos, 1.0)), two(pad(sin_a, 0.0)), two(pad(sin_b, 0.0))


def _head_mean_matrix(width):
    blk = np.kron(np.eye(width // HEAD_DIM), np.full((HEAD_DIM, HEAD_DIM), 1.0 / HEAD_DIM))
    return jnp.asarray(blk, dtype=BF16)


def _block_diag(w):
    n, c, d = w.shape[-3:]
    on_diag = jnp.eye(n, dtype=bool)[:, None, :, None]
    out = jnp.where(on_diag, w[..., :, :, None, :], 0.0)
    return out.reshape(w.shape[:-3] + (n * c, n * d))


def kernel(x, c, ctx, c_ctx, w_mod, b_mod, norm_g, w_in, q_norm_g, k_norm_g, lru_conv_w, lru_conv_b,
           lru_gate_w, lru_gate_b, lru_lambda, pool_w, pool_b, pool_scale, w_out, w_ffn_in, w_ffn_out):
    assert x.shape == (BATCH, SEQ, D_MODEL) and ctx.shape == (BATCH, CTX_LEN, D_MODEL)
    cos, sin_a, sin_b = (jnp.asarray(a) for a in _position_tables())
    cos_t, sin_t = cos[:, 0:HEAD_DIM].T, (sin_a + sin_b)[:, 0:HEAD_DIM].T
    consts = (_head_mean_matrix(KV_W), cos, sin_a, sin_b, cos_t, sin_t)

    c_all = jnp.zeros((MOD_ROWS, D_MODEL), F32).at[0:BATCH].set(c).at[BATCH].set(c_ctx)
    mod = _modulation(c_all, w_mod, b_mod).reshape(DEPTH, MOD_ROWS, 6, D_MODEL)

    w_in_b, w_out_b = w_in.astype(BF16), w_out.astype(BF16)
    w_ffn_in_b, w_ffn_out_b = w_ffn_in.astype(BF16), w_ffn_out.astype(BF16)
    gq = jnp.broadcast_to(q_norm_g[:, :, None], (DEPTH, HEAD_DIM, TOKEN_TILE))
    gk = jnp.tile(k_norm_g, (1, N_KV_HEADS))[:, None, :]
    gate_w = _block_diag(lru_gate_w)
    gate_w = gate_w.transpose(0, 3, 1, 2, 4).reshape(DEPTH, LRU_W, 4 * LRU_W).astype(BF16)
    gate_b = lru_gate_b.reshape(DEPTH, 1, 4 * LRU_W)
    pool_w_b = _block_diag(pool_w).astype(BF16)
    conv_b, pool_b3, pool_s3 = lru_conv_b[:, None, :], pool_b[:, None, :], pool_scale[:, None, :]

    for l in range(DEPTH):
        last = l == DEPTH - 1
        q, k, vt, rest = _in_projection(l, ctx, x, mod, norm_g, w_in_b, consts, gq, gk)
        att = _attention(q, k, vt, context=False)
        rec, pool = _lru_pool(l, rest, lru_conv_w, conv_b, gate_w, gate_b, lru_lambda,
                              pool_w_b, pool_b3, pool_s3)
        weights = (mod, norm_g, w_out_b, w_ffn_in_b, w_ffn_out_b)
        x_next = _out_ffn(l, att, rec, pool, x, *weights, context=False)
        if not last:
            att_ctx = _attention(q, k, vt, context=True)
            ctx = _out_ffn(l, att_ctx, rec, pool, ctx, *weights, context=True)
        x = x_next
    return x
```

```python
import functools

import numpy as np
import jax
import jax.numpy as jnp
from jax import lax
from jax.experimental import pallas as pl
from jax.experimental.pallas import tpu as pltpu

D_MODEL = 1024
BATCH = 4
SEQ = 4096
DEPTH = 2
GRID_W = 64
CTX_LEN = 256
LALL = CTX_LEN + SEQ

N_Q_HEADS = 8
N_KV_HEADS = 2
HEAD_DIM = 64
Q_GROUP = N_Q_HEADS // N_KV_HEADS
ATTN_W = N_Q_HEADS * HEAD_DIM
KV_W = N_KV_HEADS * HEAD_DIM
ROPE_THETA = 10000.0
LRU_W = D_MODEL // 4
LRU_BLOCKS = 4
LRU_BW = LRU_W // LRU_BLOCKS
CONV_W = 4
LRU_C = 8.0
POOL_W = D_MODEL // 4
POOL_GROUPS = 4
POOL_GW = POOL_W // POOL_GROUPS
POOL_WINDOWS = (2, 4, 8, 16)
MIX_W = ATTN_W + LRU_W + POOL_W
IN_W = ATTN_W + 2 * KV_W + 2 * LRU_W + POOL_W
REST_W = 2 * LRU_W + POOL_W
FFN_HIDDEN = -(-8 * D_MODEL // 768) * 256
RMS_EPS = 1e-6

V7X_LANES = 128
V7X_SUBLANES = 8
V7X_VMEM_BYTES = 64 * 1024 * 1024

TOKEN_TILE = CTX_LEN
N_TILES = LALL // TOKEN_TILE
MOD_ROWS = 8
MOD_TILE_N = 1536
ROPE_HALF = HEAD_DIM // 4
GROUPS_PER_TILE = TOKEN_TILE // V7X_SUBLANES
ONES_ROWS = 16
ATTN_Q_TILES = 4
KEY_CHUNK = TOKEN_TILE
SCORE_SLOTS = 3
FFN_TILES_PER_STEP = 4
FFN_CHUNK = 512
FFN_CHUNKS = [(lo, min(FFN_CHUNK, FFN_HIDDEN - lo)) for lo in range(0, FFN_HIDDEN, FFN_CHUNK)]
Q_SCALE = HEAD_DIM ** -0.5 * float(np.log2(np.e))

F32 = jnp.float32
BF16 = jnp.bfloat16


def _vmem_limit(nbytes):
    return int(min(max(nbytes, 16 * 1024 * 1024), V7X_VMEM_BYTES - 6 * 1024 * 1024))


def _rms(x):
    return x * lax.rsqrt(jnp.mean(x * x, axis=-1, keepdims=True) + RMS_EPS)


def _dot(a, b):
    return jnp.dot(a, b, preferred_element_type=F32)


def _layer_spec(l, shape, **kw):
    return pl.BlockSpec((None,) + shape, lambda *_: (l,) + (0,) * len(shape), **kw)


def _mod_row(b, t):
    return jnp.where(t == 0, BATCH, b)


def _token_specs(width):
    T = TOKEN_TILE
    ctx_spec = pl.BlockSpec((1, T, width), lambda b, t: (b, 0, 0))
    next_spec = pl.BlockSpec((1, T, width), lambda b, t: (b, jnp.minimum(t, N_TILES - 2), 0))
    return ctx_spec, next_spec


def _mod_kernel(c_ref, w_ref, b_ref, o_ref):
    c = c_ref[...]
    h = (c * jax.nn.sigmoid(c)).astype(BF16)
    o_ref[0] = _dot(h, w_ref[0].astype(BF16)) + b_ref[0]


def _modulation(c_all, w_mod, b_mod):
    n = 6 * D_MODEL
    return pl.pallas_call(
        _mod_kernel,
        grid=(DEPTH, n // MOD_TILE_N),
        in_specs=[
            pl.BlockSpec((MOD_ROWS, D_MODEL), lambda l, j: (0, 0)),
            pl.BlockSpec((1, D_MODEL, MOD_TILE_N), lambda l, j: (l, 0, j)),
            pl.BlockSpec((1, 1, MOD_TILE_N), lambda l, j: (l, 0, j)),
        ],
        out_specs=pl.BlockSpec((1, MOD_ROWS, MOD_TILE_N), lambda l, j: (l, 0, j)),
        out_shape=jax.ShapeDtypeStruct((DEPTH, MOD_ROWS, n), F32),
        compiler_params=pltpu.CompilerParams(
            dimension_semantics=("arbitrary", "arbitrary"),
            vmem_limit_bytes=_vmem_limit(4 * D_MODEL * MOD_TILE_N * 4)),
        name="modulation",
    )(c_all, w_mod, b_mod.reshape(DEPTH, 1, n))


def _head_norm_rope(z, s_ref, g, cos, sin_a, sin_b):
    z2 = z * z
    hi = z2.astype(BF16)
    lo = (z2 - hi.astype(F32)).astype(BF16)
    ms = _dot(hi, s_ref[...]) + _dot(lo, s_ref[...])
    zn = (z * lax.rsqrt(ms + RMS_EPS)) * g
    outs = []
    for c in range(z.shape[1] // V7X_LANES):
        zc = zn[:, c * V7X_LANES:(c + 1) * V7X_LANES]
        up = pltpu.roll(zc, V7X_LANES - ROPE_HALF, 1)
        dn = pltpu.roll(zc, ROPE_HALF, 1)
        outs.append(zc * cos + up * sin_a + dn * sin_b)
    return outs


def _head_norm_rope_t(zt, g, cos_t, sin_t):
    zn = (zt * lax.rsqrt(jnp.mean(zt * zt, axis=0, keepdims=True) + RMS_EPS)) * g
    h = ROPE_HALF
    partner = jnp.concatenate([zn[h:2 * h], zn[0:h], zn[3 * h:4 * h], zn[2 * h:3 * h]], axis=0)
    return zn * cos_t + partner * sin_t


def _inproj_kernel(ctx_ref, xn_ref, mod_ref, modn_ref, g_ref, w_ref, sk_ref, gq_ref, gk_ref,
                   cos_ref, sa_ref, sb_ref, cost_ref, sint_ref, q_ref, k_ref, vt_ref, rest_ref, h_ref):
    def norm_mod(x, mod):
        return ((_rms(x) * g_ref[0:1]) * (1.0 + mod[1:2]) + mod[0:1]).astype(BF16)

    @pl.when(pl.program_id(1) == 0)
    def _():
        h_ref[...] = norm_mod(ctx_ref[0], mod_ref[0])

    h = h_ref[...]
    qkv_w = ATTN_W + 2 * KV_W
    first_w = qkv_w + LRU_W
    y = _dot(h, w_ref[:, 0:first_w])
    (kc,) = _head_norm_rope(y[:, ATTN_W:ATTN_W + KV_W], sk_ref, gk_ref[...],
                            cos_ref[...], sa_ref[...], sb_ref[...])
    rest_ref[0, :, 0:LRU_W] = y[:, qkv_w:first_w]
    rest_ref[0, :, LRU_W:REST_W] = _dot(h, w_ref[:, first_w:IN_W])
    h_ref[...] = norm_mod(xn_ref[0], modn_ref[0])

    for c in range(ATTN_W // V7X_LANES):
        zt = y[:, c * V7X_LANES:(c + 1) * V7X_LANES].T
        for i in range(V7X_LANES // HEAD_DIM):
            qt = _head_norm_rope_t(zt[i * HEAD_DIM:(i + 1) * HEAD_DIM], gq_ref[...],
                                   cost_ref[...], sint_ref[...])
            q_ref[0, 2 * c + i, 0] = (qt * Q_SCALE).astype(BF16)

    kc = kc.astype(BF16)
    k_ref[0, 0] = kc[:, 0:HEAD_DIM]
    k_ref[0, 1] = kc[:, HEAD_DIM:2 * HEAD_DIM]

    vt = y[:, ATTN_W + KV_W:qkv_w].T.astype(BF16)
    vt_ref[0, 0, 0] = vt[0:HEAD_DIM]
    vt_ref[0, 1, 0] = vt[HEAD_DIM:2 * HEAD_DIM]


def _in_projection(l, ctx_src, x_src, mod, norm_g, w_in, consts, gq, gk):
    sk, cos, sin_a, sin_b, cos_t, sin_t = consts
    T = TOKEN_TILE
    full = lambda shape: pl.BlockSpec(shape, lambda b, t: (0,) * len(shape))
    tab = pl.BlockSpec((T, V7X_LANES), lambda b, t: (t, 0))
    return pl.pallas_call(
        _inproj_kernel,
        grid=(BATCH, N_TILES),
        in_specs=[
            *_token_specs(D_MODEL),
            pl.BlockSpec((None, 1, 6, D_MODEL), lambda b, t: (l, _mod_row(b, t), 0, 0)),
            pl.BlockSpec((None, 1, 6, D_MODEL), lambda b, t: (l, b, 0, 0)),
            _layer_spec(l, (4, D_MODEL)),
            _layer_spec(l, (D_MODEL, IN_W)),
            full((KV_W, KV_W)),
            _layer_spec(l, (HEAD_DIM, T)),
            _layer_spec(l, (1, KV_W)),
            tab, tab, tab,
            pl.BlockSpec((HEAD_DIM, T), lambda b, t: (0, t)),
            pl.BlockSpec((HEAD_DIM, T), lambda b, t: (0, t)),
        ],
        out_specs=[
            pl.BlockSpec((1, N_Q_HEADS, 1, HEAD_DIM, T), lambda b, t: (b, 0, t, 0, 0)),
            pl.BlockSpec((1, N_KV_HEADS, T, HEAD_DIM), lambda b, t: (b, 0, t, 0)),
            pl.BlockSpec((1, N_KV_HEADS, 1, HEAD_DIM, T), lambda b, t: (b, 0, t, 0, 0)),
            pl.BlockSpec((1, T, REST_W), lambda b, t: (b, t, 0)),
        ],
        out_shape=[
            jax.ShapeDtypeStruct((BATCH, N_Q_HEADS, N_TILES, HEAD_DIM, T), BF16),
            jax.ShapeDtypeStruct((BATCH, N_KV_HEADS, LALL, HEAD_DIM), BF16),
            jax.ShapeDtypeStruct((BATCH, N_KV_HEADS, N_TILES, HEAD_DIM, T), BF16),
            jax.ShapeDtypeStruct((BATCH, LALL, REST_W), F32),
        ],
        scratch_shapes=[pltpu.VMEM((T, D_MODEL), BF16)],
        compiler_params=pltpu.CompilerParams(
            dimension_semantics=("arbitrary", "arbitrary"),
            vmem_limit_bytes=_vmem_limit(40 * 1024 * 1024)),
        name="in_projection",
    )(ctx_src, x_src, mod, mod, norm_g, w_in, sk, gq, gk, cos, sin_a, sin_b, cos_t, sin_t)


def _attn_kernel(*refs, q_tiles, n_chunks):
    T = TOKEN_TILE
    q_refs = refs[:q_tiles]
    k_ref, vt_ref, o_ref, s_ref, m_ref, acc_ref = refs[q_tiles:]
    pieces = Q_GROUP // 2

    def scores(n, piece):
        i, c = divmod(n, n_chunks)
        qt = jnp.concatenate([q_refs[i][0, g, 0] for g in (2 * piece, 2 * piece + 1)], axis=1)
        cols = slice(2 * piece * T, 2 * (piece + 1) * T)
        s_ref[n % SCORE_SLOTS, :, cols] = _dot(k_ref[0, 0, c * KEY_CHUNK:(c + 1) * KEY_CHUNK, :], qt)

    def update(n, piece):
        i, c = divmod(n, n_chunks)
        tile, off = divmod(c * KEY_CHUNK, T)
        vt1 = jnp.concatenate([vt_ref[0, 0, tile][:, off:off + KEY_CHUNK],
                               jnp.ones((ONES_ROWS, KEY_CHUNK), BF16)], axis=0)
        for g in (2 * piece, 2 * piece + 1):
            cols = slice(g * T, (g + 1) * T)
            s = s_ref[n % SCORE_SLOTS, :, cols]
            m = m_ref[i, :, cols]
            m_new = jnp.maximum(m, jnp.max(s, axis=0, keepdims=True))
            alpha = jnp.exp2(m - m_new)
            p = jnp.exp2((s - m_new).astype(BF16))
            m_ref[i, :, cols] = m_new
            acc_ref[i, :, cols] = alpha * acc_ref[i, :, cols] + _dot(vt1, p)

    def finish(i):
        o = acc_ref[i, 0:HEAD_DIM] / acc_ref[i, HEAD_DIM:HEAD_DIM + 1]
        o = jnp.concatenate([o[:, g * T:(g + 1) * T] for g in range(Q_GROUP)], axis=0)
        o_ref[0, i * T:(i + 1) * T, :] = o.T.astype(BF16)

    def tile_block(i):
        if i > 0:
            finish(i - 1)
        for n in range(i * n_chunks, (i + 1) * n_chunks):
            for piece in range(pieces):
                if n + SCORE_SLOTS - 1 < q_tiles * n_chunks:
                    scores(n + SCORE_SLOTS - 1, piece)
                update(n, piece)

    m_ref[...] = jnp.full(m_ref.shape, -1e30, F32)
    acc_ref[...] = jnp.zeros(acc_ref.shape, F32)
    for n in range(min(SCORE_SLOTS - 1, n_chunks)):
        for piece in range(pieces):
            scores(n, piece)
    for i in range(q_tiles):
        pl.when(pl.program_id(2) >= -i)(functools.partial(tile_block, i))
    finish(q_tiles - 1)


def _attention(q, k, vt, *, context):
    T = TOKEN_TILE
    q_tiles = 1 if context else ATTN_Q_TILES
    n_steps = 1 if context else (N_TILES - 1) // q_tiles
    n_cols = Q_GROUP * T
    q_spec = lambda i: pl.BlockSpec(
        (1, Q_GROUP, 1, HEAD_DIM, T), lambda b, h, t: (b, h, 0 if context else q_tiles * t + i + 1, 0, 0))
    return pl.pallas_call(
        functools.partial(_attn_kernel, q_tiles=q_tiles, n_chunks=(CTX_LEN if context else LALL) // KEY_CHUNK),
        grid=(BATCH, N_KV_HEADS, n_steps),
        in_specs=[
            *[q_spec(i) for i in range(q_tiles)],
            pl.BlockSpec((1, 1, LALL, HEAD_DIM), lambda b, h, t: (b, h, 0, 0)),
            pl.BlockSpec((1, 1, N_TILES, HEAD_DIM, T), lambda b, h, t: (b, h, 0, 0, 0)),
        ],
        out_specs=pl.BlockSpec((1, q_tiles * T, Q_GROUP * HEAD_DIM), lambda b, h, t: (b, t, h)),
        out_shape=jax.ShapeDtypeStruct((BATCH, n_steps * q_tiles * T, ATTN_W), BF16),
        scratch_shapes=[
            pltpu.VMEM((SCORE_SLOTS, KEY_CHUNK, n_cols), F32),
            pltpu.VMEM((q_tiles, 1, n_cols), F32),
            pltpu.VMEM((q_tiles, HEAD_DIM + ONES_ROWS, n_cols), F32),
        ],
        compiler_params=pltpu.CompilerParams(
            dimension_semantics=("arbitrary", "arbitrary", "arbitrary"),
            vmem_limit_bytes=_vmem_limit(40 * 1024 * 1024)),
        name="attention_ctx" if context else "attention",
    )(*([q] * q_tiles), k, vt)


def _softplus(z):
    return jnp.maximum(z, 0.0) + jnp.log1p(jnp.exp(-jnp.abs(z)))


def _with_halo(ref, col, c):
    T = TOKEN_TILE
    r0 = pl.multiple_of(c * T, T)
    main = ref[0, pl.ds(r0, T), col:col + LRU_W]
    p0 = pl.multiple_of(jnp.maximum(r0 - V7X_SUBLANES, 0), V7X_SUBLANES)
    n0 = pl.multiple_of(jnp.minimum(r0 + T, LALL - V7X_SUBLANES), V7X_SUBLANES)
    prev = ref[0, pl.ds(p0, V7X_SUBLANES), col:col + LRU_W]
    nxt = ref[0, pl.ds(n0, V7X_SUBLANES), col:col + LRU_W]
    prev = jnp.where(c >= 2, prev, 0.0)
    nxt = jnp.where((c >= 1) & (c <= N_TILES - 2), nxt, 0.0)
    return main, jnp.concatenate([prev, main, nxt], axis=0)


def _shift_rows(ext, k):
    n = ext.shape[0]
    if k == 0:
        return ext[V7X_SUBLANES:V7X_SUBLANES + TOKEN_TILE]
    return pltpu.roll(ext, (-k) % n, 0)[V7X_SUBLANES:V7X_SUBLANES + TOKEN_TILE]


def _lru_pool_kernel(rest_ref, cw_ref, cb_ref, gw_ref, gb_ref, lam_ref, pw_ref, pb_ref, ps_ref,
                     rec_ref, pool_ref, hf_ref, hb_ref, a_ref, u_ref, conv_ref):
    T = TOKEN_TILE
    S = V7X_SUBLANES
    grouped = (GROUPS_PER_TILE, S, LRU_W)
    row_in_group = lax.broadcasted_iota(jnp.int32, grouped, 1)

    def conv_step(c, _):
        main, ext = _with_halo(rest_ref, 0, c)
        conv_ref[pl.ds(pl.multiple_of(c * T, T), T), :] = cb_ref[...] + (
            cw_ref[0:1] * _shift_rows(ext, -2) + cw_ref[1:2] * _shift_rows(ext, -1)
            + cw_ref[2:3] * main + cw_ref[3:4] * _shift_rows(ext, 1))
        return 0

    lax.fori_loop(0, N_TILES, conv_step, 0)

    def local_scan(c, d):
        u = conv_ref[pl.ds(pl.multiple_of(c * T, T), T), :]
        g = _dot(u.astype(BF16), gw_ref[:, d * 2 * LRU_W:(d + 1) * 2 * LRU_W])
        g = jax.nn.sigmoid(g + gb_ref[:, d * 2 * LRU_W:(d + 1) * 2 * LRU_W])
        log_a = (-LRU_C * g[:, 0:LRU_W]) * _softplus(-lam_ref[d:d + 1])
        a = jnp.exp(log_a)
        neg_expm1 = (1.0 - a) * (1.0 + a)
        root = jnp.where(neg_expm1 > 0.0, neg_expm1 * lax.rsqrt(neg_expm1), 0.0)
        v = root * (g[:, LRU_W:2 * LRU_W] * u)
        a, v = a.reshape(grouped), v.reshape(grouped)
        for step in (1, 2, 4):
            if d == 0:
                a_s, v_s = pltpu.roll(a, step, 1), pltpu.roll(v, step, 1)
                keep = row_in_group >= step
            else:
                a_s, v_s = pltpu.roll(a, S - step, 1), pltpu.roll(v, S - step, 1)
                keep = row_in_group < S - step
            v = jnp.where(keep, a * v_s + v, v)
            a = jnp.where(keep, a * a_s, a)
        a_ref[d] = a.reshape(T, LRU_W)
        u_ref[d] = v.reshape(T, LRU_W)

    def tile_step(i, carry):
        h_f, h_b = carry
        c_f = i
        c_b = jnp.where(i == 0, 0, N_TILES - i)
        local_scan(c_f, 0)
        local_scan(c_b, 1)
        r_f = pl.multiple_of(c_f * T, T)
        r_b = pl.multiple_of(c_b * T, T)
        for gi in range(GROUPS_PER_TILE):
            lo = gi * S
            a, v = a_ref[0, lo:lo + S], u_ref[0, lo:lo + S]
            hf_ref[pl.ds(r_f + lo, S), :] = v + a * h_f
            h_f = (jnp.broadcast_to(v[S - 1:S], (S, LRU_W))
                   + jnp.broadcast_to(a[S - 1:S], (S, LRU_W)) * h_f)
            lo = (GROUPS_PER_TILE - 1 - gi) * S
            a, v = a_ref[1, lo:lo + S], u_ref[1, lo:lo + S]
            hb_ref[pl.ds(r_b + lo, S), :] = v + a * h_b
            h_b = (jnp.broadcast_to(v[0:1], (S, LRU_W))
                   + jnp.broadcast_to(a[0:1], (S, LRU_W)) * h_b)
        return h_f, h_b

    zeros = jnp.zeros((S, LRU_W), F32)
    lax.fori_loop(0, N_TILES, tile_step, (zeros, zeros))

    lane = lax.broadcasted_iota(jnp.int32, (T, POOL_W), 1)
    half_win = jnp.where(lane < POOL_GW, POOL_WINDOWS[0] // 2,
                         jnp.where(lane < 2 * POOL_GW, POOL_WINDOWS[1] // 2,
                                   jnp.where(lane < 3 * POOL_GW, POOL_WINDOWS[2] // 2,
                                             POOL_WINDOWS[3] // 2)))
    row = lax.broadcasted_iota(jnp.int32, (T, POOL_W), 0)

    def out_step(c, _):
        r0 = pl.multiple_of(c * T, T)
        gate = rest_ref[0, pl.ds(r0, T), LRU_W:2 * LRU_W]
        h = hf_ref[pl.ds(r0, T), :] + hb_ref[pl.ds(r0, T), :]
        rec_ref[0, pl.ds(r0, T), :] = (jax.nn.gelu(gate) * h).astype(BF16)

        px, ext = _with_halo(rest_ref, 2 * LRU_W, c)
        p2 = ext + pltpu.roll(ext, 1, 0)
        p4 = p2 + pltpu.roll(p2, 2, 0)
        p8 = p4 + pltpu.roll(p4, 4, 0)
        p16 = p8 + pltpu.roll(p8, 8, 0)
        win = jnp.where(lane < POOL_GW, _shift_rows(p2, 0),
                        jnp.where(lane < 2 * POOL_GW, _shift_rows(p4, 1),
                                  jnp.where(lane < 3 * POOL_GW, _shift_rows(p8, 3),
                                            _shift_rows(p16, 7))))
        seg_len = jnp.where(c == 0, CTX_LEN, SEQ)
        pos = row + jnp.where(c == 0, 0, r0 - CTX_LEN)
        cnt = jnp.minimum(pos + half_win, seg_len) - jnp.maximum(pos - half_win, 0)
        d = win / cnt.astype(F32) - px
        y = _dot(d.astype(BF16), pw_ref[...]) + pb_ref[...]
        pool_ref[0, pl.ds(r0, T), :] = (y * ps_ref[...]).astype(BF16)
        return 0

    lax.fori_loop(0, N_TILES, out_step, 0)


def _lru_pool(l, rest, conv_w, conv_b, gate_w, gate_b, lam, pool_w, pool_b, pool_scale):
    seq = lambda w: pl.BlockSpec((1, LALL, w), lambda b: (b, 0, 0))
    return pl.pallas_call(
        _lru_pool_kernel,
        grid=(BATCH,),
        in_specs=[
            seq(REST_W),
            _layer_spec(l, (CONV_W, LRU_W)), _layer_spec(l, (1, LRU_W)),
            _layer_spec(l, (LRU_W, 4 * LRU_W)), _layer_spec(l, (1, 4 * LRU_W)), _layer_spec(l, (2, LRU_W)),
            _layer_spec(l, (POOL_W, POOL_W)), _layer_spec(l, (1, POOL_W)), _layer_spec(l, (1, POOL_W)),
        ],
        out_specs=[seq(LRU_W), seq(POOL_W)],
        out_shape=[jax.ShapeDtypeStruct((BATCH, LALL, LRU_W), BF16),
                   jax.ShapeDtypeStruct((BATCH, LALL, POOL_W), BF16)],
        scratch_shapes=[
            pltpu.VMEM((LALL, LRU_W), F32), pltpu.VMEM((LALL, LRU_W), F32),
            pltpu.VMEM((2, TOKEN_TILE, LRU_W), F32), pltpu.VMEM((2, TOKEN_TILE, LRU_W), F32),
            pltpu.VMEM((LALL, LRU_W), F32),
        ],
        compiler_params=pltpu.CompilerParams(
            dimension_semantics=("arbitrary",),
            vmem_limit_bytes=_vmem_limit(56 * 1024 * 1024)),
        name="lru_pool",
    )(rest, conv_w, conv_b, gate_w, gate_b, lam, pool_w, pool_b, pool_scale)


def _out_ffn_kernel(*refs, n_sub):
    T = TOKEN_TILE
    att, rec, pool = refs[0:n_sub], refs[n_sub:2 * n_sub], refs[2 * n_sub:3 * n_sub]
    x_ref, mod_ref, g_ref, wo_ref, wi_ref, wf_ref, o_ref = refs[3 * n_sub:]
    mod = mod_ref[0]
    ys = [_dot(att[k][0], wo_ref[0:ATTN_W])
          + _dot(rec[k][0], wo_ref[ATTN_W:ATTN_W + LRU_W])
          + _dot(pool[k][0], wo_ref[ATTN_W + LRU_W:MIX_W]) for k in range(n_sub)]
    for k in range(n_sub):
        rows = slice(k * T, (k + 1) * T)
        x = x_ref[0, rows] + mod[2:3] * (_rms(ys[k]) * g_ref[1:2])
        h = ((_rms(x) * g_ref[2:3]) * (1.0 + mod[4:5]) + mod[3:4]).astype(BF16)
        gate_up = lambda lo, n: (_dot(h, wi_ref[:, lo:lo + n]),
                                 _dot(h, wi_ref[:, FFN_HIDDEN + lo:FFN_HIDDEN + lo + n]))
        gu = gate_up(*FFN_CHUNKS[0])
        f = None
        for i, (lo, n) in enumerate(FFN_CHUNKS):
            gu_next = gate_up(*FFN_CHUNKS[i + 1]) if i + 1 < len(FFN_CHUNKS) else None
            g, u = gu
            part = _dot((g * jax.nn.sigmoid(g) * u).astype(BF16), wf_ref[lo:lo + n])
            f = part if f is None else f + part
            gu = gu_next
        o_ref[0, rows] = x + mod[5:6] * (_rms(f) * g_ref[3:4])


def _out_ffn(l, att, rec, pool, x_src, mod, norm_g, w_out, w_ffn_in, w_ffn_out, *, context):
    T = TOKEN_TILE
    n_sub = 1 if context else FFN_TILES_PER_STEP
    n_steps = 1 if context else (N_TILES - 1) // n_sub
    tile = lambda k, skip: (lambda b, t: (b, 0 if context else n_sub * t + k + skip, 0))
    weight = lambda shape: _layer_spec(l, shape, pipeline_mode=pl.Buffered(1))
    subs = range(n_sub)
    return pl.pallas_call(
        functools.partial(_out_ffn_kernel, n_sub=n_sub),
        grid=(BATCH, n_steps),
        in_specs=[
            *[pl.BlockSpec((1, T, ATTN_W), tile(k, 0)) for k in subs],
            *[pl.BlockSpec((1, T, LRU_W), tile(k, 1)) for k in subs],
            *[pl.BlockSpec((1, T, POOL_W), tile(k, 1)) for k in subs],
            pl.BlockSpec((1, n_sub * T, D_MODEL), lambda b, t: (b, t, 0)),
            pl.BlockSpec((None, 1, 6, D_MODEL), lambda b, t: (l, BATCH if context else b, 0, 0)),
            weight((4, D_MODEL)),
            weight((MIX_W, D_MODEL)),
            weight((D_MODEL, 2 * FFN_HIDDEN)),
            weight((FFN_HIDDEN, D_MODEL)),
        ],
        out_specs=pl.BlockSpec((1, n_sub * T, D_MODEL), lambda b, t: (b, t, 0)),
        out_shape=jax.ShapeDtypeStruct((BATCH, n_steps * n_sub * T, D_MODEL), F32),
        compiler_params=pltpu.CompilerParams(
            dimension_semantics=("arbitrary", "arbitrary"),
            vmem_limit_bytes=_vmem_limit(52 * 1024 * 1024)),
        name="out_ffn_ctx" if context else "out_ffn",
    )(*([att] * n_sub), *([rec] * n_sub), *([pool] * n_sub), x_src, mod, norm_g,
      w_out, w_ffn_in, w_ffn_out)


def _position_tables():
    half = ROPE_HALF
    freq = (ROPE_THETA ** (-np.arange(half, dtype=np.float32) / half)).astype(np.float32)
    p = np.arange(SEQ)
    ang_row = (p // GRID_W).astype(np.float32)[:, None] * freq
    ang_col = (p % GRID_W).astype(np.float32)[:, None] * freq
    ang = np.concatenate([ang_row, ang_row, ang_col, ang_col], axis=1)
    first = np.tile(np.arange(HEAD_DIM) % (2 * half) < half, (SEQ, 1))
    cos = np.cos(ang)
    sin_a = np.where(first, -np.sin(ang), 0.0)
    sin_b = np.where(first, 0.0, np.sin(ang))
    pad = lambda a, v: np.concatenate([np.full((CTX_LEN, HEAD_DIM), v), a], axis=0)
    two = lambda a: np.tile(a, (1, V7X_LANES // HEAD_DIM)).astype(np.float32)
    return two(pad(cos, 1.0)), two(pad(sin_a, 0.0)), two(pad(sin_b, 0.0))


def _head_mean_matrix(width):
    blk = np.kron(np.eye(width // HEAD_DIM), np.full((HEAD_DIM, HEAD_DIM), 1.0 / HEAD_DIM))
    return jnp.asarray(blk, dtype=BF16)


def _block_diag(w):
    n, c, d = w.shape[-3:]
    on_diag = jnp.eye(n, dtype=bool)[:, None, :, None]
    out = jnp.where(on_diag, w[..., :, :, None, :], 0.0)
    return out.reshape(w.shape[:-3] + (n * c, n * d))


def kernel(x, c, ctx, c_ctx, w_mod, b_mod, norm_g, w_in, q_norm_g, k_norm_g, lru_conv_w, lru_conv_b,
           lru_gate_w, lru_gate_b, lru_lambda, pool_w, pool_b, pool_scale, w_out, w_ffn_in, w_ffn_out):
    assert x.shape == (BATCH, SEQ, D_MODEL) and ctx.shape == (BATCH, CTX_LEN, D_MODEL)
    cos, sin_a, sin_b = (jnp.asarray(a) for a in _position_tables())
    cos_t, sin_t = cos[:, 0:HEAD_DIM].T, (sin_a + sin_b)[:, 0:HEAD_DIM].T
    consts = (_head_mean_matrix(KV_W), cos, sin_a, sin_b, cos_t, sin_t)

    c_all = jnp.zeros((MOD_ROWS, D_MODEL), F32).at[0:BATCH].set(c).at[BATCH].set(c_ctx)
    mod = _modulation(c_all, w_mod, b_mod).reshape(DEPTH, MOD_ROWS, 6, D_MODEL)

    w_in_b, w_out_b = w_in.astype(BF16), w_out.astype(BF16)
    w_ffn_in_b, w_ffn_out_b = w_ffn_in.astype(BF16), w_ffn_out.astype(BF16)
    gq = jnp.broadcast_to(q_norm_g[:, :, None], (DEPTH, HEAD_DIM, TOKEN_TILE))
    gk = jnp.tile(k_norm_g, (1, N_KV_HEADS))[:, None, :]
    gate_w = _block_diag(lru_gate_w)
    gate_w = gate_w.transpose(0, 3, 1, 2, 4).reshape(DEPTH, LRU_W, 4 * LRU_W).astype(BF16)
    gate_b = lru_gate_b.reshape(DEPTH, 1, 4 * LRU_W)
    pool_w_b = _block_diag(pool_w).astype(BF16)
    conv_b, pool_b3, pool_s3 = lru_conv_b[:, None, :], pool_b[:, None, :], pool_scale[:, None, :]

    for l in range(DEPTH):
        last = l == DEPTH - 1
        q, k, vt, rest = _in_projection(l, ctx, x, mod, norm_g, w_in_b, consts, gq, gk)
        att = _attention(q, k, vt, context=False)
        rec, pool = _lru_pool(l, rest, lru_conv_w, conv_b, gate_w, gate_b, lru_lambda,
                              pool_w_b, pool_b3, pool_s3)
        weights = (mod, norm_g, w_out_b, w_ffn_in_b, w_ffn_out_b)
        x_next = _out_ffn(l, att, rec, pool, x, *weights, context=False)
        if not last:
            att_ctx = _attention(q, k, vt, context=True)
            ctx = _out_ffn(l, att_ctx, rec, pool, ctx, *weights, context=True)
        x = x_next
    return x
```

```python
import functools

import numpy as np
import jax
import jax.numpy as jnp
from jax import lax
from jax.experimental import pallas as pl
from jax.experimental.pallas import tpu as pltpu

D_MODEL = 1024
BATCH = 4
SEQ = 4096
DEPTH = 2
GRID_W = 64
CTX_LEN = 256
LALL = CTX_LEN + SEQ

N_Q_HEADS = 8
N_KV_HEADS = 2
HEAD_DIM = 64
Q_GROUP = N_Q_HEADS // N_KV_HEADS
ATTN_W = N_Q_HEADS * HEAD_DIM
KV_W = N_KV_HEADS * HEAD_DIM
ROPE_THETA = 10000.0
LRU_W = D_MODEL // 4
LRU_BLOCKS = 4
LRU_BW = LRU_W // LRU_BLOCKS
CONV_W = 4
LRU_C = 8.0
POOL_W = D_MODEL // 4
POOL_GROUPS = 4
POOL_GW = POOL_W // POOL_GROUPS
POOL_WINDOWS = (2, 4, 8, 16)
MIX_W = ATTN_W + LRU_W + POOL_W
IN_W = ATTN_W + 2 * KV_W + 2 * LRU_W + POOL_W
REST_W = 2 * LRU_W + POOL_W
FFN_HIDDEN = -(-8 * D_MODEL // 768) * 256
RMS_EPS = 1e-6

V7X_LANES = 128
V7X_SUBLANES = 8
V7X_VMEM_BYTES = 64 * 1024 * 1024

TOKEN_TILE = CTX_LEN
N_TILES = LALL // TOKEN_TILE
MOD_ROWS = 8
MOD_TILE_N = 1536
ROPE_HALF = HEAD_DIM // 4
GROUPS_PER_TILE = TOKEN_TILE // V7X_SUBLANES
ONES_ROWS = 16
ATTN_Q_TILES = 4
KEY_CHUNK = TOKEN_TILE
SCORE_SLOTS = 3
FFN_TILES_PER_STEP = 4
FFN_CHUNK = 512
FFN_CHUNKS = [(lo, min(FFN_CHUNK, FFN_HIDDEN - lo)) for lo in range(0, FFN_HIDDEN, FFN_CHUNK)]
Q_SCALE = HEAD_DIM ** -0.5 * float(np.log2(np.e))

F32 = jnp.float32
BF16 = jnp.bfloat16


def _vmem_limit(nbytes):
    return int(min(max(nbytes, 16 * 1024 * 1024), V7X_VMEM_BYTES - 6 * 1024 * 1024))


def _rms(x):
    return x * lax.rsqrt(jnp.mean(x * x, axis=-1, keepdims=True) + RMS_EPS)


def _dot(a, b):
    return jnp.dot(a, b, preferred_element_type=F32)


def _layer_spec(l, shape, **kw):
    return pl.BlockSpec((None,) + shape, lambda *_: (l,) + (0,) * len(shape), **kw)


def _mod_kernel(c_ref, w_ref, b_ref, o_ref):
    c = c_ref[...]
    h = (c * jax.nn.sigmoid(c)).astype(BF16)
    o_ref[0] = _dot(h, w_ref[0].astype(BF16)) + b_ref[0]


def _modulation(c_all, w_mod, b_mod):
    n = 6 * D_MODEL
    return pl.pallas_call(
        _mod_kernel,
        grid=(DEPTH, n // MOD_TILE_N),
        in_specs=[
            pl.BlockSpec((MOD_ROWS, D_MODEL), lambda l, j: (0, 0)),
            pl.BlockSpec((1, D_MODEL, MOD_TILE_N), lambda l, j: (l, 0, j)),
            pl.BlockSpec((1, 1, MOD_TILE_N), lambda l, j: (l, 0, j)),
        ],
        out_specs=pl.BlockSpec((1, MOD_ROWS, MOD_TILE_N), lambda l, j: (l, 0, j)),
        out_shape=jax.ShapeDtypeStruct((DEPTH, MOD_ROWS, n), F32),
        compiler_params=pltpu.CompilerParams(
            dimension_semantics=("arbitrary", "arbitrary"),
            vmem_limit_bytes=_vmem_limit(4 * D_MODEL * MOD_TILE_N * 4)),
        name="modulation",
    )(c_all, w_mod, b_mod.reshape(DEPTH, 1, n))


def _head_norm_rope(z, s_ref, g, cos, sin_a, sin_b):
    z2 = z * z
    hi = z2.astype(BF16)
    lo = (z2 - hi.astype(F32)).astype(BF16)
    ms = _dot(hi, s_ref[...]) + _dot(lo, s_ref[...])
    zn = (z * lax.rsqrt(ms + RMS_EPS)) * g
    outs = []
    for c in range(z.shape[1] // V7X_LANES):
        zc = zn[:, c * V7X_LANES:(c + 1) * V7X_LANES]
        up = pltpu.roll(zc, V7X_LANES - ROPE_HALF, 1)
        dn = pltpu.roll(zc, ROPE_HALF, 1)
        outs.append(zc * cos + up * sin_a + dn * sin_b)
    return outs


def _head_norm_rope_t(zt, g, cos_t, sin_t):
    zn = (zt * lax.rsqrt(jnp.mean(zt * zt, axis=0, keepdims=True) + RMS_EPS)) * g
    h = ROPE_HALF
    partner = jnp.concatenate([zn[h:2 * h], zn[0:h], zn[3 * h:4 * h], zn[2 * h:3 * h]], axis=0)
    return zn * cos_t + partner * sin_t


def _inproj_kernel(ctx_ref, xn_ref, mod_ref, modn_ref, g_ref, w_ref, sk_ref, gq_ref, gk_ref,
                   cos_ref, sa_ref, sb_ref, cost_ref, sint_ref, q_ref, k_ref, vt_ref, rest_ref, h_ref):
    def norm_mod(x, mod):
        return ((_rms(x) * g_ref[0:1]) * (1.0 + mod[1:2]) + mod[0:1]).astype(BF16)

    @pl.when(pl.program_id(0) == 0)
    def _():
        h_ref[...] = norm_mod(ctx_ref[0], mod_ref[0])

    h = h_ref[...]
    qkv_w = ATTN_W + 2 * KV_W
    first_w = qkv_w + LRU_W
    y = _dot(h, w_ref[:, 0:first_w])
    (kc,) = _head_norm_rope(y[:, ATTN_W:ATTN_W + KV_W], sk_ref, gk_ref[...],
                            cos_ref[...], sa_ref[...], sb_ref[...])
    rest_ref[0, :, 0:LRU_W] = y[:, qkv_w:first_w]
    rest_ref[0, :, LRU_W:REST_W] = _dot(h, w_ref[:, first_w:IN_W])
    h_ref[...] = norm_mod(xn_ref[0], modn_ref[0])

    for c in range(ATTN_W // V7X_LANES):
        zt = y[:, c * V7X_LANES:(c + 1) * V7X_LANES].T
        for i in range(V7X_LANES // HEAD_DIM):
            qt = _head_norm_rope_t(zt[i * HEAD_DIM:(i + 1) * HEAD_DIM], gq_ref[...],
                                   cost_ref[...], sint_ref[...])
            q_ref[0, 2 * c + i, 0] = (qt * Q_SCALE).astype(BF16)

    kc = kc.astype(BF16)
    k_ref[0, 0] = kc[:, 0:HEAD_DIM]
    k_ref[0, 1] = kc[:, HEAD_DIM:2 * HEAD_DIM]

    vt = y[:, ATTN_W + KV_W:qkv_w].T.astype(BF16)
    vt_ref[0, 0, 0] = vt[0:HEAD_DIM]
    vt_ref[0, 1, 0] = vt[HEAD_DIM:2 * HEAD_DIM]


def _in_projection(l, ctx_src, x_src, mod, norm_g, w_in, consts, gq, gk):
    sk, cos, sin_a, sin_b, cos_t, sin_t = consts
    T = TOKEN_TILE
    full = lambda shape: pl.BlockSpec(shape, lambda t, b: (0,) * len(shape))
    tab = pl.BlockSpec((T, V7X_LANES), lambda t, b: (t, 0))
    next_b = lambda t, b: (b + 1) % BATCH
    next_latent = lambda t, b: jnp.clip(t + (b + 1) // BATCH - 1, 0, N_TILES - 2)
    return pl.pallas_call(
        _inproj_kernel,
        grid=(N_TILES, BATCH),
        in_specs=[
            pl.BlockSpec((1, T, D_MODEL), lambda t, b: (b, 0, 0)),
            pl.BlockSpec((1, T, D_MODEL), lambda t, b: (next_b(t, b), next_latent(t, b), 0)),
            pl.BlockSpec((None, 1, 6, D_MODEL), lambda t, b: (l, BATCH, 0, 0)),
            pl.BlockSpec((None, 1, 6, D_MODEL), lambda t, b: (l, next_b(t, b), 0, 0)),
            _layer_spec(l, (4, D_MODEL)),
            _layer_spec(l, (D_MODEL, IN_W)),
            full((KV_W, KV_W)),
            _layer_spec(l, (HEAD_DIM, T)),
            _layer_spec(l, (1, KV_W)),
            tab, tab, tab,
            pl.BlockSpec((HEAD_DIM, T), lambda t, b: (0, t)),
            pl.BlockSpec((HEAD_DIM, T), lambda t, b: (0, t)),
        ],
        out_specs=[
            pl.BlockSpec((1, N_Q_HEADS, 1, HEAD_DIM, T), lambda t, b: (b, 0, t, 0, 0)),
            pl.BlockSpec((1, N_KV_HEADS, T, HEAD_DIM), lambda t, b: (b, 0, t, 0)),
            pl.BlockSpec((1, N_KV_HEADS, 1, HEAD_DIM, T), lambda t, b: (b, 0, t, 0, 0)),
            pl.BlockSpec((1, T, REST_W), lambda t, b: (b, t, 0)),
        ],
        out_shape=[
            jax.ShapeDtypeStruct((BATCH, N_Q_HEADS, N_TILES, HEAD_DIM, T), BF16),
            jax.ShapeDtypeStruct((BATCH, N_KV_HEADS, LALL, HEAD_DIM), BF16),
            jax.ShapeDtypeStruct((BATCH, N_KV_HEADS, N_TILES, HEAD_DIM, T), BF16),
            jax.ShapeDtypeStruct((BATCH, LALL, REST_W), F32),
        ],
        scratch_shapes=[pltpu.VMEM((T, D_MODEL), BF16)],
        compiler_params=pltpu.CompilerParams(
            dimension_semantics=("arbitrary", "arbitrary"),
            vmem_limit_bytes=_vmem_limit(40 * 1024 * 1024)),
        name="in_projection",
    )(ctx_src, x_src, mod, mod, norm_g, w_in, sk, gq, gk, cos, sin_a, sin_b, cos_t, sin_t)


def _attn_kernel(*refs, q_tiles, n_chunks):
    T = TOKEN_TILE
    q_refs = refs[:q_tiles]
    k_ref, vt_ref, o_ref, s_ref, m_ref, acc_ref = refs[q_tiles:]
    pieces = Q_GROUP // 2

    def scores(n, piece):
        i, c = divmod(n, n_chunks)
        qt = jnp.concatenate([q_refs[i][0, g, 0] for g in (2 * piece, 2 * piece + 1)], axis=1)
        cols = slice(2 * piece * T, 2 * (piece + 1) * T)
        s_ref[n % SCORE_SLOTS, :, cols] = _dot(k_ref[0, 0, c * KEY_CHUNK:(c + 1) * KEY_CHUNK, :], qt)

    def update(n, piece):
        i, c = divmod(n, n_chunks)
        tile, off = divmod(c * KEY_CHUNK, T)
        vt1 = jnp.concatenate([vt_ref[0, 0, tile][:, off:off + KEY_CHUNK],
                               jnp.ones((ONES_ROWS, KEY_CHUNK), BF16)], axis=0)
        for g in (2 * piece, 2 * piece + 1):
            cols = slice(g * T, (g + 1) * T)
            s = s_ref[n % SCORE_SLOTS, :, cols]
            m = m_ref[i, :, cols]
            m_new = jnp.maximum(m, jnp.max(s, axis=0, keepdims=True))
            alpha = jnp.exp2(m - m_new)
            p = jnp.exp2((s - m_new).astype(BF16))
            m_ref[i, :, cols] = m_new
            acc_ref[i, :, cols] = alpha * acc_ref[i, :, cols] + _dot(vt1, p)

    def finish(i):
        o = acc_ref[i, 0:HEAD_DIM] / acc_ref[i, HEAD_DIM:HEAD_DIM + 1]
        o = jnp.concatenate([o[:, g * T:(g + 1) * T] for g in range(Q_GROUP)], axis=0)
        o_ref[0, i * T:(i + 1) * T, :] = o.T.astype(BF16)

    def tile_block(i):
        if i > 0:
            finish(i - 1)
        for n in range(i * n_chunks, (i + 1) * n_chunks):
            for piece in range(pieces):
                if n + SCORE_SLOTS - 1 < q_tiles * n_chunks:
                    scores(n + SCORE_SLOTS - 1, piece)
                update(n, piece)

    m_ref[...] = jnp.full(m_ref.shape, -1e30, F32)
    acc_ref[...] = jnp.zeros(acc_ref.shape, F32)
    for n in range(min(SCORE_SLOTS - 1, n_chunks)):
        for piece in range(pieces):
            scores(n, piece)
    for i in range(q_tiles):
        pl.when(pl.program_id(2) >= -i)(functools.partial(tile_block, i))
    finish(q_tiles - 1)


def _attention(q, k, vt, *, context):
    T = TOKEN_TILE
    q_tiles = 1 if context else ATTN_Q_TILES
    n_steps = 1 if context else (N_TILES - 1) // q_tiles
    n_cols = Q_GROUP * T
    q_spec = lambda i: pl.BlockSpec(
        (1, Q_GROUP, 1, HEAD_DIM, T), lambda b, h, t: (b, h, 0 if context else q_tiles * t + i + 1, 0, 0))
    return pl.pallas_call(
        functools.partial(_attn_kernel, q_tiles=q_tiles, n_chunks=(CTX_LEN if context else LALL) // KEY_CHUNK),
        grid=(BATCH, N_KV_HEADS, n_steps),
        in_specs=[
            *[q_spec(i) for i in range(q_tiles)],
            pl.BlockSpec((1, 1, LALL, HEAD_DIM), lambda b, h, t: (b, h, 0, 0)),
            pl.BlockSpec((1, 1, N_TILES, HEAD_DIM, T), lambda b, h, t: (b, h, 0, 0, 0)),
        ],
        out_specs=pl.BlockSpec((1, q_tiles * T, Q_GROUP * HEAD_DIM), lambda b, h, t: (b, t, h)),
        out_shape=jax.ShapeDtypeStruct((BATCH, n_steps * q_tiles * T, ATTN_W), BF16),
        scratch_shapes=[
            pltpu.VMEM((SCORE_SLOTS, KEY_CHUNK, n_cols), F32),
            pltpu.VMEM((q_tiles, 1, n_cols), F32),
            pltpu.VMEM((q_tiles, HEAD_DIM + ONES_ROWS, n_cols), F32),
        ],
        compiler_params=pltpu.CompilerParams(
            dimension_semantics=("arbitrary", "arbitrary", "arbitrary"),
            vmem_limit_bytes=_vmem_limit(40 * 1024 * 1024)),
        name="attention_ctx" if context else "attention",
    )(*([q] * q_tiles), k, vt)


def _softplus(z):
    return jnp.maximum(z, 0.0) + jnp.log1p(jnp.exp(-jnp.abs(z)))


def _with_halo(ref, col, c):
    T = TOKEN_TILE
    r0 = pl.multiple_of(c * T, T)
    main = ref[0, pl.ds(r0, T), col:col + LRU_W]
    p0 = pl.multiple_of(jnp.maximum(r0 - V7X_SUBLANES, 0), V7X_SUBLANES)
    n0 = pl.multiple_of(jnp.minimum(r0 + T, LALL - V7X_SUBLANES), V7X_SUBLANES)
    prev = ref[0, pl.ds(p0, V7X_SUBLANES), col:col + LRU_W]
    nxt = ref[0, pl.ds(n0, V7X_SUBLANES), col:col + LRU_W]
    prev = jnp.where(c >= 2, prev, 0.0)
    nxt = jnp.where((c >= 1) & (c <= N_TILES - 2), nxt, 0.0)
    return main, jnp.concatenate([prev, main, nxt], axis=0)


def _shift_rows(ext, k):
    n = ext.shape[0]
    if k == 0:
        return ext[V7X_SUBLANES:V7X_SUBLANES + TOKEN_TILE]
    return pltpu.roll(ext, (-k) % n, 0)[V7X_SUBLANES:V7X_SUBLANES + TOKEN_TILE]


def _lru_pool_kernel(rest_ref, cw_ref, cb_ref, gw_ref, gb_ref, lam_ref, pw_ref, pb_ref, ps_ref,
                     rec_ref, pool_ref, hf_ref, hb_ref, a_ref, u_ref, conv_ref):
    T = TOKEN_TILE
    S = V7X_SUBLANES
    grouped = (GROUPS_PER_TILE, S, LRU_W)
    row_in_group = lax.broadcasted_iota(jnp.int32, grouped, 1)

    def conv_step(c, _):
        main, ext = _with_halo(rest_ref, 0, c)
        conv_ref[pl.ds(pl.multiple_of(c * T, T), T), :] = cb_ref[...] + (
            cw_ref[0:1] * _shift_rows(ext, -2) + cw_ref[1:2] * _shift_rows(ext, -1)
            + cw_ref[2:3] * main + cw_ref[3:4] * _shift_rows(ext, 1))
        return 0

    lax.fori_loop(0, N_TILES, conv_step, 0)

    def local_scan(c, d):
        u = conv_ref[pl.ds(pl.multiple_of(c * T, T), T), :]
        g = _dot(u.astype(BF16), gw_ref[:, d * 2 * LRU_W:(d + 1) * 2 * LRU_W])
        g = jax.nn.sigmoid(g + gb_ref[:, d * 2 * LRU_W:(d + 1) * 2 * LRU_W])
        log_a = (-LRU_C * g[:, 0:LRU_W]) * _softplus(-lam_ref[d:d + 1])
        a = jnp.exp(log_a)
        neg_expm1 = (1.0 - a) * (1.0 + a)
        root = jnp.where(neg_expm1 > 0.0, neg_expm1 * lax.rsqrt(neg_expm1), 0.0)
        v = root * (g[:, LRU_W:2 * LRU_W] * u)
        a, v = a.reshape(grouped), v.reshape(grouped)
        for step in (1, 2, 4):
            if d == 0:
                a_s, v_s = pltpu.roll(a, step, 1), pltpu.roll(v, step, 1)
                keep = row_in_group >= step
            else:
                a_s, v_s = pltpu.roll(a, S - step, 1), pltpu.roll(v, S - step, 1)
                keep = row_in_group < S - step
            v = jnp.where(keep, a * v_s + v, v)
            a = jnp.where(keep, a * a_s, a)
        a_ref[d] = a.reshape(T, LRU_W)
        u_ref[d] = v.reshape(T, LRU_W)

    def tile_step(i, carry):
        h_f, h_b = carry
        c_f = i
        c_b = jnp.where(i == 0, 0, N_TILES - i)
        local_scan(c_f, 0)
        local_scan(c_b, 1)
        r_f = pl.multiple_of(c_f * T, T)
        r_b = pl.multiple_of(c_b * T, T)
        for gi in range(GROUPS_PER_TILE):
            lo = gi * S
            a, v = a_ref[0, lo:lo + S], u_ref[0, lo:lo + S]
            hf_ref[pl.ds(r_f + lo, S), :] = v + a * h_f
            h_f = (jnp.broadcast_to(v[S - 1:S], (S, LRU_W))
                   + jnp.broadcast_to(a[S - 1:S], (S, LRU_W)) * h_f)
            lo = (GROUPS_PER_TILE - 1 - gi) * S
            a, v = a_ref[1, lo:lo + S], u_ref[1, lo:lo + S]
            hb_ref[pl.ds(r_b + lo, S), :] = v + a * h_b
            h_b = (jnp.broadcast_to(v[0:1], (S, LRU_W))
                   + jnp.broadcast_to(a[0:1], (S, LRU_W)) * h_b)
        return h_f, h_b

    zeros = jnp.zeros((S, LRU_W), F32)
    lax.fori_loop(0, N_TILES, tile_step, (zeros, zeros))

    lane = lax.broadcasted_iota(jnp.int32, (T, POOL_W), 1)
    half_win = jnp.where(lane < POOL_GW, POOL_WINDOWS[0] // 2,
                         jnp.where(lane < 2 * POOL_GW, POOL_WINDOWS[1] // 2,
                                   jnp.where(lane < 3 * POOL_GW, POOL_WINDOWS[2] // 2,
                                             POOL_WINDOWS[3] // 2)))
    row = lax.broadcasted_iota(jnp.int32, (T, POOL_W), 0)

    def out_step(c, _):
        r0 = pl.multiple_of(c * T, T)
        gate = rest_ref[0, pl.ds(r0, T), LRU_W:2 * LRU_W]
        h = hf_ref[pl.ds(r0, T), :] + hb_ref[pl.ds(r0, T), :]
        rec_ref[0, pl.ds(r0, T), :] = (jax.nn.gelu(gate) * h).astype(BF16)

        px, ext = _with_halo(rest_ref, 2 * LRU_W, c)
        p2 = ext + pltpu.roll(ext, 1, 0)
        p4 = p2 + pltpu.roll(p2, 2, 0)
        p8 = p4 + pltpu.roll(p4, 4, 0)
        p16 = p8 + pltpu.roll(p8, 8, 0)
        win = jnp.where(lane < POOL_GW, _shift_rows(p2, 0),
                        jnp.where(lane < 2 * POOL_GW, _shift_rows(p4, 1),
                                  jnp.where(lane < 3 * POOL_GW, _shift_rows(p8, 3),
                                            _shift_rows(p16, 7))))
        seg_len = jnp.where(c == 0, CTX_LEN, SEQ)
        pos = row + jnp.where(c == 0, 0, r0 - CTX_LEN)
        cnt = jnp.minimum(pos + half_win, seg_len) - jnp.maximum(pos - half_win, 0)
        d = win / cnt.astype(F32) - px
        y = _dot(d.astype(BF16), pw_ref[...]) + pb_ref[...]
        pool_ref[0, pl.ds(r0, T), :] = (y * ps_ref[...]).astype(BF16)
        return 0

    lax.fori_loop(0, N_TILES, out_step, 0)


def _lru_pool(l, rest, conv_w, conv_b, gate_w, gate_b, lam, pool_w, pool_b, pool_scale):
    seq = lambda w: pl.BlockSpec((1, LALL, w), lambda b: (b, 0, 0))
    return pl.pallas_call(
        _lru_pool_kernel,
        grid=(BATCH,),
        in_specs=[
            seq(REST_W),
            _layer_spec(l, (CONV_W, LRU_W)), _layer_spec(l, (1, LRU_W)),
            _layer_spec(l, (LRU_W, 4 * LRU_W)), _layer_spec(l, (1, 4 * LRU_W)), _layer_spec(l, (2, LRU_W)),
            _layer_spec(l, (POOL_W, POOL_W)), _layer_spec(l, (1, POOL_W)), _layer_spec(l, (1, POOL_W)),
        ],
        out_specs=[seq(LRU_W), seq(POOL_W)],
        out_shape=[jax.ShapeDtypeStruct((BATCH, LALL, LRU_W), BF16),
                   jax.ShapeDtypeStruct((BATCH, LALL, POOL_W), BF16)],
        scratch_shapes=[
            pltpu.VMEM((LALL, LRU_W), F32), pltpu.VMEM((LALL, LRU_W), F32),
            pltpu.VMEM((2, TOKEN_TILE, LRU_W), F32), pltpu.VMEM((2, TOKEN_TILE, LRU_W), F32),
            pltpu.VMEM((LALL, LRU_W), F32),
        ],
        compiler_params=pltpu.CompilerParams(
            dimension_semantics=("arbitrary",),
            vmem_limit_bytes=_vmem_limit(56 * 1024 * 1024)),
        name="lru_pool",
    )(rest, conv_w, conv_b, gate_w, gate_b, lam, pool_w, pool_b, pool_scale)


def _out_ffn_kernel(*refs, n_sub):
    T = TOKEN_TILE
    att, rec, pool = refs[0:n_sub], refs[n_sub:2 * n_sub], refs[2 * n_sub:3 * n_sub]
    x_ref, mod_ref, g_ref, wo_ref, wi_ref, wf_ref, o_ref = refs[3 * n_sub:]
    mod = mod_ref[0]
    ys = [_dot(att[k][0], wo_ref[0:ATTN_W])
          + _dot(rec[k][0], wo_ref[ATTN_W:ATTN_W + LRU_W])
          + _dot(pool[k][0], wo_ref[ATTN_W + LRU_W:MIX_W]) for k in range(n_sub)]
    for k in range(n_sub):
        rows = slice(k * T, (k + 1) * T)
        x = x_ref[0, rows] + mod[2:3] * (_rms(ys[k]) * g_ref[1:2])
        h = ((_rms(x) * g_ref[2:3]) * (1.0 + mod[4:5]) + mod[3:4]).astype(BF16)
        gate_up = lambda lo, n: (_dot(h, wi_ref[:, lo:lo + n]),
                                 _dot(h, wi_ref[:, FFN_HIDDEN + lo:FFN_HIDDEN + lo + n]))
        gu = gate_up(*FFN_CHUNKS[0])
        f = None
        for i, (lo, n) in enumerate(FFN_CHUNKS):
            gu_next = gate_up(*FFN_CHUNKS[i + 1]) if i + 1 < len(FFN_CHUNKS) else None
            g, u = gu
            part = _dot((g * jax.nn.sigmoid(g) * u).astype(BF16), wf_ref[lo:lo + n])
            f = part if f is None else f + part
            gu = gu_next
        o_ref[0, rows] = x + mod[5:6] * (_rms(f) * g_ref[3:4])


def _out_ffn(l, att, rec, pool, x_src, mod, norm_g, w_out, w_ffn_in, w_ffn_out, *, context):
    T = TOKEN_TILE
    n_sub = 1 if context else FFN_TILES_PER_STEP
    n_steps = 1 if context else (N_TILES - 1) // n_sub
    tile = lambda k, skip: (lambda b, t: (b, 0 if context else n_sub * t + k + skip, 0))
    weight = lambda shape: _layer_spec(l, shape, pipeline_mode=pl.Buffered(1))
    subs = range(n_sub)
    return pl.pallas_call(
        functools.partial(_out_ffn_kernel, n_sub=n_sub),
        grid=(BATCH, n_steps),
        in_specs=[
            *[pl.BlockSpec((1, T, ATTN_W), tile(k, 0)) for k in subs],
            *[pl.BlockSpec((1, T, LRU_W), tile(k, 1)) for k in subs],
            *[pl.BlockSpec((1, T, POOL_W), tile(k, 1)) for k in subs],
            pl.BlockSpec((1, n_sub * T, D_MODEL), lambda b, t: (b, t, 0)),
            pl.BlockSpec((None, 1, 6, D_MODEL), lambda b, t: (l, BATCH if context else b, 0, 0)),
            weight((4, D_MODEL)),
            weight((MIX_W, D_MODEL)),
            weight((D_MODEL, 2 * FFN_HIDDEN)),
            weight((FFN_HIDDEN, D_MODEL)),
        ],
        out_specs=pl.BlockSpec((1, n_sub * T, D_MODEL), lambda b, t: (b, t, 0)),
        out_shape=jax.ShapeDtypeStruct((BATCH, n_steps * n_sub * T, D_MODEL), F32),
        compiler_params=pltpu.CompilerParams(
            dimension_semantics=("arbitrary", "arbitrary"),
            vmem_limit_bytes=_vmem_limit(52 * 1024 * 1024)),
        name="out_ffn_ctx" if context else "out_ffn",
    )(*([att] * n_sub), *([rec] * n_sub), *([pool] * n_sub), x_src, mod, norm_g,
      w_out, w_ffn_in, w_ffn_out)


def _position_tables():
    half = ROPE_HALF
    freq = (ROPE_THETA ** (-np.arange(half, dtype=np.float32) / half)).astype(np.float32)
    p = np.arange(SEQ)
    ang_row = (p // GRID_W).astype(np.float32)[:, None] * freq
    ang_col = (p % GRID_W).astype(np.float32)[:, None] * freq
    ang = np.concatenate([ang_row, ang_row, ang_col, ang_col], axis=1)
    first = np.tile(np.arange(HEAD_DIM) % (2 * half) < half, (SEQ, 1))
    cos = np.cos(ang)
    sin_a = np.where(first, -np.sin(ang), 0.0)
    sin_b = np.where(first, 0.0, np.sin(ang))
    pad = lambda a, v: np.concatenate([np.full((CTX_LEN, HEAD_DIM), v), a], axis=0)
    two = lambda a: np.tile(a, (1, V7X_LANES // HEAD_DIM)).astype(np.float32)
    return two(pad(cos, 1.0)), two(pad(sin_a, 0.0)), two(pad(sin_b, 0.0))


def _head_mean_matrix(width):
    blk = np.kron(np.eye(width // HEAD_DIM), np.full((HEAD_DIM, HEAD_DIM), 1.0 / HEAD_DIM))
    return jnp.asarray(blk, dtype=BF16)


def _block_diag(w):
    n, c, d = w.shape[-3:]
    on_diag = jnp.eye(n, dtype=bool)[:, None, :, None]
    out = jnp.where(on_diag, w[..., :, :, None, :], 0.0)
    return out.reshape(w.shape[:-3] + (n * c, n * d))


def kernel(x, c, ctx, c_ctx, w_mod, b_mod, norm_g, w_in, q_norm_g, k_norm_g, lru_conv_w, lru_conv_b,
           lru_gate_w, lru_gate_b, lru_lambda, pool_w, pool_b, pool_scale, w_out, w_ffn_in, w_ffn_out):
    assert x.shape == (BATCH, SEQ, D_MODEL) and ctx.shape == (BATCH, CTX_LEN, D_MODEL)
    cos, sin_a, sin_b = (jnp.asarray(a) for a in _position_tables())
    cos_t, sin_t = cos[:, 0:HEAD_DIM].T, (sin_a + sin_b)[:, 0:HEAD_DIM].T
    consts = (_head_mean_matrix(KV_W), cos, sin_a, sin_b, cos_t, sin_t)

    c_all = jnp.zeros((MOD_ROWS, D_MODEL), F32).at[0:BATCH].set(c).at[BATCH].set(c_ctx)
    mod = _modulation(c_all, w_mod, b_mod).reshape(DEPTH, MOD_ROWS, 6, D_MODEL)

    w_in_b, w_out_b = w_in.astype(BF16), w_out.astype(BF16)
    w_ffn_in_b, w_ffn_out_b = w_ffn_in.astype(BF16), w_ffn_out.astype(BF16)
    gq = jnp.broadcast_to(q_norm_g[:, :, None], (DEPTH, HEAD_DIM, TOKEN_TILE))
    gk = jnp.tile(k_norm_g, (1, N_KV_HEADS))[:, None, :]
    gate_w = _block_diag(lru_gate_w)
    gate_w = gate_w.transpose(0, 3, 1, 2, 4).reshape(DEPTH, LRU_W, 4 * LRU_W).astype(BF16)
    gate_b = lru_gate_b.reshape(DEPTH, 1, 4 * LRU_W)
    pool_w_b = _block_diag(pool_w).astype(BF16)
    conv_b, pool_b3, pool_s3 = lru_conv_b[:, None, :], pool_b[:, None, :], pool_scale[:, None, :]

    for l in range(DEPTH):
        last = l == DEPTH - 1
        q, k, vt, rest = _in_projection(l, ctx, x, mod, norm_g, w_in_b, consts, gq, gk)
        att = _attention(q, k, vt, context=False)
        rec, pool = _lru_pool(l, rest, lru_conv_w, conv_b, gate_w, gate_b, lru_lambda,
                              pool_w_b, pool_b3, pool_s3)
        weights = (mod, norm_g, w_out_b, w_ffn_in_b, w_ffn_out_b)
        x_next = _out_ffn(l, att, rec, pool, x, *weights, context=False)
        if not last:
            att_ctx = _attention(q, k, vt, context=True)
            ctx = _out_ffn(l, att_ctx, rec, pool, ctx, *weights, context=True)
        x = x_next
    return x
```

```python
import functools

import numpy as np
import jax
import jax.numpy as jnp
from jax import lax
from jax.experimental import pallas as pl
from jax.experimental.pallas import tpu as pltpu

D_MODEL = 1024
BATCH = 4
SEQ = 4096
DEPTH = 2
GRID_W = 64
CTX_LEN = 256
LALL = CTX_LEN + SEQ

N_Q_HEADS = 8
N_KV_HEADS = 2
HEAD_DIM = 64
Q_GROUP = N_Q_HEADS // N_KV_HEADS
ATTN_W = N_Q_HEADS * HEAD_DIM
KV_W = N_KV_HEADS * HEAD_DIM
ROPE_THETA = 10000.0
LRU_W = D_MODEL // 4
LRU_BLOCKS = 4
LRU_BW = LRU_W // LRU_BLOCKS
CONV_W = 4
LRU_C = 8.0
POOL_W = D_MODEL // 4
POOL_GROUPS = 4
POOL_GW = POOL_W // POOL_GROUPS
POOL_WINDOWS = (2, 4, 8, 16)
MIX_W = ATTN_W + LRU_W + POOL_W
IN_W = ATTN_W + 2 * KV_W + 2 * LRU_W + POOL_W
REST_W = 2 * LRU_W + POOL_W
FFN_HIDDEN = -(-8 * D_MODEL // 768) * 256
RMS_EPS = 1e-6

V7X_LANES = 128
V7X_SUBLANES = 8
V7X_VMEM_BYTES = 64 * 1024 * 1024

TOKEN_TILE = CTX_LEN
N_TILES = LALL // TOKEN_TILE
MOD_ROWS = 8
MOD_TILE_N = 1536
ROPE_HALF = HEAD_DIM // 4
GROUPS_PER_TILE = TOKEN_TILE // V7X_SUBLANES
ONES_ROWS = 16
ATTN_Q_TILES = 4
KEY_CHUNK = TOKEN_TILE
SCORE_SLOTS = 3
FFN_TILES_PER_STEP = 4
FFN_CHUNK = 512
FFN_CHUNKS = [(lo, min(FFN_CHUNK, FFN_HIDDEN - lo)) for lo in range(0, FFN_HIDDEN, FFN_CHUNK)]
Q_SCALE = HEAD_DIM ** -0.5 * float(np.log2(np.e))

F32 = jnp.float32
BF16 = jnp.bfloat16


def _vmem_limit(nbytes):
    return int(min(max(nbytes, 16 * 1024 * 1024), V7X_VMEM_BYTES - 6 * 1024 * 1024))


def _rms(x):
    return x * lax.rsqrt(jnp.mean(x * x, axis=-1, keepdims=True) + RMS_EPS)


def _dot(a, b):
    return jnp.dot(a, b, preferred_element_type=F32)


def _layer_spec(l, shape, **kw):
    return pl.BlockSpec((None,) + shape, lambda *_: (l,) + (0,) * len(shape), **kw)


def _mod_row(b, t):
    return jnp.where(t == 0, BATCH, b)


def _token_specs(width):
    T = TOKEN_TILE
    ctx_spec = pl.BlockSpec((1, T, width), lambda b, t: (b, 0, 0))
    next_spec = pl.BlockSpec((1, T, width), lambda b, t: (b, jnp.minimum(t, N_TILES - 2), 0))
    return ctx_spec, next_spec


def _mod_kernel(c_ref, w_ref, b_ref, o_ref):
    c = c_ref[...]
    h = (c * jax.nn.sigmoid(c)).astype(BF16)
    o_ref[0] = _dot(h, w_ref[0].astype(BF16)) + b_ref[0]


def _modulation(c_all, w_mod, b_mod):
    n = 6 * D_MODEL
    return pl.pallas_call(
        _mod_kernel,
        grid=(DEPTH, n // MOD_TILE_N),
        in_specs=[
            pl.BlockSpec((MOD_ROWS, D_MODEL), lambda l, j: (0, 0)),
            pl.BlockSpec((1, D_MODEL, MOD_TILE_N), lambda l, j: (l, 0, j)),
            pl.BlockSpec((1, 1, MOD_TILE_N), lambda l, j: (l, 0, j)),
        ],
        out_specs=pl.BlockSpec((1, MOD_ROWS, MOD_TILE_N), lambda l, j: (l, 0, j)),
        out_shape=jax.ShapeDtypeStruct((DEPTH, MOD_ROWS, n), F32),
        compiler_params=pltpu.CompilerParams(
            dimension_semantics=("arbitrary", "arbitrary"),
            vmem_limit_bytes=_vmem_limit(4 * D_MODEL * MOD_TILE_N * 4)),
        name="modulation",
    )(c_all, w_mod, b_mod.reshape(DEPTH, 1, n))


def _head_norm_rope(z, s_ref, g, cos, sin_a, sin_b):
    z2 = z * z
    hi = z2.astype(BF16)
    lo = (z2 - hi.astype(F32)).astype(BF16)
    ms = _dot(hi, s_ref[...]) + _dot(lo, s_ref[...])
    zn = (z * lax.rsqrt(ms + RMS_EPS)) * g
    outs = []
    for c in range(z.shape[1] // V7X_LANES):
        zc = zn[:, c * V7X_LANES:(c + 1) * V7X_LANES]
        up = pltpu.roll(zc, V7X_LANES - ROPE_HALF, 1)
        dn = pltpu.roll(zc, ROPE_HALF, 1)
        outs.append(zc * cos + up * sin_a + dn * sin_b)
    return outs


def _head_norm_rope_t(zt, g, cos_t, sin_t):
    zn = (zt * lax.rsqrt(jnp.mean(zt * zt, axis=0, keepdims=True) + RMS_EPS)) * g
    h = ROPE_HALF
    partner = jnp.concatenate([zn[h:2 * h], zn[0:h], zn[3 * h:4 * h], zn[2 * h:3 * h]], axis=0)
    return zn * cos_t + partner * sin_t


def _inproj_kernel(ctx_ref, xn_ref, mod_ref, modn_ref, g_ref, w_ref, sk_ref, gq_ref, gk_ref,
                   cos_ref, sa_ref, sb_ref, cost_ref, sint_ref, q_ref, k_ref, vt_ref, rest_ref, h_ref):
    def norm_mod(x, mod):
        return ((_rms(x) * g_ref[0:1]) * (1.0 + mod[1:2]) + mod[0:1]).astype(BF16)

    @pl.when(pl.program_id(1) == 0)
    def _():
        h_ref[...] = norm_mod(ctx_ref[0], mod_ref[0])

    h = h_ref[...]
    qkv_w = ATTN_W + 2 * KV_W
    first_w = qkv_w + LRU_W
    y = _dot(h, w_ref[:, 0:first_w])
    (kc,) = _head_norm_rope(y[:, ATTN_W:ATTN_W + KV_W], sk_ref, gk_ref[...],
                            cos_ref[...], sa_ref[...], sb_ref[...])
    rest_ref[0, :, 0:LRU_W] = y[:, qkv_w:first_w]
    rest_ref[0, :, LRU_W:REST_W] = _dot(h, w_ref[:, first_w:IN_W])
    h_ref[...] = norm_mod(xn_ref[0], modn_ref[0])

    for c in range(ATTN_W // V7X_LANES):
        zt = y[:, c * V7X_LANES:(c + 1) * V7X_LANES].T
        for i in range(V7X_LANES // HEAD_DIM):
            qt = _head_norm_rope_t(zt[i * HEAD_DIM:(i + 1) * HEAD_DIM], gq_ref[...],
                                   cost_ref[...], sint_ref[...])
            q_ref[0, 2 * c + i, 0] = (qt * Q_SCALE).astype(BF16)

    kc = kc.astype(BF16)
    k_ref[0, 0] = kc[:, 0:HEAD_DIM]
    k_ref[0, 1] = kc[:, HEAD_DIM:2 * HEAD_DIM]

    vt = y[:, ATTN_W + KV_W:qkv_w].T.astype(BF16)
    vt_ref[0, 0, 0] = vt[0:HEAD_DIM]
    vt_ref[0, 1, 0] = vt[HEAD_DIM:2 * HEAD_DIM]


def _in_projection(l, ctx_src, x_src, mod, norm_g, w_in, consts, gq, gk):
    sk, cos, sin_a, sin_b, cos_t, sin_t = consts
    T = TOKEN_TILE
    full = lambda shape: pl.BlockSpec(shape, lambda b, t: (0,) * len(shape))
    tab = pl.BlockSpec((T, V7X_LANES), lambda b, t: (t, 0))
    return pl.pallas_call(
        _inproj_kernel,
        grid=(BATCH, N_TILES),
        in_specs=[
            *_token_specs(D_MODEL),
            pl.BlockSpec((None, 1, 6, D_MODEL), lambda b, t: (l, _mod_row(b, t), 0, 0)),
            pl.BlockSpec((None, 1, 6, D_MODEL), lambda b, t: (l, b, 0, 0)),
            _layer_spec(l, (4, D_MODEL)),
            _layer_spec(l, (D_MODEL, IN_W)),
            full((KV_W, KV_W)),
            _layer_spec(l, (HEAD_DIM, T)),
            _layer_spec(l, (1, KV_W)),
            tab, tab, tab,
            pl.BlockSpec((HEAD_DIM, T), lambda b, t: (0, t)),
            pl.BlockSpec((HEAD_DIM, T), lambda b, t: (0, t)),
        ],
        out_specs=[
            pl.BlockSpec((1, N_Q_HEADS, 1, HEAD_DIM, T), lambda b, t: (b, 0, t, 0, 0)),
            pl.BlockSpec((1, N_KV_HEADS, T, HEAD_DIM), lambda b, t: (b, 0, t, 0)),
            pl.BlockSpec((1, N_KV_HEADS, 1, HEAD_DIM, T), lambda b, t: (b, 0, t, 0, 0)),
            pl.BlockSpec((1, T, REST_W), lambda b, t: (b, t, 0)),
        ],
        out_shape=[
            jax.ShapeDtypeStruct((BATCH, N_Q_HEADS, N_TILES, HEAD_DIM, T), BF16),
            jax.ShapeDtypeStruct((BATCH, N_KV_HEADS, LALL, HEAD_DIM), BF16),
            jax.ShapeDtypeStruct((BATCH, N_KV_HEADS, N_TILES, HEAD_DIM, T), BF16),
            jax.ShapeDtypeStruct((BATCH, LALL, REST_W), F32),
        ],
        scratch_shapes=[pltpu.VMEM((T, D_MODEL), BF16)],
        compiler_params=pltpu.CompilerParams(
            dimension_semantics=("arbitrary", "arbitrary"),
            vmem_limit_bytes=_vmem_limit(40 * 1024 * 1024)),
        name="in_projection",
    )(ctx_src, x_src, mod, mod, norm_g, w_in, sk, gq, gk, cos, sin_a, sin_b, cos_t, sin_t)


def _attn_kernel(*refs, q_tiles, n_chunks):
    T = TOKEN_TILE
    q_refs = refs[:q_tiles]
    k_ref, vt_ref, o_ref, s_ref, m_ref, acc_ref = refs[q_tiles:]
    pieces = Q_GROUP // 2

    def scores(n, piece):
        i, c = divmod(n, n_chunks)
        qt = jnp.concatenate([q_refs[i][0, g, 0] for g in (2 * piece, 2 * piece + 1)], axis=1)
        cols = slice(2 * piece * T, 2 * (piece + 1) * T)
        s_ref[n % SCORE_SLOTS, :, cols] = _dot(k_ref[0, 0, c * KEY_CHUNK:(c + 1) * KEY_CHUNK, :], qt)

    def update(n, piece):
        i, c = divmod(n, n_chunks)
        tile, off = divmod(c * KEY_CHUNK, T)
        vt1 = jnp.concatenate([vt_ref[0, 0, tile][:, off:off + KEY_CHUNK],
                               jnp.ones((ONES_ROWS, KEY_CHUNK), BF16)], axis=0)
        for g in (2 * piece, 2 * piece + 1):
            cols = slice(g * T, (g + 1) * T)
            s = s_ref[n % SCORE_SLOTS, :, cols]
            m = m_ref[i, :, cols]
            m_new = jnp.maximum(m, jnp.max(s, axis=0, keepdims=True))
            alpha = jnp.exp2(m - m_new)
            p = jnp.exp2((s - m_new).astype(BF16))
            m_ref[i, :, cols] = m_new
            acc_ref[i, :, cols] = alpha * acc_ref[i, :, cols] + _dot(vt1, p)

    def finish(i):
        o = acc_ref[i, 0:HEAD_DIM] / acc_ref[i, HEAD_DIM:HEAD_DIM + 1]
        o = jnp.concatenate([o[:, g * T:(g + 1) * T] for g in range(Q_GROUP)], axis=0)
        o_ref[0, i * T:(i + 1) * T, :] = o.T.astype(BF16)

    def tile_block(i):
        if i > 0:
            finish(i - 1)
        for n in range(i * n_chunks, (i + 1) * n_chunks):
            for piece in range(pieces):
                if n + SCORE_SLOTS - 1 < q_tiles * n_chunks:
                    scores(n + SCORE_SLOTS - 1, piece)
                update(n, piece)

    m_ref[...] = jnp.full(m_ref.shape, -1e30, F32)
    acc_ref[...] = jnp.zeros(acc_ref.shape, F32)
    for n in range(min(SCORE_SLOTS - 1, n_chunks)):
        for piece in range(pieces):
            scores(n, piece)
    for i in range(q_tiles):
        pl.when(pl.program_id(2) >= -i)(functools.partial(tile_block, i))
    finish(q_tiles - 1)


def _attention(q, k, vt, *, context):
    T = TOKEN_TILE
    q_tiles = 1 if context else ATTN_Q_TILES
    n_steps = 1 if context else (N_TILES - 1) // q_tiles
    n_cols = Q_GROUP * T
    q_spec = lambda i: pl.BlockSpec(
        (1, Q_GROUP, 1, HEAD_DIM, T), lambda b, h, t: (b, h, 0 if context else q_tiles * t + i + 1, 0, 0))
    return pl.pallas_call(
        functools.partial(_attn_kernel, q_tiles=q_tiles, n_chunks=(CTX_LEN if context else LALL) // KEY_CHUNK),
        grid=(BATCH, N_KV_HEADS, n_steps),
        in_specs=[
            *[q_spec(i) for i in range(q_tiles)],
            pl.BlockSpec((1, 1, LALL, HEAD_DIM), lambda b, h, t: (b, h, 0, 0)),
            pl.BlockSpec((1, 1, N_TILES, HEAD_DIM, T), lambda b, h, t: (b, h, 0, 0, 0)),
        ],
        out_specs=pl.BlockSpec((1, q_tiles * T, Q_GROUP * HEAD_DIM), lambda b, h, t: (b, t, h)),
        out_shape=jax.ShapeDtypeStruct((BATCH, n_steps * q_tiles * T, ATTN_W), BF16),
        scratch_shapes=[
            pltpu.VMEM((SCORE_SLOTS, KEY_CHUNK, n_cols), F32),
            pltpu.VMEM((q_tiles, 1, n_cols), F32),
            pltpu.VMEM((q_tiles, HEAD_DIM + ONES_ROWS, n_cols), F32),
        ],
        compiler_params=pltpu.CompilerParams(
            dimension_semantics=("arbitrary", "arbitrary", "arbitrary"),
            vmem_limit_bytes=_vmem_limit(40 * 1024 * 1024)),
        name="attention_ctx" if context else "attention",
    )(*([q] * q_tiles), k, vt)


def _softplus(z):
    return jnp.maximum(z, 0.0) + jnp.log1p(jnp.exp(-jnp.abs(z)))


def _with_halo(ref, col, c):
    T = TOKEN_TILE
    r0 = pl.multiple_of(c * T, T)
    main = ref[0, pl.ds(r0, T), col:col + LRU_W]
    p0 = pl.multiple_of(jnp.maximum(r0 - V7X_SUBLANES, 0), V7X_SUBLANES)
    n0 = pl.multiple_of(jnp.minimum(r0 + T, LALL - V7X_SUBLANES), V7X_SUBLANES)
    prev = ref[0, pl.ds(p0, V7X_SUBLANES), col:col + LRU_W]
    nxt = ref[0, pl.ds(n0, V7X_SUBLANES), col:col + LRU_W]
    prev = jnp.where(c >= 2, prev, 0.0)
    nxt = jnp.where((c >= 1) & (c <= N_TILES - 2), nxt, 0.0)
    return main, jnp.concatenate([prev, main, nxt], axis=0)


def _shift_rows(ext, k):
    n = ext.shape[0]
    if k == 0:
        return ext[V7X_SUBLANES:V7X_SUBLANES + TOKEN_TILE]
    return pltpu.roll(ext, (-k) % n, 0)[V7X_SUBLANES:V7X_SUBLANES + TOKEN_TILE]


def _lru_pool_kernel(rest_ref, cw_ref, cb_ref, gw_ref, gb_ref, lam_ref, pw_ref, pb_ref, ps_ref,
                     rec_ref, pool_ref, hf_ref, hb_ref, a_ref, u_ref, conv_ref):
    T = TOKEN_TILE
    S = V7X_SUBLANES
    grouped = (GROUPS_PER_TILE, S, LRU_W)
    row_in_group = lax.broadcasted_iota(jnp.int32, grouped, 1)

    def conv_step(c, _):
        main, ext = _with_halo(rest_ref, 0, c)
        conv_ref[pl.ds(pl.multiple_of(c * T, T), T), :] = cb_ref[...] + (
            cw_ref[0:1] * _shift_rows(ext, -2) + cw_ref[1:2] * _shift_rows(ext, -1)
            + cw_ref[2:3] * main + cw_ref[3:4] * _shift_rows(ext, 1))
        return 0

    lax.fori_loop(0, N_TILES, conv_step, 0)

    def local_scan(c, d):
        u = conv_ref[pl.ds(pl.multiple_of(c * T, T), T), :]
        g = _dot(u.astype(BF16), gw_ref[:, d * 2 * LRU_W:(d + 1) * 2 * LRU_W])
        g = jax.nn.sigmoid(g + gb_ref[:, d * 2 * LRU_W:(d + 1) * 2 * LRU_W])
        log_a = (-LRU_C * g[:, 0:LRU_W]) * _softplus(-lam_ref[d:d + 1])
        a = jnp.exp(log_a)
        neg_expm1 = (1.0 - a) * (1.0 + a)
        root = jnp.where(neg_expm1 > 0.0, neg_expm1 * lax.rsqrt(neg_expm1), 0.0)
        v = root * (g[:, LRU_W:2 * LRU_W] * u)
        a, v = a.reshape(grouped), v.reshape(grouped)
        for step in (1, 2, 4):
            if d == 0:
                a_s, v_s = pltpu.roll(a, step, 1), pltpu.roll(v, step, 1)
                keep = row_in_group >= step
            else:
                a_s, v_s = pltpu.roll(a, S - step, 1), pltpu.roll(v, S - step, 1)
                keep = row_in_group < S - step
            v = jnp.where(keep, a * v_s + v, v)
            a = jnp.where(keep, a * a_s, a)
        a_ref[d] = a.reshape(T, LRU_W)
        u_ref[d] = v.reshape(T, LRU_W)

    def tile_step(i, carry):
        h_f, h_b = carry
        c_f = i
        c_b = jnp.where(i == 0, 0, N_TILES - i)
        local_scan(c_f, 0)
        local_scan(c_b, 1)
        r_f = pl.multiple_of(c_f * T, T)
        r_b = pl.multiple_of(c_b * T, T)
        for gi in range(GROUPS_PER_TILE):
            lo = gi * S
            a, v = a_ref[0, lo:lo + S], u_ref[0, lo:lo + S]
            hf_ref[pl.ds(r_f + lo, S), :] = v + a * h_f
            h_f = (jnp.broadcast_to(v[S - 1:S], (S, LRU_W))
                   + jnp.broadcast_to(a[S - 1:S], (S, LRU_W)) * h_f)
            lo = (GROUPS_PER_TILE - 1 - gi) * S
            a, v = a_ref[1, lo:lo + S], u_ref[1, lo:lo + S]
            hb_ref[pl.ds(r_b + lo, S), :] = v + a * h_b
            h_b = (jnp.broadcast_to(v[0:1], (S, LRU_W))
                   + jnp.broadcast_to(a[0:1], (S, LRU_W)) * h_b)
        return h_f, h_b

    zeros = jnp.zeros((S, LRU_W), F32)
    lax.fori_loop(0, N_TILES, tile_step, (zeros, zeros))

    lane = lax.broadcasted_iota(jnp.int32, (T, POOL_W), 1)
    half_win = jnp.where(lane < POOL_GW, POOL_WINDOWS[0] // 2,
                         jnp.where(lane < 2 * POOL_GW, POOL_WINDOWS[1] // 2,
                                   jnp.where(lane < 3 * POOL_GW, POOL_WINDOWS[2] // 2,
                                             POOL_WINDOWS[3] // 2)))
    row = lax.broadcasted_iota(jnp.int32, (T, POOL_W), 0)

    def out_step(c, _):
        r0 = pl.multiple_of(c * T, T)
        gate = rest_ref[0, pl.ds(r0, T), LRU_W:2 * LRU_W]
        h = hf_ref[pl.ds(r0, T), :] + hb_ref[pl.ds(r0, T), :]
        rec_ref[0, pl.ds(r0, T), :] = (jax.nn.gelu(gate) * h).astype(BF16)

        px, ext = _with_halo(rest_ref, 2 * LRU_W, c)
        p2 = ext + pltpu.roll(ext, 1, 0)
        p4 = p2 + pltpu.roll(p2, 2, 0)
        p8 = p4 + pltpu.roll(p4, 4, 0)
        p16 = p8 + pltpu.roll(p8, 8, 0)
        win = jnp.where(lane < POOL_GW, _shift_rows(p2, 0),
                        jnp.where(lane < 2 * POOL_GW, _shift_rows(p4, 1),
                                  jnp.where(lane < 3 * POOL_GW, _shift_rows(p8, 3),
                                            _shift_rows(p16, 7))))
        seg_len = jnp.where(c == 0, CTX_LEN, SEQ)
        pos = row + jnp.where(c == 0, 0, r0 - CTX_LEN)
        cnt = jnp.minimum(pos + half_win, seg_len) - jnp.maximum(pos - half_win, 0)
        d = win / cnt.astype(F32) - px
        y = _dot(d.astype(BF16), pw_ref[...]) + pb_ref[...]
        pool_ref[0, pl.ds(r0, T), :] = (y * ps_ref[...]).astype(BF16)
        return 0

    lax.fori_loop(0, N_TILES, out_step, 0)


def _lru_pool(l, rest, conv_w, conv_b, gate_w, gate_b, lam, pool_w, pool_b, pool_scale):
    seq = lambda w: pl.BlockSpec((1, LALL, w), lambda b: (b, 0, 0))
    return pl.pallas_call(
        _lru_pool_kernel,
        grid=(BATCH,),
        in_specs=[
            seq(REST_W),
            _layer_spec(l, (CONV_W, LRU_W)), _layer_spec(l, (1, LRU_W)),
            _layer_spec(l, (LRU_W, 4 * LRU_W)), _layer_spec(l, (1, 4 * LRU_W)), _layer_spec(l, (2, LRU_W)),
            _layer_spec(l, (POOL_W, POOL_W)), _layer_spec(l, (1, POOL_W)), _layer_spec(l, (1, POOL_W)),
        ],
        out_specs=[seq(LRU_W), seq(POOL_W)],
        out_shape=[jax.ShapeDtypeStruct((BATCH, LALL, LRU_W), BF16),
                   jax.ShapeDtypeStruct((BATCH, LALL, POOL_W), BF16)],
        scratch_shapes=[
            pltpu.VMEM((LALL, LRU_W), F32), pltpu.VMEM((LALL, LRU_W), F32),
            pltpu.VMEM((2, TOKEN_TILE, LRU_W), F32), pltpu.VMEM((2, TOKEN_TILE, LRU_W), F32),
            pltpu.VMEM((LALL, LRU_W), F32),
        ],
        compiler_params=pltpu.CompilerParams(
            dimension_semantics=("arbitrary",),
            vmem_limit_bytes=_vmem_limit(56 * 1024 * 1024)),
        name="lru_pool",
    )(rest, conv_w, conv_b, gate_w, gate_b, lam, pool_w, pool_b, pool_scale)


def _out_ffn_kernel(*refs, n_sub):
    T = TOKEN_TILE
    att, rec, pool = refs[0:n_sub], refs[n_sub:2 * n_sub], refs[2 * n_sub:3 * n_sub]
    x_ref, mod_ref, g_ref, wo_ref, wi_ref, wf_ref, o_ref = refs[3 * n_sub:]
    mod = mod_ref[0]
    ys = [_dot(att[k][0], wo_ref[0:ATTN_W])
          + _dot(rec[k][0], wo_ref[ATTN_W:ATTN_W + LRU_W])
          + _dot(pool[k][0], wo_ref[ATTN_W + LRU_W:MIX_W]) for k in range(n_sub)]
    for k in range(n_sub):
        rows = slice(k * T, (k + 1) * T)
        x = x_ref[0, rows] + mod[2:3] * (_rms(ys[k]) * g_ref[1:2])
        h = ((_rms(x) * g_ref[2:3]) * (1.0 + mod[4:5]) + mod[3:4]).astype(BF16)
        gate_up = lambda lo, n: (_dot(h, wi_ref[:, lo:lo + n]),
                                 _dot(h, wi_ref[:, FFN_HIDDEN + lo:FFN_HIDDEN + lo + n]))
        gu = gate_up(*FFN_CHUNKS[0])
        f = None
        for i, (lo, n) in enumerate(FFN_CHUNKS):
            gu_next = gate_up(*FFN_CHUNKS[i + 1]) if i + 1 < len(FFN_CHUNKS) else None
            g, u = gu
            part = _dot((g * jax.nn.sigmoid(g) * u).astype(BF16), wf_ref[lo:lo + n])
            f = part if f is None else f + part
            gu = gu_next
        o_ref[0, rows] = x + mod[5:6] * (_rms(f) * g_ref[3:4])


def _out_ffn(l, att, rec, pool, x_src, mod, norm_g, w_out, w_ffn_in, w_ffn_out, *, context):
    T = TOKEN_TILE
    n_sub = 1 if context else FFN_TILES_PER_STEP
    n_steps = 1 if context else (N_TILES - 1) // n_sub
    tile = lambda k, skip: (lambda b, t: (b, 0 if context else n_sub * t + k + skip, 0))
    weight = lambda shape: _layer_spec(l, shape, pipeline_mode=pl.Buffered(1))
    subs = range(n_sub)
    return pl.pallas_call(
        functools.partial(_out_ffn_kernel, n_sub=n_sub),
        grid=(BATCH, n_steps),
        in_specs=[
            *[pl.BlockSpec((1, T, ATTN_W), tile(k, 0)) for k in subs],
            *[pl.BlockSpec((1, T, LRU_W), tile(k, 1)) for k in subs],
            *[pl.BlockSpec((1, T, POOL_W), tile(k, 1)) for k in subs],
            pl.BlockSpec((1, n_sub * T, D_MODEL), lambda b, t: (b, t, 0)),
            pl.BlockSpec((None, 1, 6, D_MODEL), lambda b, t: (l, BATCH if context else b, 0, 0)),
            weight((4, D_MODEL)),
            weight((MIX_W, D_MODEL)),
            weight((D_MODEL, 2 * FFN_HIDDEN)),
            weight((FFN_HIDDEN, D_MODEL)),
        ],
        out_specs=pl.BlockSpec((1, n_sub * T, D_MODEL), lambda b, t: (b, t, 0)),
        out_shape=jax.ShapeDtypeStruct((BATCH, n_steps * n_sub * T, D_MODEL), F32),
        compiler_params=pltpu.CompilerParams(
            dimension_semantics=("arbitrary", "arbitrary"),
            vmem_limit_bytes=_vmem_limit(52 * 1024 * 1024)),
        name="out_ffn_ctx" if context else "out_ffn",
    )(*([att] * n_sub), *([rec] * n_sub), *([pool] * n_sub), x_src, mod, norm_g,
      w_out, w_ffn_in, w_ffn_out)


def _position_tables():
    half = ROPE_HALF
    freq = (ROPE_THETA ** (-np.arange(half, dtype=np.float32) / half)).astype(np.float32)
    p = np.arange(SEQ)
    ang_row = (p // GRID_W).astype(np.float32)[:, None] * freq
    ang_col = (p % GRID_W).astype(np.float32)[:, None] * freq
    ang = np.concatenate([ang_row, ang_row, ang_col, ang_col], axis=1)
    first = np.tile(np.arange(HEAD_DIM) % (2 * half) < half, (SEQ, 1))
    cos = np.cos(ang)
    sin_a = np.where(first, -np.sin(ang), 0.0)
    sin_b = np.where(first, 0.0, np.sin(ang))
    pad = lambda a, v: np.concatenate([np.full((CTX_LEN, HEAD_DIM), v), a], axis=0)
    two = lambda a: np.tile(a, (1, V7X_LANES // HEAD_DIM)).astype(np.float32)
    return two(pad(cos, 1.0)), two(pad(sin_a, 0.0)), two(pad(sin_b, 0.0))


def _head_mean_matrix(width):
    blk = np.kron(np.eye(width // HEAD_DIM), np.full((HEAD_DIM, HEAD_DIM), 1.0 / HEAD_DIM))
    return jnp.asarray(blk, dtype=BF16)


def _block_diag(w):
    n, c, d = w.shape[-3:]
    on_diag = jnp.eye(n, dtype=bool)[:, None, :, None]
    out = jnp.where(on_diag, w[..., :, :, None, :], 0.0)
    return out.reshape(w.shape[:-3] + (n * c, n * d))


def kernel(x, c, ctx, c_ctx, w_mod, b_mod, norm_g, w_in, q_norm_g, k_norm_g, lru_conv_w, lru_conv_b,
           lru_gate_w, lru_gate_b, lru_lambda, pool_w, pool_b, pool_scale, w_out, w_ffn_in, w_ffn_out):
    assert x.shape == (BATCH, SEQ, D_MODEL) and ctx.shape == (BATCH, CTX_LEN, D_MODEL)
    cos, sin_a, sin_b = (jnp.asarray(a) for a in _position_tables())
    cos_t, sin_t = cos[:, 0:HEAD_DIM].T, (sin_a + sin_b)[:, 0:HEAD_DIM].T
    consts = (_head_mean_matrix(KV_W), cos, sin_a, sin_b, cos_t, sin_t)

    c_all = jnp.zeros((MOD_ROWS, D_MODEL), F32).at[0:BATCH].set(c).at[BATCH].set(c_ctx)
    mod = _modulation(c_all, w_mod, b_mod).reshape(DEPTH, MOD_ROWS, 6, D_MODEL)

    w_in_b, w_out_b = w_in.astype(BF16), w_out.astype(BF16)
    w_ffn_in_b, w_ffn_out_b = w_ffn_in.astype(BF16), w_ffn_out.astype(BF16)
    gq = jnp.broadcast_to(q_norm_g[:, :, None], (DEPTH, HEAD_DIM, TOKEN_TILE))
    gk = jnp.tile(k_norm_g, (1, N_KV_HEADS))[:, None, :]
    gate_w = _block_diag(lru_gate_w)
    gate_w = gate_w.transpose(0, 3, 1, 2, 4).reshape(DEPTH, LRU_W, 4 * LRU_W).astype(BF16)
    gate_b = lru_gate_b.reshape(DEPTH, 1, 4 * LRU_W)
    pool_w_b = _block_diag(pool_w).astype(BF16)
    conv_b, pool_b3, pool_s3 = lru_conv_b[:, None, :], pool_b[:, None, :], pool_scale[:, None, :]

    for l in range(DEPTH):
        last = l == DEPTH - 1
        q, k, vt, rest = _in_projection(l, ctx, x, mod, norm_g, w_in_b, consts, gq, gk)
        att = _attention(q, k, vt, context=False)
        rec, pool = _lru_pool(l, rest, lru_conv_w, conv_b, gate_w, gate_b, lru_lambda,
                              pool_w_b, pool_b3, pool_s3)
        weights = (mod, norm_g, w_out_b, w_ffn_in_b, w_ffn_out_b)
        x_next = _out_ffn(l, att, rec, pool, x, *weights, context=False)
        if not last:
            att_ctx = _attention(q, k, vt, context=True)
            ctx = _out_ffn(l, att_ctx, rec, pool, ctx, *weights, context=True)
        x = x_next
    return x
```

```python
import functools

import numpy as np
import jax
import jax.numpy as jnp
from jax import lax
from jax.experimental import pallas as pl
from jax.experimental.pallas import tpu as pltpu

D_MODEL = 1024
BATCH = 4
SEQ = 4096
DEPTH = 2
GRID_W = 64
CTX_LEN = 256
LALL = CTX_LEN + SEQ

N_Q_HEADS = 8
N_KV_HEADS = 2
HEAD_DIM = 64
Q_GROUP = N_Q_HEADS // N_KV_HEADS
ATTN_W = N_Q_HEADS * HEAD_DIM
KV_W = N_KV_HEADS * HEAD_DIM
ROPE_THETA = 10000.0
LRU_W = D_MODEL // 4
LRU_BLOCKS = 4
LRU_BW = LRU_W // LRU_BLOCKS
CONV_W = 4
LRU_C = 8.0
POOL_W = D_MODEL // 4
POOL_GROUPS = 4
POOL_GW = POOL_W // POOL_GROUPS
POOL_WINDOWS = (2, 4, 8, 16)
MIX_W = ATTN_W + LRU_W + POOL_W
IN_W = ATTN_W + 2 * KV_W + 2 * LRU_W + POOL_W
REST_W = 2 * LRU_W + POOL_W
FFN_HIDDEN = -(-8 * D_MODEL // 768) * 256
RMS_EPS = 1e-6

V7X_LANES = 128
V7X_SUBLANES = 8
V7X_VMEM_BYTES = 64 * 1024 * 1024

TOKEN_TILE = CTX_LEN
N_TILES = LALL // TOKEN_TILE
MOD_ROWS = 8
MOD_TILE_N = 1536
ROPE_HALF = HEAD_DIM // 4
GROUPS_PER_TILE = TOKEN_TILE // V7X_SUBLANES
ONES_ROWS = 16
ATTN_Q_TILES = 4
KEY_CHUNK = TOKEN_TILE
SCORE_SLOTS = 3
FFN_TILES_PER_STEP = 4
FFN_CHUNK = 512
FFN_CHUNKS = [(lo, min(FFN_CHUNK, FFN_HIDDEN - lo)) for lo in range(0, FFN_HIDDEN, FFN_CHUNK)]
Q_SCALE = HEAD_DIM ** -0.5 * float(np.log2(np.e))

F32 = jnp.float32
BF16 = jnp.bfloat16


def _vmem_limit(nbytes):
    return int(min(max(nbytes, 16 * 1024 * 1024), V7X_VMEM_BYTES - 6 * 1024 * 1024))


def _rms(x):
    return x * lax.rsqrt(jnp.mean(x * x, axis=-1, keepdims=True) + RMS_EPS)


def _dot(a, b):
    return jnp.dot(a, b, preferred_element_type=F32)


def _layer_spec(l, shape, **kw):
    return pl.BlockSpec((None,) + shape, lambda *_: (l,) + (0,) * len(shape), **kw)


def _mod_row(b, t):
    return jnp.where(t == 0, BATCH, b)


def _token_specs(width):
    T = TOKEN_TILE
    ctx_spec = pl.BlockSpec((1, T, width), lambda b, t: (b, 0, 0))
    next_spec = pl.BlockSpec((1, T, width), lambda b, t: (b, jnp.minimum(t, N_TILES - 2), 0))
    return ctx_spec, next_spec


def _mod_kernel(c_ref, w_ref, b_ref, o_ref):
    c = c_ref[...]
    h = (c * jax.nn.sigmoid(c)).astype(BF16)
    o_ref[0] = _dot(h, w_ref[0].astype(BF16)) + b_ref[0]


def _modulation(c_all, w_mod, b_mod):
    n = 6 * D_MODEL
    return pl.pallas_call(
        _mod_kernel,
        grid=(DEPTH, n // MOD_TILE_N),
        in_specs=[
            pl.BlockSpec((MOD_ROWS, D_MODEL), lambda l, j: (0, 0)),
            pl.BlockSpec((1, D_MODEL, MOD_TILE_N), lambda l, j: (l, 0, j)),
            pl.BlockSpec((1, 1, MOD_TILE_N), lambda l, j: (l, 0, j)),
        ],
        out_specs=pl.BlockSpec((1, MOD_ROWS, MOD_TILE_N), lambda l, j: (l, 0, j)),
        out_shape=jax.ShapeDtypeStruct((DEPTH, MOD_ROWS, n), F32),
        compiler_params=pltpu.CompilerParams(
            dimension_semantics=("arbitrary", "arbitrary"),
            vmem_limit_bytes=_vmem_limit(4 * D_MODEL * MOD_TILE_N * 4)),
        name="modulation",
    )(c_all, w_mod, b_mod.reshape(DEPTH, 1, n))


def _head_norm_rope(z, s_ref, g, cos, sin_a, sin_b):
    z2 = z * z
    hi = z2.astype(BF16)
    lo = (z2 - hi.astype(F32)).astype(BF16)
    ms = _dot(hi, s_ref[...]) + _dot(lo, s_ref[...])
    zn = (z * lax.rsqrt(ms + RMS_EPS)) * g
    outs = []
    for c in range(z.shape[1] // V7X_LANES):
        zc = zn[:, c * V7X_LANES:(c + 1) * V7X_LANES]
        up = pltpu.roll(zc, V7X_LANES - ROPE_HALF, 1)
        dn = pltpu.roll(zc, ROPE_HALF, 1)
        outs.append(zc * cos + up * sin_a + dn * sin_b)
    return outs


def _head_norm_rope_t(zt, g, cos_t, sin_t):
    zn = (zt * lax.rsqrt(jnp.mean(zt * zt, axis=0, keepdims=True) + RMS_EPS)) * g
    h = ROPE_HALF
    partner = jnp.concatenate([zn[h:2 * h], zn[0:h], zn[3 * h:4 * h], zn[2 * h:3 * h]], axis=0)
    return zn * cos_t + partner * sin_t


def _inproj_kernel(ctx_ref, xn_ref, mod_ref, modn_ref, g_ref, w_ref, sk_ref, gq_ref, gk_ref,
                   cos_ref, sa_ref, sb_ref, cost_ref, sint_ref, q_ref, k_ref, vt_ref, rest_ref, h_ref):
    def norm_mod(x, mod):
        return ((_rms(x) * g_ref[0:1]) * (1.0 + mod[1:2]) + mod[0:1]).astype(BF16)

    @pl.when(pl.program_id(1) == 0)
    def _():
        h_ref[...] = norm_mod(ctx_ref[0], mod_ref[0])

    h = h_ref[...]
    qkv_w = ATTN_W + 2 * KV_W
    first_w = qkv_w + LRU_W
    y = _dot(h, w_ref[:, 0:first_w])
    (kc,) = _head_norm_rope(y[:, ATTN_W:ATTN_W + KV_W], sk_ref, gk_ref[...],
                            cos_ref[...], sa_ref[...], sb_ref[...])
    rest_ref[0, :, 0:LRU_W] = y[:, qkv_w:first_w]
    rest_ref[0, :, LRU_W:REST_W] = _dot(h, w_ref[:, first_w:IN_W])
    h_ref[...] = norm_mod(xn_ref[0], modn_ref[0])

    for c in range(ATTN_W // V7X_LANES):
        zt = y[:, c * V7X_LANES:(c + 1) * V7X_LANES].T
        for i in range(V7X_LANES // HEAD_DIM):
            qt = _head_norm_rope_t(zt[i * HEAD_DIM:(i + 1) * HEAD_DIM], gq_ref[...],
                                   cost_ref[...], sint_ref[...])
            q_ref[0, 2 * c + i, 0] = (qt * Q_SCALE).astype(BF16)

    kc = kc.astype(BF16)
    k_ref[0, 0] = kc[:, 0:HEAD_DIM]
    k_ref[0, 1] = kc[:, HEAD_DIM:2 * HEAD_DIM]

    vt = y[:, ATTN_W + KV_W:qkv_w].T.astype(BF16)
    vt_ref[0, 0, 0] = vt[0:HEAD_DIM]
    vt_ref[0, 1, 0] = vt[HEAD_DIM:2 * HEAD_DIM]


def _in_projection(l, ctx_src, x_src, mod, norm_g, w_in, consts, gq, gk):
    sk, cos, sin_a, sin_b, cos_t, sin_t = consts
    T = TOKEN_TILE
    full = lambda shape: pl.BlockSpec(shape, lambda b, t: (0,) * len(shape))
    tab = pl.BlockSpec((T, V7X_LANES), lambda b, t: (t, 0))
    return pl.pallas_call(
        _inproj_kernel,
        grid=(BATCH, N_TILES),
        in_specs=[
            *_token_specs(D_MODEL),
            pl.BlockSpec((None, 1, 6, D_MODEL), lambda b, t: (l, _mod_row(b, t), 0, 0)),
            pl.BlockSpec((None, 1, 6, D_MODEL), lambda b, t: (l, b, 0, 0)),
            _layer_spec(l, (4, D_MODEL)),
            _layer_spec(l, (D_MODEL, IN_W)),
            full((KV_W, KV_W)),
            _layer_spec(l, (HEAD_DIM, T)),
            _layer_spec(l, (1, KV_W)),
            tab, tab, tab,
            pl.BlockSpec((HEAD_DIM, T), lambda b, t: (0, t)),
            pl.BlockSpec((HEAD_DIM, T), lambda b, t: (0, t)),
        ],
        out_specs=[
            pl.BlockSpec((1, N_Q_HEADS, 1, HEAD_DIM, T), lambda b, t: (b, 0, t, 0, 0)),
            pl.BlockSpec((1, N_KV_HEADS, T, HEAD_DIM), lambda b, t: (b, 0, t, 0)),
            pl.BlockSpec((1, N_KV_HEADS, 1, HEAD_DIM, T), lambda b, t: (b, 0, t, 0, 0)),
            pl.BlockSpec((1, T, REST_W), lambda b, t: (b, t, 0)),
        ],
        out_shape=[
            jax.ShapeDtypeStruct((BATCH, N_Q_HEADS, N_TILES, HEAD_DIM, T), BF16),
            jax.ShapeDtypeStruct((BATCH, N_KV_HEADS, LALL, HEAD_DIM), BF16),
            jax.ShapeDtypeStruct((BATCH, N_KV_HEADS, N_TILES, HEAD_DIM, T), BF16),
            jax.ShapeDtypeStruct((BATCH, LALL, REST_W), F32),
        ],
        scratch_shapes=[pltpu.VMEM((T, D_MODEL), BF16)],
        compiler_params=pltpu.CompilerParams(
            dimension_semantics=("arbitrary", "arbitrary"),
            vmem_limit_bytes=_vmem_limit(40 * 1024 * 1024)),
        name="in_projection",
    )(ctx_src, x_src, mod, mod, norm_g, w_in, sk, gq, gk, cos, sin_a, sin_b, cos_t, sin_t)


def _attn_kernel(*refs, q_tiles, n_chunks):
    T = TOKEN_TILE
    q_refs = refs[:q_tiles]
    k_ref, vt_ref, o_ref, s_ref, m_ref, acc_ref = refs[q_tiles:]
    pieces = Q_GROUP // 2

    def scores(n, piece):
        i, c = divmod(n, n_chunks)
        qt = jnp.concatenate([q_refs[i][0, g, 0] for g in (2 * piece, 2 * piece + 1)], axis=1)
        cols = slice(2 * piece * T, 2 * (piece + 1) * T)
        s_ref[n % SCORE_SLOTS, :, cols] = _dot(k_ref[0, 0, c * KEY_CHUNK:(c + 1) * KEY_CHUNK, :], qt)

    def update(n, piece):
        i, c = divmod(n, n_chunks)
        tile, off = divmod(c * KEY_CHUNK, T)
        vt1 = jnp.concatenate([vt_ref[0, 0, tile][:, off:off + KEY_CHUNK],
                               jnp.ones((ONES_ROWS, KEY_CHUNK), BF16)], axis=0)
        for g in (2 * piece, 2 * piece + 1):
            cols = slice(g * T, (g + 1) * T)
            s = s_ref[n % SCORE_SLOTS, :, cols]
            m = m_ref[i, :, cols]
            m_new = jnp.maximum(m, jnp.max(s, axis=0, keepdims=True))
            alpha = jnp.exp2(m - m_new)
            p = jnp.exp2((s - m_new).astype(BF16))
            m_ref[i, :, cols] = m_new
            acc_ref[i, :, cols] = alpha * acc_ref[i, :, cols] + _dot(vt1, p)

    def finish(i):
        o = acc_ref[i, 0:HEAD_DIM] / acc_ref[i, HEAD_DIM:HEAD_DIM + 1]
        o = jnp.concatenate([o[:, g * T:(g + 1) * T] for g in range(Q_GROUP)], axis=0)
        o_ref[0, i * T:(i + 1) * T, :] = o.T.astype(BF16)

    def tile_block(i):
        if i > 0:
            finish(i - 1)
        for n in range(i * n_chunks, (i + 1) * n_chunks):
            for piece in range(pieces):
                if n + SCORE_SLOTS - 1 < q_tiles * n_chunks:
                    scores(n + SCORE_SLOTS - 1, piece)
                update(n, piece)

    m_ref[...] = jnp.full(m_ref.shape, -1e30, F32)
    acc_ref[...] = jnp.zeros(acc_ref.shape, F32)
    for n in range(min(SCORE_SLOTS - 1, n_chunks)):
        for piece in range(pieces):
            scores(n, piece)
    for i in range(q_tiles):
        pl.when(pl.program_id(2) >= -i)(functools.partial(tile_block, i))
    finish(q_tiles - 1)


def _attention(q, k, vt, *, context):
    T = TOKEN_TILE
    q_tiles = 1 if context else ATTN_Q_TILES
    n_steps = 1 if context else (N_TILES - 1) // q_tiles
    n_cols = Q_GROUP * T
    q_spec = lambda i: pl.BlockSpec(
        (1, Q_GROUP, 1, HEAD_DIM, T), lambda b, h, t: (b, h, 0 if context else q_tiles * t + i + 1, 0, 0))
    return pl.pallas_call(
        functools.partial(_attn_kernel, q_tiles=q_tiles, n_chunks=(CTX_LEN if context else LALL) // KEY_CHUNK),
        grid=(BATCH, N_KV_HEADS, n_steps),
        in_specs=[
            *[q_spec(i) for i in range(q_tiles)],
            pl.BlockSpec((1, 1, LALL, HEAD_DIM), lambda b, h, t: (b, h, 0, 0)),
            pl.BlockSpec((1, 1, N_TILES, HEAD_DIM, T), lambda b, h, t: (b, h, 0, 0, 0)),
        ],
        out_specs=pl.BlockSpec((1, q_tiles * T, Q_GROUP * HEAD_DIM), lambda b, h, t: (b, t, h)),
        out_shape=jax.ShapeDtypeStruct((BATCH, n_steps * q_tiles * T, ATTN_W), BF16),
        scratch_shapes=[
            pltpu.VMEM((SCORE_SLOTS, KEY_CHUNK, n_cols), F32),
            pltpu.VMEM((q_tiles, 1, n_cols), F32),
            pltpu.VMEM((q_tiles, HEAD_DIM + ONES_ROWS, n_cols), F32),
        ],
        compiler_params=pltpu.CompilerParams(
            dimension_semantics=("arbitrary", "arbitrary", "arbitrary"),
            vmem_limit_bytes=_vmem_limit(40 * 1024 * 1024)),
        name="attention_ctx" if context else "attention",
    )(*([q] * q_tiles), k, vt)


def _softplus(z):
    return jnp.maximum(z, 0.0) + jnp.log1p(jnp.exp(-jnp.abs(z)))


def _with_halo(ref, col, c):
    T = TOKEN_TILE
    r0 = pl.multiple_of(c * T, T)
    main = ref[0, pl.ds(r0, T), col:col + LRU_W]
    p0 = pl.multiple_of(jnp.maximum(r0 - V7X_SUBLANES, 0), V7X_SUBLANES)
    n0 = pl.multiple_of(jnp.minimum(r0 + T, LALL - V7X_SUBLANES), V7X_SUBLANES)
    prev = ref[0, pl.ds(p0, V7X_SUBLANES), col:col + LRU_W]
    nxt = ref[0, pl.ds(n0, V7X_SUBLANES), col:col + LRU_W]
    prev = jnp.where(c >= 2, prev, 0.0)
    nxt = jnp.where((c >= 1) & (c <= N_TILES - 2), nxt, 0.0)
    return main, jnp.concatenate([prev, main, nxt], axis=0)


def _shift_rows(ext, k):
    n = ext.shape[0]
    if k == 0:
        return ext[V7X_SUBLANES:V7X_SUBLANES + TOKEN_TILE]
    return pltpu.roll(ext, (-k) % n, 0)[V7X_SUBLANES:V7X_SUBLANES + TOKEN_TILE]


def _lru_pool_kernel(rest_ref, cw_ref, cb_ref, gw_ref, gb_ref, lam_ref, pw_ref, pb_ref, ps_ref,
                     rec_ref, pool_ref, hf_ref, hb_ref, a_ref, u_ref, conv_ref):
    T = TOKEN_TILE
    S = V7X_SUBLANES
    grouped = (GROUPS_PER_TILE, S, LRU_W)
    row_in_group = lax.broadcasted_iota(jnp.int32, grouped, 1)

    def conv_step(c, _):
        main, ext = _with_halo(rest_ref, 0, c)
        conv_ref[pl.ds(pl.multiple_of(c * T, T), T), :] = cb_ref[...] + (
            cw_ref[0:1] * _shift_rows(ext, -2) + cw_ref[1:2] * _shift_rows(ext, -1)
            + cw_ref[2:3] * main + cw_ref[3:4] * _shift_rows(ext, 1))
        return 0

    lax.fori_loop(0, N_TILES, conv_step, 0)

    def local_scan(c, d):
        u = conv_ref[pl.ds(pl.multiple_of(c * T, T), T), :]
        g = _dot(u.astype(BF16), gw_ref[:, d * 2 * LRU_W:(d + 1) * 2 * LRU_W])
        g = jax.nn.sigmoid(g + gb_ref[:, d * 2 * LRU_W:(d + 1) * 2 * LRU_W])
        log_a = (-LRU_C * g[:, 0:LRU_W]) * _softplus(-lam_ref[d:d + 1])
        a = jnp.exp(log_a)
        neg_expm1 = (1.0 - a) * (1.0 + a)
        root = jnp.where(neg_expm1 > 0.0, neg_expm1 * lax.rsqrt(neg_expm1), 0.0)
        v = root * (g[:, LRU_W:2 * LRU_W] * u)
        a, v = a.reshape(grouped), v.reshape(grouped)
        for step in (1, 2, 4):
            if d == 0:
                a_s, v_s = pltpu.roll(a, step, 1), pltpu.roll(v, step, 1)
                keep = row_in_group >= step
            else:
                a_s, v_s = pltpu.roll(a, S - step, 1), pltpu.roll(v, S - step, 1)
                keep = row_in_group < S - step
            v = jnp.where(keep, a * v_s + v, v)
            a = jnp.where(keep, a * a_s, a)
        a_ref[d] = a.reshape(T, LRU_W)
        u_ref[d] = v.reshape(T, LRU_W)

    def tile_step(i, carry):
        h_f, h_b = carry
        c_f = i
        c_b = jnp.where(i == 0, 0, N_TILES - i)
        local_scan(c_f, 0)
        local_scan(c_b, 1)
        r_f = pl.multiple_of(c_f * T, T)
        r_b = pl.multiple_of(c_b * T, T)
        for gi in range(GROUPS_PER_TILE):
            lo = gi * S
            a, v = a_ref[0, lo:lo + S], u_ref[0, lo:lo + S]
            hf_ref[pl.ds(r_f + lo, S), :] = v + a * h_f
            h_f = (jnp.broadcast_to(v[S - 1:S], (S, LRU_W))
                   + jnp.broadcast_to(a[S - 1:S], (S, LRU_W)) * h_f)
            lo = (GROUPS_PER_TILE - 1 - gi) * S
            a, v = a_ref[1, lo:lo + S], u_ref[1, lo:lo + S]
            hb_ref[pl.ds(r_b + lo, S), :] = v + a * h_b
            h_b = (jnp.broadcast_to(v[0:1], (S, LRU_W))
                   + jnp.broadcast_to(a[0:1], (S, LRU_W)) * h_b)
        return h_f, h_b

    zeros = jnp.zeros((S, LRU_W), F32)
    lax.fori_loop(0, N_TILES, tile_step, (zeros, zeros))

    lane = lax.broadcasted_iota(jnp.int32, (T, POOL_W), 1)
    half_win = jnp.where(lane < POOL_GW, POOL_WINDOWS[0] // 2,
                         jnp.where(lane < 2 * POOL_GW, POOL_WINDOWS[1] // 2,
                                   jnp.where(lane < 3 * POOL_GW, POOL_WINDOWS[2] // 2,
                                             POOL_WINDOWS[3] // 2)))
    row = lax.broadcasted_iota(jnp.int32, (T, POOL_W), 0)

    def out_step(c, _):
        r0 = pl.multiple_of(c * T, T)
        gate = rest_ref[0, pl.ds(r0, T), LRU_W:2 * LRU_W]
        h = hf_ref[pl.ds(r0, T), :] + hb_ref[pl.ds(r0, T), :]
        rec_ref[0, pl.ds(r0, T), :] = (jax.nn.gelu(gate) * h).astype(BF16)

        px, ext = _with_halo(rest_ref, 2 * LRU_W, c)
        p2 = ext + pltpu.roll(ext, 1, 0)
        p4 = p2 + pltpu.roll(p2, 2, 0)
        p8 = p4 + pltpu.roll(p4, 4, 0)
        p16 = p8 + pltpu.roll(p8, 8, 0)
        win = jnp.where(lane < POOL_GW, _shift_rows(p2, 0),
                        jnp.where(lane < 2 * POOL_GW, _shift_rows(p4, 1),
                                  jnp.where(lane < 3 * POOL_GW, _shift_rows(p8, 3),
                                            _shift_rows(p16, 7))))
        seg_len = jnp.where(c == 0, CTX_LEN, SEQ)
        pos = row + jnp.where(c == 0, 0, r0 - CTX_LEN)
        cnt = jnp.minimum(pos + half_win, seg_len) - jnp.maximum(pos - half_win, 0)
        d = win / cnt.astype(F32) - px
        y = _dot(d.astype(BF16), pw_ref[...]) + pb_ref[...]
        pool_ref[0, pl.ds(r0, T), :] = (y * ps_ref[...]).astype(BF16)
        return 0

    lax.fori_loop(0, N_TILES, out_step, 0)


def _lru_pool(l, rest, conv_w, conv_b, gate_w, gate_b, lam, pool_w, pool_b, pool_scale):
    seq = lambda w: pl.BlockSpec((1, LALL, w), lambda b: (b, 0, 0))
    return pl.pallas_call(
        _lru_pool_kernel,
        grid=(BATCH,),
        in_specs=[
            seq(REST_W),
            _layer_spec(l, (CONV_W, LRU_W)), _layer_spec(l, (1, LRU_W)),
            _layer_spec(l, (LRU_W, 4 * LRU_W)), _layer_spec(l, (1, 4 * LRU_W)), _layer_spec(l, (2, LRU_W)),
            _layer_spec(l, (POOL_W, POOL_W)), _layer_spec(l, (1, POOL_W)), _layer_spec(l, (1, POOL_W)),
        ],
        out_specs=[seq(LRU_W), seq(POOL_W)],
        out_shape=[jax.ShapeDtypeStruct((BATCH, LALL, LRU_W), BF16),
                   jax.ShapeDtypeStruct((BATCH, LALL, POOL_W), BF16)],
        scratch_shapes=[
            pltpu.VMEM((LALL, LRU_W), F32), pltpu.VMEM((LALL, LRU_W), F32),
            pltpu.VMEM((2, TOKEN_TILE, LRU_W), F32), pltpu.VMEM((2, TOKEN_TILE, LRU_W), F32),
            pltpu.VMEM((LALL, LRU_W), F32),
        ],
        compiler_params=pltpu.CompilerParams(
            dimension_semantics=("arbitrary",),
            vmem_limit_bytes=_vmem_limit(56 * 1024 * 1024)),
        name="lru_pool",
    )(rest, conv_w, conv_b, gate_w, gate_b, lam, pool_w, pool_b, pool_scale)


def _out_ffn_kernel(*refs, n_sub):
    T = TOKEN_TILE
    att, rec, pool = refs[0:n_sub], refs[n_sub:2 * n_sub], refs[2 * n_sub:3 * n_sub]
    x_ref, mod_ref, g_ref, wo_ref, wi_ref, wf_ref, o_ref = refs[3 * n_sub:]
    mod = mod_ref[0]
    ys = [_dot(att[k][0], wo_ref[0:ATTN_W])
          + _dot(rec[k][0], wo_ref[ATTN_W:ATTN_W + LRU_W])
          + _dot(pool[k][0], wo_ref[ATTN_W + LRU_W:MIX_W]) for k in range(n_sub)]
    for k in range(n_sub):
        rows = slice(k * T, (k + 1) * T)
        x = x_ref[0, rows] + mod[2:3] * (_rms(ys[k]) * g_ref[1:2])
        h = ((_rms(x) * g_ref[2:3]) * (1.0 + mod[4:5]) + mod[3:4]).astype(BF16)
        gate_up = lambda lo, n: (_dot(h, wi_ref[:, lo:lo + n]),
                                 _dot(h, wi_ref[:, FFN_HIDDEN + lo:FFN_HIDDEN + lo + n]))
        gu = gate_up(*FFN_CHUNKS[0])
        f = None
        for i, (lo, n) in enumerate(FFN_CHUNKS):
            gu_next = gate_up(*FFN_CHUNKS[i + 1]) if i + 1 < len(FFN_CHUNKS) else None
            g, u = gu
            part = _dot((g * jax.nn.sigmoid(g) * u).astype(BF16), wf_ref[lo:lo + n])
            f = part if f is None else f + part
            gu = gu_next
        o_ref[0, rows] = x + mod[5:6] * (_rms(f) * g_ref[3:4])


def _out_ffn(l, att, rec, pool, x_src, mod, norm_g, w_out, w_ffn_in, w_ffn_out, *, context):
    T = TOKEN_TILE
    n_sub = BATCH if context else FFN_TILES_PER_STEP
    n_steps = 1 if context else (N_TILES - 1) // n_sub
    n_b = 1 if context else BATCH
    tile = lambda k, skip: (lambda b, t: (k, 0, 0) if context else (b, n_sub * t + k + skip, 0))
    weight = lambda shape: _layer_spec(l, shape, pipeline_mode=pl.Buffered(1))
    subs = range(n_sub)
    x_src = x_src.reshape(n_b, -1, D_MODEL)
    return pl.pallas_call(
        functools.partial(_out_ffn_kernel, n_sub=n_sub),
        grid=(n_b, n_steps),
        in_specs=[
            *[pl.BlockSpec((1, T, ATTN_W), tile(k, 0)) for k in subs],
            *[pl.BlockSpec((1, T, LRU_W), tile(k, 1)) for k in subs],
            *[pl.BlockSpec((1, T, POOL_W), tile(k, 1)) for k in subs],
            pl.BlockSpec((1, n_sub * T, D_MODEL), lambda b, t: (b, t, 0)),
            pl.BlockSpec((None, 1, 6, D_MODEL), lambda b, t: (l, BATCH if context else b, 0, 0)),
            weight((4, D_MODEL)),
            weight((MIX_W, D_MODEL)),
            weight((D_MODEL, 2 * FFN_HIDDEN)),
            weight((FFN_HIDDEN, D_MODEL)),
        ],
        out_specs=pl.BlockSpec((1, n_sub * T, D_MODEL), lambda b, t: (b, t, 0)),
        out_shape=jax.ShapeDtypeStruct((n_b, n_steps * n_sub * T, D_MODEL), F32),
        compiler_params=pltpu.CompilerParams(
            dimension_semantics=("arbitrary", "arbitrary"),
            vmem_limit_bytes=_vmem_limit(52 * 1024 * 1024)),
        name="out_ffn_ctx" if context else "out_ffn",
    )(*([att] * n_sub), *([rec] * n_sub), *([pool] * n_sub), x_src, mod, norm_g,
      w_out, w_ffn_in, w_ffn_out).reshape(BATCH, -1, D_MODEL)


def _position_tables():
    half = ROPE_HALF
    freq = (ROPE_THETA ** (-np.arange(half, dtype=np.float32) / half)).astype(np.float32)
    p = np.arange(SEQ)
    ang_row = (p // GRID_W).astype(np.float32)[:, None] * freq
    ang_col = (p % GRID_W).astype(np.float32)[:, None] * freq
    ang = np.concatenate([ang_row, ang_row, ang_col, ang_col], axis=1)
    first = np.tile(np.arange(HEAD_DIM) % (2 * half) < half, (SEQ, 1))
    cos = np.cos(ang)
    sin_a = np.where(first, -np.sin(ang), 0.0)
    sin_b = np.where(first, 0.0, np.sin(ang))
    pad = lambda a, v: np.concatenate([np.full((CTX_LEN, HEAD_DIM), v), a], axis=0)
    two = lambda a: np.tile(a, (1, V7X_LANES // HEAD_DIM)).astype(np.float32)
    return two(pad(cos, 1.0)), two(pad(sin_a, 0.0)), two(pad(sin_b, 0.0))


def _head_mean_matrix(width):
    blk = np.kron(np.eye(width // HEAD_DIM), np.full((HEAD_DIM, HEAD_DIM), 1.0 / HEAD_DIM))
    return jnp.asarray(blk, dtype=BF16)


def _block_diag(w):
    n, c, d = w.shape[-3:]
    on_diag = jnp.eye(n, dtype=bool)[:, None, :, None]
    out = jnp.where(on_diag, w[..., :, :, None, :], 0.0)
    return out.reshape(w.shape[:-3] + (n * c, n * d))


def kernel(x, c, ctx, c_ctx, w_mod, b_mod, norm_g, w_in, q_norm_g, k_norm_g, lru_conv_w, lru_conv_b,
           lru_gate_w, lru_gate_b, lru_lambda, pool_w, pool_b, pool_scale, w_out, w_ffn_in, w_ffn_out):
    assert x.shape == (BATCH, SEQ, D_MODEL) and ctx.shape == (BATCH, CTX_LEN, D_MODEL)
    cos, sin_a, sin_b = (jnp.asarray(a) for a in _position_tables())
    cos_t, sin_t = cos[:, 0:HEAD_DIM].T, (sin_a + sin_b)[:, 0:HEAD_DIM].T
    consts = (_head_mean_matrix(KV_W), cos, sin_a, sin_b, cos_t, sin_t)

    c_all = jnp.zeros((MOD_ROWS, D_MODEL), F32).at[0:BATCH].set(c).at[BATCH].set(c_ctx)
    mod = _modulation(c_all, w_mod, b_mod).reshape(DEPTH, MOD_ROWS, 6, D_MODEL)

    w_in_b, w_out_b = w_in.astype(BF16), w_out.astype(BF16)
    w_ffn_in_b, w_ffn_out_b = w_ffn_in.astype(BF16), w_ffn_out.astype(BF16)
    gq = jnp.broadcast_to(q_norm_g[:, :, None], (DEPTH, HEAD_DIM, TOKEN_TILE))
    gk = jnp.tile(k_norm_g, (1, N_KV_HEADS))[:, None, :]
    gate_w = _block_diag(lru_gate_w)
    gate_w = gate_w.transpose(0, 3, 1, 2, 4).reshape(DEPTH, LRU_W, 4 * LRU_W).astype(BF16)
    gate_b = lru_gate_b.reshape(DEPTH, 1, 4 * LRU_W)
    pool_w_b = _block_diag(pool_w).astype(BF16)
    conv_b, pool_b3, pool_s3 = lru_conv_b[:, None, :], pool_b[:, None, :], pool_scale[:, None, :]

    for l in range(DEPTH):
        last = l == DEPTH - 1
        q, k, vt, rest = _in_projection(l, ctx, x, mod, norm_g, w_in_b, consts, gq, gk)
        att = _attention(q, k, vt, context=False)
        rec, pool = _lru_pool(l, rest, lru_conv_w, conv_b, gate_w, gate_b, lru_lambda,
                              pool_w_b, pool_b3, pool_s3)
        weights = (mod, norm_g, w_out_b, w_ffn_in_b, w_ffn_out_b)
        x_next = _out_ffn(l, att, rec, pool, x, *weights, context=False)
        if not last:
            att_ctx = _attention(q, k, vt, context=True)
            ctx = _out_ffn(l, att_ctx, rec, pool, ctx, *weights, context=True)
        x = x_next
    return x
```
